```python
import math
import jax, jax.numpy as jnp
from jax import lax
import numpy as np

D_MODEL = 1024
BATCH = 2
SEQ = 16384
DEPTH = 2

N_META = 16
NORM_EPS = 1e-6

ATTN_HEAD_DIM = 64
ATTN_HEADS = D_MODEL // ATTN_HEAD_DIM
ATTN_KV_HEADS = ATTN_HEADS // 8
ATTN_GROUPS = ATTN_HEADS // ATTN_KV_HEADS
ATTN_WIDTH = ATTN_HEADS * ATTN_HEAD_DIM
ATTN_KV_WIDTH = ATTN_KV_HEADS * ATTN_HEAD_DIM
ATTN_IN = 2 * ATTN_WIDTH + 2 * ATTN_KV_WIDTH
WINDOW = 128
ATTN_BLOCK = 128

DN_HEAD_DIM_K = 128
DN_HEAD_DIM_V = 128
DN_K_HEADS = D_MODEL // DN_HEAD_DIM_K
DN_V_HEADS = 2 * DN_K_HEADS
DN_KEY_WIDTH = DN_K_HEADS * DN_HEAD_DIM_K
DN_VALUE_WIDTH = DN_V_HEADS * DN_HEAD_DIM_V
DN_CONV = 4
DN_CHUNK = 64
DN_CONV_WIDTH = 2 * DN_KEY_WIDTH + DN_VALUE_WIDTH
DN_IN = DN_CONV_WIDTH + DN_VALUE_WIDTH + 2 * DN_V_HEADS

N_ATTN_LAYERS = (DEPTH + 1) // 2
N_DN_LAYERS = DEPTH // 2

kernel_name = "hybrid_swa_sink_alibi_gated_deltanet_meta"


def rms_norm(x, w):
    xf = x.astype(jnp.float32)
    y = xf * lax.rsqrt(jnp.mean(xf * xf, axis=-1, keepdims=True) + NORM_EPS)
    return (y * w.astype(jnp.float32)).astype(x.dtype)


def l2_norm(x):
    xf = x.astype(jnp.float32)
    return xf * lax.rsqrt(jnp.sum(xf * xf, axis=-1, keepdims=True) + NORM_EPS)


def alibi_slopes(n_heads):
    return jnp.asarray(np.exp2(-8.0 * np.arange(1, n_heads + 1) / n_heads), dtype=jnp.float32)


def banded_sink_attention(q, k, v, sinks):
    B, L = q.shape[:2]
    pad = ATTN_BLOCK - N_META
    Lp = L + pad
    nb = Lp // ATTN_BLOCK
    padt = lambda t: jnp.pad(t, ((0, 0), (pad, 0), (0, 0), (0, 0)))
    qb = padt(q).reshape(B, nb, ATTN_BLOCK, ATTN_KV_HEADS, ATTN_GROUPS, ATTN_HEAD_DIM)
    kb = padt(k).reshape(B, nb, ATTN_BLOCK, ATTN_KV_HEADS, ATTN_HEAD_DIM)
    vb = padt(v).reshape(B, nb, ATTN_BLOCK, ATTN_KV_HEADS, ATTN_HEAD_DIM)
    prev = lambda t: jnp.pad(t, ((0, 0), (1, 0), (0, 0), (0, 0), (0, 0)))[:, :-1]
    k_band = jnp.concatenate([prev(kb), kb], axis=2)
    v_band = jnp.concatenate([prev(vb), vb], axis=2)
    k_meta = k[:, :N_META]
    v_meta = v[:, :N_META]

    scale = ATTN_HEAD_DIM ** -0.5
    s_band = jnp.einsum('bnqhgd,bnkhd->bnhgqk', qb, k_band,
                        preferred_element_type=jnp.float32) * scale
    s_meta = jnp.einsum('bnqhgd,bmhd->bnhgqm', qb, k_meta,
                        preferred_element_type=jnp.float32) * scale

    pos_q = jnp.arange(Lp, dtype=jnp.int32).reshape(nb, ATTN_BLOCK) - pad
    pos_kb = jnp.concatenate([pos_q - ATTN_BLOCK, pos_q], axis=-1)
    pos_meta = jnp.arange(N_META, dtype=jnp.int32)
    dist_band = pos_q[:, :, None] - pos_kb[:, None, :]
    valid_band = (pos_kb[:, None, :] >= N_META) & (dist_band >= 0) & (dist_band < WINDOW)
    dist_meta = pos_q[:, :, None] - pos_meta[None, None, :]
    valid_meta = dist_meta >= 0

    slopes = alibi_slopes(ATTN_HEADS).reshape(1, 1, ATTN_KV_HEADS, ATTN_GROUPS, 1, 1)
    clipdist = lambda d: jnp.minimum(d, WINDOW).astype(jnp.float32)[None, :, None, None]
    s_band = jnp.where(valid_band[None, :, None, None], s_band - slopes * clipdist(dist_band), -jnp.inf)
    s_meta = jnp.where(valid_meta[None, :, None, None], s_meta - slopes * clipdist(dist_meta), -jnp.inf)

    sink = jnp.broadcast_to(
        sinks.astype(jnp.float32).reshape(1, 1, ATTN_KV_HEADS, ATTN_GROUPS, 1, 1),
        s_band.shape[:-1] + (1,))
    p = jax.nn.softmax(jnp.concatenate([s_band, s_meta, sink], axis=-1), axis=-1)
    p_band = p[..., :2 * ATTN_BLOCK].astype(v.dtype)
    p_meta = p[..., 2 * ATTN_BLOCK:2 * ATTN_BLOCK + N_META].astype(v.dtype)
    o = (jnp.einsum('bnhgqk,bnkhd->bnqhgd', p_band, v_band)
         + jnp.einsum('bnhgqm,bmhd->bnqhgd', p_meta, v_meta))
    return o.reshape(B, Lp, ATTN_WIDTH)[:, pad:]


def attention_mixer(h, norm_w, w_in, q_norm_w, k_norm_w, sinks, w_out):
    B, L, _ = h.shape
    u = rms_norm(h, norm_w) @ w_in
    q, k, v, gate = jnp.split(
        u, [ATTN_WIDTH, ATTN_WIDTH + ATTN_KV_WIDTH, ATTN_WIDTH + 2 * ATTN_KV_WIDTH], axis=-1)
    q = rms_norm(q.reshape(B, L, ATTN_HEADS, ATTN_HEAD_DIM), q_norm_w)
    k = rms_norm(k.reshape(B, L, ATTN_KV_HEADS, ATTN_HEAD_DIM), k_norm_w)
    v = v.reshape(B, L, ATTN_KV_HEADS, ATTN_HEAD_DIM)
    o = banded_sink_attention(q, k, v, sinks)
    return (o * jax.nn.silu(gate)) @ w_out


def causal_depthwise_conv(x, w):
    K, C = w.shape
    return lax.conv_general_dilated(
        x, w[:, None, :], window_strides=(1,), padding=[(K - 1, 0)],
        dimension_numbers=('NWC', 'WIO', 'NWC'), feature_group_count=C)


def chunked_gated_delta_rule(q, k, v, beta, g):
    B, L, H, _ = q.shape
    pad = DN_CHUNK - N_META
    Lc = L + pad
    n = Lc // DN_CHUNK

    def to_chunks(t):
        t = jnp.pad(t.astype(jnp.float32), [(0, 0), (pad, 0)] + [(0, 0)] * (t.ndim - 2))
        t = t.reshape((B, n, DN_CHUNK) + t.shape[2:])
        return jnp.swapaxes(jnp.moveaxis(t, 1, 0), 2, 3)

    xs = (to_chunks(q), to_chunks(k), to_chunks(v), to_chunks(beta), to_chunks(g))
    causal = jnp.tril(jnp.ones((DN_CHUNK, DN_CHUNK), dtype=bool))
    strict = jnp.tril(jnp.ones((DN_CHUNK, DN_CHUNK), dtype=bool), -1)
    eye = jnp.eye(DN_CHUNK, dtype=jnp.float32)

    def step(S, inp):
        qc, kc, vc, bc, gc = inp
        gcum = jnp.cumsum(gc, axis=-1)
        decay = jnp.exp(jnp.where(causal, gcum[..., :, None] - gcum[..., None, :], -jnp.inf))
        kb = kc * bc[..., None]
        m = jnp.where(strict, jnp.einsum('bhcd,bhsd->bhcs', kb, kc) * decay, 0.0)
        rhs = jnp.concatenate([vc * bc[..., None], kb * jnp.exp(gcum)[..., None]], axis=-1)
        sol = lax.linalg.triangular_solve(m + eye, rhs, left_side=True, lower=True,
                                          unit_diagonal=True)
        u, w = sol[..., :DN_HEAD_DIM_V], sol[..., DN_HEAD_DIM_V:]
        v_new = u - jnp.einsum('bhcd,bhdv->bhcv', w, S)
        attn = jnp.einsum('bhcd,bhsd->bhcs', qc, kc) * decay
        o = (jnp.einsum('bhcd,bhdv->bhcv', qc * jnp.exp(gcum)[..., None], S)
             + jnp.einsum('bhcs,bhsv->bhcv', attn, v_new))
        g_last = gcum[..., -1]
        k_state = kc * jnp.exp(g_last[..., None] - gcum)[..., None]
        S = S * jnp.exp(g_last)[..., None, None] + jnp.einsum('bhcd,bhcv->bhdv', k_state, v_new)
        return S, o

    S0 = jnp.zeros((B, H, DN_HEAD_DIM_K, DN_HEAD_DIM_V), jnp.float32)
    _, o = lax.scan(step, S0, xs)
    o = jnp.moveaxis(jnp.swapaxes(o, 2, 3), 0, 1).reshape(B, Lc, H, DN_HEAD_DIM_V)
    return o[:, pad:]


def deltanet_mixer(h, norm_w, w_in, conv_w, a_log, dt_bias, o_norm_w, w_out):
    B, L, _ = h.shape
    u = rms_norm(h, norm_w) @ w_in
    qkv, z, b, a = jnp.split(
        u, [DN_CONV_WIDTH, DN_CONV_WIDTH + DN_VALUE_WIDTH,
            DN_CONV_WIDTH + DN_VALUE_WIDTH + DN_V_HEADS], axis=-1)
    qkv = jax.nn.silu(causal_depthwise_conv(qkv, conv_w))
    q, k, v = jnp.split(qkv, [DN_KEY_WIDTH, 2 * DN_KEY_WIDTH], axis=-1)
    rep = DN_V_HEADS // DN_K_HEADS
    q = jnp.repeat(l2_norm(q.reshape(B, L, DN_K_HEADS, DN_HEAD_DIM_K)), rep, axis=2)
    k = jnp.repeat(l2_norm(k.reshape(B, L, DN_K_HEADS, DN_HEAD_DIM_K)), rep, axis=2)
    q = q * (DN_HEAD_DIM_K ** -0.5)
    v = v.reshape(B, L, DN_V_HEADS, DN_HEAD_DIM_V)
    beta = jax.nn.sigmoid(b.astype(jnp.float32))
    g = -jnp.exp(a_log.astype(jnp.float32)) * jax.nn.softplus(
        a.astype(jnp.float32) + dt_bias.astype(jnp.float32))
    o = chunked_gated_delta_rule(q, k, v, beta, g).astype(h.dtype)
    o = rms_norm(o, o_norm_w) * jax.nn.silu(z.reshape(B, L, DN_V_HEADS, DN_HEAD_DIM_V))
    return o.reshape(B, L, DN_VALUE_WIDTH) @ w_out


def setup_inputs(seed: int = 0) -> dict:
    key = jax.random.key(seed)
    ks = jax.random.split(key, 20)
    f32 = jnp.float32
    nA, nB = N_ATTN_LAYERS, N_DN_LAYERS
    out_scale = 0.5
    dt = jnp.exp(jax.random.uniform(ks[13], (nB, DN_V_HEADS), f32,
                                    math.log(1e-3), math.log(1e-1)))
    return {
        "x": jax.random.normal(ks[0], (BATCH, SEQ, D_MODEL), f32),
        "meta_tokens": jax.random.normal(ks[1], (N_META, D_MODEL), f32),
        "attn_norm_w": 1.0 + 0.02 * jax.random.normal(ks[2], (nA, D_MODEL), f32),
        "attn_w_in": jax.random.normal(ks[3], (nA, D_MODEL, ATTN_IN), f32) * D_MODEL ** -0.5,
        "attn_q_norm_w": 1.0 + 0.02 * jax.random.normal(ks[4], (nA, ATTN_HEAD_DIM), f32),
        "attn_k_norm_w": 1.0 + 0.02 * jax.random.normal(ks[5], (nA, ATTN_HEAD_DIM), f32),
        "attn_sinks": 0.5 * jax.random.normal(ks[6], (nA, ATTN_HEADS), f32),
        "attn_w_out": jax.random.normal(ks[7], (nA, ATTN_WIDTH, D_MODEL), f32)
                      * ATTN_WIDTH ** -0.5 * out_scale,
        "dn_norm_w": 1.0 + 0.02 * jax.random.normal(ks[8], (nB, D_MODEL), f32),
        "dn_w_in": jax.random.normal(ks[9], (nB, D_MODEL, DN_IN), f32) * D_MODEL ** -0.5,
        "dn_conv_w": jax.random.normal(ks[10], (nB, DN_CONV, DN_CONV_WIDTH), f32) * DN_CONV ** -0.5,
        "dn_a_log": jnp.log(jax.random.uniform(ks[11], (nB, DN_V_HEADS), f32, 1.0, 16.0)),
        "dn_dt_bias": dt + jnp.log(-jnp.expm1(-dt)),
        "dn_o_norm_w": 1.0 + 0.02 * jax.random.normal(ks[12], (nB, DN_HEAD_DIM_V), f32),
        "dn_w_out": jax.random.normal(ks[14], (nB, DN_VALUE_WIDTH, D_MODEL), f32)
                    * DN_VALUE_WIDTH ** -0.5 * out_scale,
    }


def reference(x, meta_tokens, attn_norm_w, attn_w_in, attn_q_norm_w, attn_k_norm_w,
              attn_sinks, attn_w_out, dn_norm_w, dn_w_in, dn_conv_w, dn_a_log,
              dn_dt_bias, dn_o_norm_w, dn_w_out):
    B = x.shape[0]
    meta = jnp.broadcast_to(meta_tokens.astype(x.dtype)[None], (B, N_META, x.shape[-1]))
    h = jnp.concatenate([meta, x], axis=1)
    for i in range(DEPTH):
        j = i // 2
        if i % 2 == 0:
            h = h + attention_mixer(h, attn_norm_w[j], attn_w_in[j], attn_q_norm_w[j],
                                    attn_k_norm_w[j], attn_sinks[j], attn_w_out[j])
        else:
            h = h + deltanet_mixer(h, dn_norm_w[j], dn_w_in[j], dn_conv_w[j], dn_a_log[j],
                                   dn_dt_bias[j], dn_o_norm_w[j], dn_w_out[j])
    return h[:, N_META:]
```

```python
import numpy as np
import jax
import jax.numpy as jnp
from jax import lax
from jax.experimental import pallas as pl
from jax.experimental.pallas import tpu as pltpu

F32 = jnp.float32
BF16 = jnp.bfloat16

N_META = 16
NORM_EPS = 1e-6

ATTN_HEAD_DIM = 64
ATTN_HEADS = 16
ATTN_KV_HEADS = 2
ATTN_GROUPS = ATTN_HEADS // ATTN_KV_HEADS
ATTN_WIDTH = ATTN_HEADS * ATTN_HEAD_DIM
ATTN_KV_WIDTH = ATTN_KV_HEADS * ATTN_HEAD_DIM
WINDOW = 128
ATTN_BLOCK = 128

DN_HEAD_DIM = 128
DN_K_HEADS = 8
DN_V_HEADS = 16
DN_KEY_WIDTH = DN_K_HEADS * DN_HEAD_DIM
DN_VALUE_WIDTH = DN_V_HEADS * DN_HEAD_DIM
DN_CONV = 4
DN_CHUNK = 64
DN_CONV_WIDTH = 2 * DN_KEY_WIDTH + DN_VALUE_WIDTH

LANES = 128
SUBLANES = 8
ROW_TILE = 256
PAD = ROW_TILE - N_META
META_BLOCK = PAD // ATTN_BLOCK
META_OFFSET = PAD % ATTN_BLOCK
CONV_COLS = 512
VMEM_LIMIT = 56 * 1024 * 1024

NEG_BIG = -1e30
CLIP_INVALID = 1e30


def _alibi_slopes(n_heads):
    return np.exp2(-8.0 * np.arange(1, n_heads + 1) / n_heads).astype(np.float32)


def _bdot(a, b):
    return jnp.dot(a.astype(BF16), b.astype(BF16), preferred_element_type=F32)


def _bdot_nt(a, b):
    return lax.dot_general(a.astype(BF16), b.astype(BF16), (((1,), (1,)), ((), ())),
                           preferred_element_type=F32)


def _rms_rows(x, w):
    return x * lax.rsqrt(jnp.mean(x * x, axis=-1, keepdims=True) + NORM_EPS) * w


def _silu(x):
    return x * jax.nn.sigmoid(x)


def _attn_in_kernel(h_ref, nw_ref, w_ref, q_ref, kv_ref, g_ref):
    xn = _rms_rows(h_ref[...], nw_ref[...]).astype(BF16)
    q_ref[...] = jnp.dot(xn, w_ref[:, :ATTN_WIDTH], preferred_element_type=F32).astype(BF16)
    kv_ref[...] = jnp.dot(xn, w_ref[:, ATTN_WIDTH:ATTN_WIDTH + 2 * ATTN_KV_WIDTH],
                          preferred_element_type=F32).astype(BF16)
    g_ref[...] = jnp.dot(xn, w_ref[:, ATTN_WIDTH + 2 * ATTN_KV_WIDTH:],
                         preferred_element_type=F32).astype(BF16)


def _attn_in(h2d, norm_w, w_in):
    rows, d = h2d.shape
    n_in = w_in.shape[1]
    return pl.pallas_call(
        _attn_in_kernel,
        grid=(rows // ROW_TILE,),
        in_specs=[pl.BlockSpec((ROW_TILE, d), lambda i: (i, 0)),
                  pl.BlockSpec((1, d), lambda i: (0, 0)),
                  pl.BlockSpec((d, n_in), lambda i: (0, 0))],
        out_specs=[pl.BlockSpec((ROW_TILE, ATTN_WIDTH), lambda i: (i, 0)),
                   pl.BlockSpec((ROW_TILE, 2 * ATTN_KV_WIDTH), lambda i: (i, 0)),
                   pl.BlockSpec((ROW_TILE, ATTN_WIDTH), lambda i: (i, 0))],
        out_shape=[jax.ShapeDtypeStruct((rows, ATTN_WIDTH), BF16),
                   jax.ShapeDtypeStruct((rows, 2 * ATTN_KV_WIDTH), BF16),
                   jax.ShapeDtypeStruct((rows, ATTN_WIDTH), BF16)],
        compiler_params=pltpu.CompilerParams(dimension_semantics=("parallel",),
                                             vmem_limit_bytes=VMEM_LIMIT),
        name="attn_in",
    )(h2d, norm_w, w_in)


def _attn_kernel(sink_ref, q_ref, kvm_ref, kvp_ref, kvc_ref, gate_ref, qnw_ref, knw_ref, o_ref):
    nref = pl.program_id(1) - META_BLOCK
    qi = lax.broadcasted_iota(jnp.int32, (ATTN_BLOCK, ATTN_BLOCK), 0)
    c = lax.broadcasted_iota(jnp.int32, (ATTN_BLOCK, ATTN_BLOCK), 1)
    dist_m = nref * ATTN_BLOCK + qi - c
    valid_m = jnp.logical_and(c >= META_OFFSET, dist_m >= 0)
    dist_p = ATTN_BLOCK + qi - c
    valid_p = jnp.logical_and(c > qi, nref >= 2)
    dist_c = qi - c
    valid_c = jnp.logical_and(c <= qi, nref >= 1)

    def clip(dist, valid):
        return jnp.where(valid, jnp.minimum(dist, WINDOW).astype(F32), CLIP_INVALID)

    clipdist = jnp.concatenate([clip(dist_m, valid_m), clip(dist_p, valid_p),
                                clip(dist_c, valid_c)], axis=1)
    slopes = _alibi_slopes(ATTN_HEADS)
    scale = ATTN_HEAD_DIM ** -0.5
    qnw = qnw_ref[...]
    knw = knw_ref[...]

    for h in range(ATTN_KV_HEADS):
        ks = slice(h * ATTN_HEAD_DIM, (h + 1) * ATTN_HEAD_DIM)
        vs = slice(ATTN_KV_WIDTH + h * ATTN_HEAD_DIM, ATTN_KV_WIDTH + (h + 1) * ATTN_HEAD_DIM)
        k_ext = jnp.concatenate([kvm_ref[:, ks], kvp_ref[:, ks], kvc_ref[:, ks]], axis=0)
        k_ext = _rms_rows(k_ext.astype(F32), knw).astype(BF16)
        v_ext = jnp.concatenate([kvm_ref[:, vs], kvp_ref[:, vs], kvc_ref[:, vs]], axis=0)
        heads = [h * ATTN_GROUPS + g for g in range(ATTN_GROUPS)]
        qs = jnp.concatenate(
            [q_ref[:, hq * ATTN_HEAD_DIM:(hq + 1) * ATTN_HEAD_DIM] for hq in heads], axis=0)
        qs = (_rms_rows(qs.astype(F32), qnw) * scale).astype(BF16)
        s = _bdot_nt(qs, k_ext)
        s = jnp.concatenate(
            [s[g * ATTN_BLOCK:(g + 1) * ATTN_BLOCK] - float(slopes[hq]) * clipdist
             for g, hq in enumerate(heads)], axis=0)
        sink = jnp.concatenate(
            [jnp.full((ATTN_BLOCK, 1), sink_ref[hq], F32) for hq in heads], axis=0)
        m = jnp.maximum(jnp.max(s, axis=-1, keepdims=True), sink)
        p = jnp.exp(s - m)
        denom = jnp.sum(p, axis=-1, keepdims=True) + jnp.exp(sink - m)
        o = _bdot(p, v_ext) / denom
        for j in range(ATTN_GROUPS // 2):
            pair = jnp.concatenate([o[(2 * j) * ATTN_BLOCK:(2 * j + 1) * ATTN_BLOCK],
                                    o[(2 * j + 1) * ATTN_BLOCK:(2 * j + 2) * ATTN_BLOCK]], axis=1)
            cols = slice(heads[2 * j] * ATTN_HEAD_DIM, (heads[2 * j] + 2) * ATTN_HEAD_DIM)
            o_ref[:, cols] = (pair * _silu(gate_ref[:, cols].astype(F32))).astype(BF16)


def _attention(q, kv, gate, sinks, qnw, knw):
    b, lp, _ = q.shape
    nb = lp // ATTN_BLOCK
    kv_block = (None, ATTN_BLOCK, 2 * ATTN_KV_WIDTH)
    wide_block = (None, ATTN_BLOCK, ATTN_WIDTH)
    return pl.pallas_call(
        _attn_kernel,
        grid=(b, nb),
        in_specs=[pl.BlockSpec(memory_space=pltpu.SMEM),
                  pl.BlockSpec(wide_block, lambda bi, n: (bi, n, 0)),
                  pl.BlockSpec(kv_block, lambda bi, n: (bi, META_BLOCK, 0)),
                  pl.BlockSpec(kv_block, lambda bi, n: (bi, jnp.maximum(n - 1, 0), 0)),
                  pl.BlockSpec(kv_block, lambda bi, n: (bi, n, 0)),
                  pl.BlockSpec(wide_block, lambda bi, n: (bi, n, 0)),
                  pl.BlockSpec((1, ATTN_HEAD_DIM), lambda bi, n: (0, 0)),
                  pl.BlockSpec((1, ATTN_HEAD_DIM), lambda bi, n: (0, 0))],
        out_specs=pl.BlockSpec(wide_block, lambda bi, n: (bi, n, 0)),
        out_shape=jax.ShapeDtypeStruct((b, lp, ATTN_WIDTH), BF16),
        compiler_params=pltpu.CompilerParams(dimension_semantics=("parallel", "parallel"),
                                             vmem_limit_bytes=VMEM_LIMIT),
        name="attn_core",
    )(sinks, q, kv, kv, kv, gate, qnw, knw)


def _attn_out_kernel(h_ref, og_ref, w_ref, o_ref):
    o_ref[...] = h_ref[...] + jnp.dot(og_ref[...], w_ref[...], preferred_element_type=F32)


def _attn_out(h2d, og2d, w_out):
    rows, d = h2d.shape
    kdim = og2d.shape[1]
    return pl.pallas_call(
        _attn_out_kernel,
        grid=(rows // ROW_TILE,),
        in_specs=[pl.BlockSpec((ROW_TILE, d), lambda i: (i, 0)),
                  pl.BlockSpec((ROW_TILE, kdim), lambda i: (i, 0)),
                  pl.BlockSpec((kdim, d), lambda i: (0, 0))],
        out_specs=pl.BlockSpec((ROW_TILE, d), lambda i: (i, 0)),
        out_shape=jax.ShapeDtypeStruct((rows, d), F32),
        compiler_params=pltpu.CompilerParams(dimension_semantics=("parallel",),
                                             vmem_limit_bytes=VMEM_LIMIT),
        name="attn_out",
    )(h2d, og2d, w_out)


def _dn_in_kernel(h_ref, nw_ref, wqkv_ref, wz_ref, wba_ref, convw_ref, alog_ref, dtb_ref,
                  q_ref, k_ref, v_ref, z_ref, bg_ref, buf_ref):
    i = pl.program_id(1)
    halo = SUBLANES

    @pl.when(i == 0)
    def _():
        buf_ref[0:halo, :] = jnp.zeros((halo, DN_CONV_WIDTH), F32)

    xn = _rms_rows(h_ref[...], nw_ref[...]).astype(BF16)
    z_ref[...] = jnp.dot(xn, wz_ref[...], preferred_element_type=F32).astype(BF16)

    ba = jnp.dot(xn, wba_ref[...], preferred_element_type=F32)
    lane = lax.broadcasted_iota(jnp.int32, ba.shape, 1)
    row = i * ROW_TILE + lax.broadcasted_iota(jnp.int32, ba.shape, 0)
    x = ba + dtb_ref[...]
    softplus = jnp.maximum(x, 0.0) + jnp.log1p(jnp.exp(-jnp.abs(x)))
    g = -jnp.exp(alog_ref[...]) * softplus
    bg = jnp.where(lane < DN_V_HEADS, jax.nn.sigmoid(ba), g)
    bg_ref[...] = jnp.where(row >= PAD, bg, 0.0)

    q_scale = DN_HEAD_DIM ** -0.5
    for cj in range(DN_CONV_WIDTH // CONV_COLS):
        cols = slice(cj * CONV_COLS, (cj + 1) * CONV_COLS)
        buf_ref[halo:halo + ROW_TILE, cols] = jnp.dot(xn, wqkv_ref[:, cols],
                                                      preferred_element_type=F32)
        acc = None
        for j in range(DN_CONV):
            start = halo - (DN_CONV - 1) + j
            term = convw_ref[j:j + 1, cols] * buf_ref[start:start + ROW_TILE, cols]
            acc = term if acc is None else acc + term
        buf_ref[0:halo, cols] = buf_ref[ROW_TILE:ROW_TILE + halo, cols]
        y = _silu(acc)
        col0 = cj * CONV_COLS
        if col0 < 2 * DN_KEY_WIDTH:
            dst, base, mul = ((q_ref, col0, q_scale) if col0 < DN_KEY_WIDTH
                              else (k_ref, col0 - DN_KEY_WIDTH, 1.0))
            for hh in range(CONV_COLS // DN_HEAD_DIM):
                yh = y[:, hh * DN_HEAD_DIM:(hh + 1) * DN_HEAD_DIM]
                inv = lax.rsqrt(jnp.sum(yh * yh, axis=-1, keepdims=True) + NORM_EPS) * mul
                dst[:, base + hh * DN_HEAD_DIM:base + (hh + 1) * DN_HEAD_DIM] = (yh * inv).astype(BF16)
        else:
            base = col0 - 2 * DN_KEY_WIDTH
            v_ref[:, base:base + CONV_COLS] = y.astype(BF16)


def _dn_in(h, norm_w, w_qkv, w_z, w_ba, conv_w, alog_row, dtb_row):
    b, lp, d = h.shape
    nt = lp // ROW_TILE
    row_block = lambda width: pl.BlockSpec((None, ROW_TILE, width), lambda bi, i: (bi, i, 0))
    full = lambda shape: pl.BlockSpec(shape, lambda bi, i: (0,) * len(shape))
    return pl.pallas_call(
        _dn_in_kernel,
        grid=(b, nt),
        in_specs=[row_block(d), full((1, d)), full(w_qkv.shape), full(w_z.shape), full(w_ba.shape),
                  full(conv_w.shape), full((1, LANES)), full((1, LANES))],
        out_specs=[row_block(DN_KEY_WIDTH), row_block(DN_KEY_WIDTH), row_block(DN_VALUE_WIDTH),
                   row_block(DN_VALUE_WIDTH), row_block(LANES)],
        out_shape=[jax.ShapeDtypeStruct((b, lp, DN_KEY_WIDTH), BF16),
                   jax.ShapeDtypeStruct((b, lp, DN_KEY_WIDTH), BF16),
                   jax.ShapeDtypeStruct((b, lp, DN_VALUE_WIDTH), BF16),
                   jax.ShapeDtypeStruct((b, lp, DN_VALUE_WIDTH), BF16),
                   jax.ShapeDtypeStruct((b, lp, LANES), F32)],
        scratch_shapes=[pltpu.VMEM((ROW_TILE + SUBLANES, DN_CONV_WIDTH), F32)],
        compiler_params=pltpu.CompilerParams(dimension_semantics=("parallel", "arbitrary"),
                                             vmem_limit_bytes=VMEM_LIMIT),
        name="dn_in",
    )(h, norm_w, w_qkv, w_z, w_ba, conv_w, alog_row, dtb_row)


def _dn_core_kernel(q_ref, k_ref, v_ref, bg_ref, o_ref, s_ref):
    @pl.when(pl.program_id(1) == 0)
    def _():
        s_ref[...] = jnp.zeros(s_ref.shape, F32)

    ri = lax.broadcasted_iota(jnp.int32, (DN_CHUNK, DN_CHUNK), 0)
    ci = lax.broadcasted_iota(jnp.int32, (DN_CHUNK, DN_CHUNK), 1)
    causal = ri >= ci
    strict = ri > ci
    eye = (ri == ci).astype(F32)

    bg = bg_ref[...]
    gcum = jnp.dot(causal.astype(F32), bg, preferred_element_type=F32,
                   precision=lax.Precision.HIGHEST)
    egc = jnp.exp(gcum)
    bg_t = bg.T
    gcum_t = gcum.T

    for kh in range(DN_K_HEADS):
        kcols = slice(kh * DN_HEAD_DIM, (kh + 1) * DN_HEAD_DIM)
        q = q_ref[:, kcols]
        k = k_ref[:, kcols]
        qk_kk = _bdot_nt(jnp.concatenate([q, k], axis=0), k)
        qk = qk_kk[:DN_CHUNK]
        kk = qk_kk[DN_CHUNK:]
        k_t = k.astype(F32).T
        for j in range(DN_V_HEADS // DN_K_HEADS):
            h = kh * (DN_V_HEADS // DN_K_HEADS) + j
            gl = DN_V_HEADS + h
            beta_col = bg[:, h:h + 1]
            beta_row = bg_t[h:h + 1, :]
            gc_col = gcum[:, gl:gl + 1]
            gc_row = gcum_t[gl:gl + 1, :]
            egc_col = egc[:, gl:gl + 1]
            g_last = gc_col[DN_CHUNK - 1:DN_CHUNK, :]
            decay = jnp.exp(jnp.where(causal, gc_col - gc_row, NEG_BIG))
            n1 = jnp.where(strict, kk * decay, 0.0) * (-beta_col)
            t = eye + n1
            npow = n1
            for _ in range(5):
                npow = _bdot(npow, npow)
                t = t + _bdot(t, npow)
            u = _bdot(t * beta_row, v_ref[:, h * DN_HEAD_DIM:(h + 1) * DN_HEAD_DIM])
            w = _bdot(t * (beta_row * jnp.exp(gc_row)), k)
            s_old = s_ref[h]
            ws_qs = _bdot(jnp.concatenate([w.astype(BF16), q], axis=0), s_old)
            v_new = u - ws_qs[:DN_CHUNK]
            attn = jnp.where(causal, qk * decay, 0.0)
            o = egc_col * ws_qs[DN_CHUNK:] + _bdot(attn, v_new)
            o_ref[:, h * DN_HEAD_DIM:(h + 1) * DN_HEAD_DIM] = o.astype(BF16)
            k_state_t = k_t * jnp.exp(g_last - gc_row)
            s_ref[h] = s_old * jnp.exp(g_last) + _bdot(k_state_t, v_new)


def _dn_core(q, k, v, bg):
    b, lp, _ = q.shape
    nc = lp // DN_CHUNK
    blk = lambda width: pl.BlockSpec((None, DN_CHUNK, width), lambda bi, c: (bi, c, 0))
    return pl.pallas_call(
        _dn_core_kernel,
        grid=(b, nc),
        in_specs=[blk(DN_KEY_WIDTH), blk(DN_KEY_WIDTH), blk(DN_VALUE_WIDTH), blk(LANES)],
        out_specs=blk(DN_VALUE_WIDTH),
        out_shape=jax.ShapeDtypeStruct((b, lp, DN_VALUE_WIDTH), BF16),
        scratch_shapes=[pltpu.VMEM((DN_V_HEADS, DN_HEAD_DIM, DN_HEAD_DIM), F32)],
        compiler_params=pltpu.CompilerParams(dimension_semantics=("parallel", "arbitrary"),
                                             vmem_limit_bytes=VMEM_LIMIT),
        name="dn_core",
    )(q, k, v, bg)


def _dn_out_kernel(h_ref, o_ref, z_ref, onw_ref, w_ref, out_ref):
    onw = onw_ref[...]
    parts = []
    for hh in range(DN_V_HEADS):
        cols = slice(hh * DN_HEAD_DIM, (hh + 1) * DN_HEAD_DIM)
        y = _rms_rows(o_ref[:, cols].astype(F32), onw) * _silu(z_ref[:, cols].astype(F32))
        parts.append(y.astype(BF16))
    y = jnp.concatenate(parts, axis=1)
    out_ref[...] = h_ref[...] + jnp.dot(y, w_ref[...], preferred_element_type=F32)


def _dn_out(h, o, z, onw, w_out, seq):
    b, lp, d = h.shape
    skip = (lp - seq) // ROW_TILE
    in_block = lambda width: pl.BlockSpec((None, ROW_TILE, width), lambda bi, i: (bi, i + skip, 0))
    return pl.pallas_call(
        _dn_out_kernel,
        grid=(b, seq // ROW_TILE),
        in_specs=[in_block(d), in_block(DN_VALUE_WIDTH), in_block(DN_VALUE_WIDTH),
                  pl.BlockSpec((1, DN_HEAD_DIM), lambda bi, i: (0, 0)),
                  pl.BlockSpec(w_out.shape, lambda bi, i: (0, 0))],
        out_specs=pl.BlockSpec((None, ROW_TILE, d), lambda bi, i: (bi, i, 0)),
        out_shape=jax.ShapeDtypeStruct((b, seq, d), F32),
        compiler_params=pltpu.CompilerParams(dimension_semantics=("parallel", "parallel"),
                                             vmem_limit_bytes=VMEM_LIMIT),
        name="dn_out",
    )(h, o, z, onw, w_out)


def kernel(x, meta_tokens, attn_norm_w, attn_w_in, attn_q_norm_w, attn_k_norm_w, attn_sinks,
           attn_w_out, dn_norm_w, dn_w_in, dn_conv_w, dn_a_log, dn_dt_bias, dn_o_norm_w, dn_w_out):
    b, seq, d = x.shape
    assert seq % ROW_TILE == 0 and attn_norm_w.shape[0] == 1 and dn_norm_w.shape[0] == 1
    lp = seq + ROW_TILE
    meta = jnp.broadcast_to(meta_tokens.astype(x.dtype)[None], (b, N_META, d))
    hp = jnp.concatenate([jnp.zeros((b, PAD, d), x.dtype), meta, x], axis=1)

    q, kv, gate = _attn_in(hp.reshape(b * lp, d), attn_norm_w[0][None],
                           attn_w_in[0].astype(BF16))
    og = _attention(q.reshape(b, lp, -1), kv.reshape(b, lp, -1), gate.reshape(b, lp, -1),
                    attn_sinks[0], attn_q_norm_w[0][None], attn_k_norm_w[0][None])
    h1 = _attn_out(hp.reshape(b * lp, d), og.reshape(b * lp, -1),
                   attn_w_out[0].astype(BF16)).reshape(b, lp, d)

    w_in = dn_w_in[0]
    w_qkv = w_in[:, :DN_CONV_WIDTH].astype(BF16)
    w_z = w_in[:, DN_CONV_WIDTH:DN_CONV_WIDTH + DN_VALUE_WIDTH].astype(BF16)
    n_ba = 2 * DN_V_HEADS
    w_ba = jnp.pad(w_in[:, DN_CONV_WIDTH + DN_VALUE_WIDTH:], ((0, 0), (0, LANES - n_ba))).astype(BF16)
    lane_pad = lambda t: jnp.pad(t[None], ((0, 0), (DN_V_HEADS, LANES - n_ba)))
    qd, kd, vd, zd, bg = _dn_in(h1, dn_norm_w[0][None], w_qkv, w_z, w_ba, dn_conv_w[0],
                                lane_pad(dn_a_log[0]), lane_pad(dn_dt_bias[0]))
    od = _dn_core(qd, kd, vd, bg)
    return _dn_out(h1, od, zd, dn_o_norm_w[0][None], dn_w_out[0].astype(BF16), seq)
```

```python
import numpy as np
import jax
import jax.numpy as jnp
from jax import lax
from jax.experimental import pallas as pl
from jax.experimental.pallas import tpu as pltpu

F32 = jnp.float32
BF16 = jnp.bfloat16

N_META = 16
NORM_EPS = 1e-6

ATTN_HEAD_DIM = 64
ATTN_HEADS = 16
ATTN_KV_HEADS = 2
ATTN_GROUPS = ATTN_HEADS // ATTN_KV_HEADS
ATTN_WIDTH = ATTN_HEADS * ATTN_HEAD_DIM
ATTN_KV_WIDTH = ATTN_KV_HEADS * ATTN_HEAD_DIM
WINDOW = 128
ATTN_BLOCK = 128

DN_HEAD_DIM = 128
DN_K_HEADS = 8
DN_V_HEADS = 16
DN_KEY_WIDTH = DN_K_HEADS * DN_HEAD_DIM
DN_VALUE_WIDTH = DN_V_HEADS * DN_HEAD_DIM
DN_CONV = 4
DN_CHUNK = 64
DN_CONV_WIDTH = 2 * DN_KEY_WIDTH + DN_VALUE_WIDTH

LANES = 128
SUBLANES = 8
ROW_TILE = 256
PAD = ROW_TILE - N_META
META_BLOCK = PAD // ATTN_BLOCK
META_OFFSET = PAD % ATTN_BLOCK
CONV_COLS = 512
VMEM_LIMIT = 56 * 1024 * 1024

NEG_BIG = -1e30
CLIP_INVALID = 1e30


def _alibi_slopes(n_heads):
    return np.exp2(-8.0 * np.arange(1, n_heads + 1) / n_heads).astype(np.float32)


def _bdot(a, b):
    return jnp.dot(a.astype(BF16), b.astype(BF16), preferred_element_type=F32)


def _bdot_nt(a, b):
    return lax.dot_general(a.astype(BF16), b.astype(BF16), (((1,), (1,)), ((), ())),
                           preferred_element_type=F32)


def _rms_rows(x, w):
    return x * lax.rsqrt(jnp.mean(x * x, axis=-1, keepdims=True) + NORM_EPS) * w


def _silu(x):
    return x * jax.nn.sigmoid(x)


def _attn_in_kernel(h_ref, nw_ref, w_ref, q_ref, kv_ref, g_ref):
    xn = _rms_rows(h_ref[...], nw_ref[...]).astype(BF16)
    q_ref[...] = jnp.dot(xn, w_ref[:, :ATTN_WIDTH], preferred_element_type=F32).astype(BF16)
    kv_ref[...] = jnp.dot(xn, w_ref[:, ATTN_WIDTH:ATTN_WIDTH + 2 * ATTN_KV_WIDTH],
                          preferred_element_type=F32).astype(BF16)
    g_ref[...] = jnp.dot(xn, w_ref[:, ATTN_WIDTH + 2 * ATTN_KV_WIDTH:],
                         preferred_element_type=F32).astype(BF16)


def _attn_in(h2d, norm_w, w_in):
    rows, d = h2d.shape
    n_in = w_in.shape[1]
    return pl.pallas_call(
        _attn_in_kernel,
        grid=(rows // ROW_TILE,),
        in_specs=[pl.BlockSpec((ROW_TILE, d), lambda i: (i, 0)),
                  pl.BlockSpec((1, d), lambda i: (0, 0)),
                  pl.BlockSpec((d, n_in), lambda i: (0, 0))],
        out_specs=[pl.BlockSpec((ROW_TILE, ATTN_WIDTH), lambda i: (i, 0)),
                   pl.BlockSpec((ROW_TILE, 2 * ATTN_KV_WIDTH), lambda i: (i, 0)),
                   pl.BlockSpec((ROW_TILE, ATTN_WIDTH), lambda i: (i, 0))],
        out_shape=[jax.ShapeDtypeStruct((rows, ATTN_WIDTH), BF16),
                   jax.ShapeDtypeStruct((rows, 2 * ATTN_KV_WIDTH), BF16),
                   jax.ShapeDtypeStruct((rows, ATTN_WIDTH), BF16)],
        compiler_params=pltpu.CompilerParams(dimension_semantics=("parallel",),
                                             vmem_limit_bytes=VMEM_LIMIT),
        name="attn_in",
    )(h2d, norm_w, w_in)


def _attn_kernel(sink_ref, q_ref, kvm_ref, kvp_ref, kvc_ref, gate_ref, qnw_ref, knw_ref, o_ref):
    nref = pl.program_id(1) - META_BLOCK
    qi = lax.broadcasted_iota(jnp.int32, (ATTN_BLOCK, ATTN_BLOCK), 0)
    c = lax.broadcasted_iota(jnp.int32, (ATTN_BLOCK, ATTN_BLOCK), 1)
    dist_m = nref * ATTN_BLOCK + qi - c
    valid_m = jnp.logical_and(c >= META_OFFSET, dist_m >= 0)
    dist_p = ATTN_BLOCK + qi - c
    valid_p = jnp.logical_and(c > qi, nref >= 2)
    dist_c = qi - c
    valid_c = jnp.logical_and(c <= qi, nref >= 1)

    def clip(dist, valid):
        return jnp.where(valid, jnp.minimum(dist, WINDOW).astype(F32), CLIP_INVALID)

    clipdist = jnp.concatenate([clip(dist_m, valid_m), clip(dist_p, valid_p),
                                clip(dist_c, valid_c)], axis=1)
    slopes = _alibi_slopes(ATTN_HEADS)
    scale = ATTN_HEAD_DIM ** -0.5
    qnw = qnw_ref[...]
    knw = knw_ref[...]

    for h in range(ATTN_KV_HEADS):
        ks = slice(h * ATTN_HEAD_DIM, (h + 1) * ATTN_HEAD_DIM)
        vs = slice(ATTN_KV_WIDTH + h * ATTN_HEAD_DIM, ATTN_KV_WIDTH + (h + 1) * ATTN_HEAD_DIM)
        k_ext = jnp.concatenate([kvm_ref[:, ks], kvp_ref[:, ks], kvc_ref[:, ks]], axis=0)
        k_ext = _rms_rows(k_ext.astype(F32), knw).astype(BF16)
        v_ext = jnp.concatenate([kvm_ref[:, vs], kvp_ref[:, vs], kvc_ref[:, vs]], axis=0)
        heads = [h * ATTN_GROUPS + g for g in range(ATTN_GROUPS)]
        qs = jnp.concatenate(
            [q_ref[:, hq * ATTN_HEAD_DIM:(hq + 1) * ATTN_HEAD_DIM] for hq in heads], axis=0)
        qs = (_rms_rows(qs.astype(F32), qnw) * scale).astype(BF16)
        s = _bdot_nt(qs, k_ext)
        s = jnp.concatenate(
            [s[g * ATTN_BLOCK:(g + 1) * ATTN_BLOCK] - float(slopes[hq]) * clipdist
             for g, hq in enumerate(heads)], axis=0)
        sink = jnp.concatenate(
            [jnp.full((ATTN_BLOCK, 1), sink_ref[hq], F32) for hq in heads], axis=0)
        m = jnp.maximum(jnp.max(s, axis=-1, keepdims=True), sink)
        p = jnp.exp(s - m)
        denom = jnp.sum(p, axis=-1, keepdims=True) + jnp.exp(sink - m)
        o = _bdot(p, v_ext) / denom
        for j in range(ATTN_GROUPS // 2):
            pair = jnp.concatenate([o[(2 * j) * ATTN_BLOCK:(2 * j + 1) * ATTN_BLOCK],
                                    o[(2 * j + 1) * ATTN_BLOCK:(2 * j + 2) * ATTN_BLOCK]], axis=1)
            cols = slice(heads[2 * j] * ATTN_HEAD_DIM, (heads[2 * j] + 2) * ATTN_HEAD_DIM)
            o_ref[:, cols] = (pair * _silu(gate_ref[:, cols].astype(F32))).astype(BF16)


def _attention(q, kv, gate, sinks, qnw, knw):
    b, lp, _ = q.shape
    nb = lp // ATTN_BLOCK
    kv_block = (None, ATTN_BLOCK, 2 * ATTN_KV_WIDTH)
    wide_block = (None, ATTN_BLOCK, ATTN_WIDTH)
    return pl.pallas_call(
        _attn_kernel,
        grid=(b, nb),
        in_specs=[pl.BlockSpec(memory_space=pltpu.SMEM),
                  pl.BlockSpec(wide_block, lambda bi, n: (bi, n, 0)),
                  pl.BlockSpec(kv_block, lambda bi, n: (bi, META_BLOCK, 0)),
                  pl.BlockSpec(kv_block, lambda bi, n: (bi, jnp.maximum(n - 1, 0), 0)),
                  pl.BlockSpec(kv_block, lambda bi, n: (bi, n, 0)),
                  pl.BlockSpec(wide_block, lambda bi, n: (bi, n, 0)),
                  pl.BlockSpec((1, ATTN_HEAD_DIM), lambda bi, n: (0, 0)),
                  pl.BlockSpec((1, ATTN_HEAD_DIM), lambda bi, n: (0, 0))],
        out_specs=pl.BlockSpec(wide_block, lambda bi, n: (bi, n, 0)),
        out_shape=jax.ShapeDtypeStruct((b, lp, ATTN_WIDTH), BF16),
        compiler_params=pltpu.CompilerParams(dimension_semantics=("parallel", "parallel"),
                                             vmem_limit_bytes=VMEM_LIMIT),
        name="attn_core",
    )(sinks, q, kv, kv, kv, gate, qnw, knw)


def _attn_out_kernel(h_ref, og_ref, w_ref, o_ref):
    o_ref[...] = h_ref[...] + jnp.dot(og_ref[...], w_ref[...], preferred_element_type=F32)


def _attn_out(h2d, og2d, w_out):
    rows, d = h2d.shape
    kdim = og2d.shape[1]
    return pl.pallas_call(
        _attn_out_kernel,
        grid=(rows // ROW_TILE,),
        in_specs=[pl.BlockSpec((ROW_TILE, d), lambda i: (i, 0)),
                  pl.BlockSpec((ROW_TILE, kdim), lambda i: (i, 0)),
                  pl.BlockSpec((kdim, d), lambda i: (0, 0))],
        out_specs=pl.BlockSpec((ROW_TILE, d), lambda i: (i, 0)),
        out_shape=jax.ShapeDtypeStruct((rows, d), F32),
        compiler_params=pltpu.CompilerParams(dimension_semantics=("parallel",),
                                             vmem_limit_bytes=VMEM_LIMIT),
        name="attn_out",
    )(h2d, og2d, w_out)


def _dn_in_kernel(h_ref, nw_ref, wqkv_ref, wz_ref, wba_ref, convw_ref, alog_ref, dtb_ref,
                  q_ref, k_ref, v_ref, z_ref, bg_ref, buf_ref):
    i = pl.program_id(1)
    halo = SUBLANES

    @pl.when(i == 0)
    def _():
        buf_ref[0:halo, :] = jnp.zeros((halo, DN_CONV_WIDTH), F32)

    xn = _rms_rows(h_ref[...], nw_ref[...]).astype(BF16)
    z_ref[...] = jnp.dot(xn, wz_ref[...], preferred_element_type=F32).astype(BF16)

    ba = jnp.dot(xn, wba_ref[...], preferred_element_type=F32)
    lane = lax.broadcasted_iota(jnp.int32, ba.shape, 1)
    row = i * ROW_TILE + lax.broadcasted_iota(jnp.int32, ba.shape, 0)
    x = ba + dtb_ref[...]
    softplus = jnp.maximum(x, 0.0) + jnp.log1p(jnp.exp(-jnp.abs(x)))
    g = -jnp.exp(alog_ref[...]) * softplus
    bg = jnp.where(lane < DN_V_HEADS, jax.nn.sigmoid(ba), g)
    bg_ref[...] = jnp.where(row >= PAD, bg, 0.0)

    q_scale = DN_HEAD_DIM ** -0.5
    for cj in range(DN_CONV_WIDTH // CONV_COLS):
        cols = slice(cj * CONV_COLS, (cj + 1) * CONV_COLS)
        buf_ref[halo:halo + ROW_TILE, cols] = jnp.dot(xn, wqkv_ref[:, cols],
                                                      preferred_element_type=F32)
        acc = None
        for j in range(DN_CONV):
            start = halo - (DN_CONV - 1) + j
            term = convw_ref[j:j + 1, cols] * buf_ref[start:start + ROW_TILE, cols]
            acc = term if acc is None else acc + term
        buf_ref[0:halo, cols] = buf_ref[ROW_TILE:ROW_TILE + halo, cols]
        y = _silu(acc)
        col0 = cj * CONV_COLS
        if col0 < 2 * DN_KEY_WIDTH:
            dst, base, mul = ((q_ref, col0, q_scale) if col0 < DN_KEY_WIDTH
                              else (k_ref, col0 - DN_KEY_WIDTH, 1.0))
            for hh in range(CONV_COLS // DN_HEAD_DIM):
                yh = y[:, hh * DN_HEAD_DIM:(hh + 1) * DN_HEAD_DIM]
                inv = lax.rsqrt(jnp.sum(yh * yh, axis=-1, keepdims=True) + NORM_EPS) * mul
                dst[:, base + hh * DN_HEAD_DIM:base + (hh + 1) * DN_HEAD_DIM] = (yh * inv).astype(BF16)
        else:
            base = col0 - 2 * DN_KEY_WIDTH
            v_ref[:, base:base + CONV_COLS] = y.astype(BF16)


def _dn_in(h, norm_w, w_qkv, w_z, w_ba, conv_w, alog_row, dtb_row):
    b, lp, d = h.shape
    nt = lp // ROW_TILE
    row_block = lambda width: pl.BlockSpec((None, ROW_TILE, width), lambda bi, i: (bi, i, 0))
    full = lambda shape: pl.BlockSpec(shape, lambda bi, i: (0,) * len(shape))
    return pl.pallas_call(
        _dn_in_kernel,
        grid=(b, nt),
        in_specs=[row_block(d), full((1, d)), full(w_qkv.shape), full(w_z.shape), full(w_ba.shape),
                  full(conv_w.shape), full((1, LANES)), full((1, LANES))],
        out_specs=[row_block(DN_KEY_WIDTH), row_block(DN_KEY_WIDTH), row_block(DN_VALUE_WIDTH),
                   row_block(DN_VALUE_WIDTH), row_block(LANES)],
        out_shape=[jax.ShapeDtypeStruct((b, lp, DN_KEY_WIDTH), BF16),
                   jax.ShapeDtypeStruct((b, lp, DN_KEY_WIDTH), BF16),
                   jax.ShapeDtypeStruct((b, lp, DN_VALUE_WIDTH), BF16),
                   jax.ShapeDtypeStruct((b, lp, DN_VALUE_WIDTH), BF16),
                   jax.ShapeDtypeStruct((b, lp, LANES), F32)],
        scratch_shapes=[pltpu.VMEM((ROW_TILE + SUBLANES, DN_CONV_WIDTH), F32)],
        compiler_params=pltpu.CompilerParams(dimension_semantics=("parallel", "arbitrary"),
                                             vmem_limit_bytes=VMEM_LIMIT),
        name="dn_in",
    )(h, norm_w, w_qkv, w_z, w_ba, conv_w, alog_row, dtb_row)


def _dn_core_kernel(q_ref, k_ref, v_ref, bg_ref, o_ref, s_ref):
    @pl.when(pl.program_id(1) == 0)
    def _():
        s_ref[...] = jnp.zeros(s_ref.shape, F32)

    ri = lax.broadcasted_iota(jnp.int32, (DN_CHUNK, DN_CHUNK), 0)
    ci = lax.broadcasted_iota(jnp.int32, (DN_CHUNK, DN_CHUNK), 1)
    causal = ri >= ci
    strict = ri > ci
    eye = (ri == ci).astype(F32)

    bg = bg_ref[...]
    gcum = jnp.dot(causal.astype(F32), bg, preferred_element_type=F32,
                   precision=lax.Precision.HIGHEST)
    egc = jnp.exp(gcum)
    gcum_t = gcum.T

    reps = DN_V_HEADS // DN_K_HEADS
    heads = range(DN_V_HEADS)
    q_l, k_l, qk_l, kk_l, kt_l = [], [], [], [], []
    for kh in range(DN_K_HEADS):
        kcols = slice(kh * DN_HEAD_DIM, (kh + 1) * DN_HEAD_DIM)
        q = q_ref[:, kcols]
        k = k_ref[:, kcols]
        qk_kk = _bdot_nt(jnp.concatenate([q, k], axis=0), k)
        q_l.append(q)
        k_l.append(k)
        qk_l.append(qk_kk[:DN_CHUNK])
        kk_l.append(qk_kk[DN_CHUNK:])
        kt_l.append(k.astype(F32).T)

    decay_l, n1_l, rhs_l, egc_col_l, gc_row_l, g_last_l = [], [], [], [], [], []
    for h in heads:
        kh = h // reps
        gl = DN_V_HEADS + h
        beta_col = bg[:, h:h + 1]
        gc_col = gcum[:, gl:gl + 1]
        gc_row = gcum_t[gl:gl + 1, :]
        egc_col = egc[:, gl:gl + 1]
        decay = jnp.exp(jnp.where(causal, gc_col - gc_row, NEG_BIG))
        decay_l.append(decay)
        n1_l.append(jnp.where(strict, kk_l[kh] * decay, 0.0) * (-beta_col))
        v = v_ref[:, h * DN_HEAD_DIM:(h + 1) * DN_HEAD_DIM].astype(F32)
        rhs_l.append(jnp.concatenate(
            [(v * beta_col).astype(BF16),
             (k_l[kh].astype(F32) * (beta_col * egc_col)).astype(BF16)], axis=1))
        egc_col_l.append(egc_col)
        gc_row_l.append(gc_row)
        g_last_l.append(gc_col[DN_CHUNK - 1:DN_CHUNK, :])

    npow_l = [_bdot(n1_l[h], n1_l[h]) for h in heads]
    t_l = [eye + n1_l[h] for h in heads]
    for _ in range(4):
        res_l = [_bdot(jnp.concatenate([t_l[h], npow_l[h]], axis=0), npow_l[h]) for h in heads]
        t_l = [t_l[h] + res_l[h][:DN_CHUNK] for h in heads]
        npow_l = [res_l[h][DN_CHUNK:] for h in heads]
    t_l = [t_l[h] + _bdot(t_l[h], npow_l[h]) for h in heads]

    uw_l = [_bdot(t_l[h], rhs_l[h]) for h in heads]
    s_l = [s_ref[h] for h in heads]
    ws_qs_l = [_bdot(jnp.concatenate([uw_l[h][:, DN_HEAD_DIM:].astype(BF16), q_l[h // reps]], axis=0),
                     s_l[h]) for h in heads]
    vnew_l = [uw_l[h][:, :DN_HEAD_DIM] - ws_qs_l[h][:DN_CHUNK] for h in heads]
    av_l = []
    for h in heads:
        kh = h // reps
        attn = jnp.where(causal, qk_l[kh] * decay_l[h], 0.0)
        k_state_t = kt_l[kh] * jnp.exp(g_last_l[h] - gc_row_l[h])
        av_l.append(_bdot(jnp.concatenate([attn, k_state_t], axis=0), vnew_l[h]))
    for h in heads:
        o = egc_col_l[h] * ws_qs_l[h][DN_CHUNK:] + av_l[h][:DN_CHUNK]
        o_ref[:, h * DN_HEAD_DIM:(h + 1) * DN_HEAD_DIM] = o.astype(BF16)
        s_ref[h] = s_l[h] * jnp.exp(g_last_l[h]) + av_l[h][DN_CHUNK:]


def _dn_core(q, k, v, bg):
    b, lp, _ = q.shape
    nc = lp // DN_CHUNK
    blk = lambda width: pl.BlockSpec((None, DN_CHUNK, width), lambda bi, c: (bi, c, 0))
    return pl.pallas_call(
        _dn_core_kernel,
        grid=(b, nc),
        in_specs=[blk(DN_KEY_WIDTH), blk(DN_KEY_WIDTH), blk(DN_VALUE_WIDTH), blk(LANES)],
        out_specs=blk(DN_VALUE_WIDTH),
        out_shape=jax.ShapeDtypeStruct((b, lp, DN_VALUE_WIDTH), BF16),
        scratch_shapes=[pltpu.VMEM((DN_V_HEADS, DN_HEAD_DIM, DN_HEAD_DIM), F32)],
        compiler_params=pltpu.CompilerParams(dimension_semantics=("parallel", "arbitrary"),
                                             vmem_limit_bytes=VMEM_LIMIT),
        name="dn_core",
    )(q, k, v, bg)


def _dn_out_kernel(h_ref, o_ref, z_ref, onw_ref, w_ref, out_ref):
    onw = onw_ref[...]
    parts = []
    for hh in range(DN_V_HEADS):
        cols = slice(hh * DN_HEAD_DIM, (hh + 1) * DN_HEAD_DIM)
        y = _rms_rows(o_ref[:, cols].astype(F32), onw) * _silu(z_ref[:, cols].astype(F32))
        parts.append(y.astype(BF16))
    y = jnp.concatenate(parts, axis=1)
    out_ref[...] = h_ref[...] + jnp.dot(y, w_ref[...], preferred_element_type=F32)


def _dn_out(h, o, z, onw, w_out, seq):
    b, lp, d = h.shape
    skip = (lp - seq) // ROW_TILE
    in_block = lambda width: pl.BlockSpec((None, ROW_TILE, width), lambda bi, i: (bi, i + skip, 0))
    return pl.pallas_call(
        _dn_out_kernel,
        grid=(b, seq // ROW_TILE),
        in_specs=[in_block(d), in_block(DN_VALUE_WIDTH), in_block(DN_VALUE_WIDTH),
                  pl.BlockSpec((1, DN_HEAD_DIM), lambda bi, i: (0, 0)),
                  pl.BlockSpec(w_out.shape, lambda bi, i: (0, 0))],
        out_specs=pl.BlockSpec((None, ROW_TILE, d), lambda bi, i: (bi, i, 0)),
        out_shape=jax.ShapeDtypeStruct((b, seq, d), F32),
        compiler_params=pltpu.CompilerParams(dimension_semantics=("parallel", "parallel"),
                                             vmem_limit_bytes=VMEM_LIMIT),
        name="dn_out",
    )(h, o, z, onw, w_out)


def kernel(x, meta_tokens, attn_norm_w, attn_w_in, attn_q_norm_w, attn_k_norm_w, attn_sinks,
           attn_w_out, dn_norm_w, dn_w_in, dn_conv_w, dn_a_log, dn_dt_bias, dn_o_norm_w, dn_w_out):
    b, seq, d = x.shape
    assert seq % ROW_TILE == 0 and attn_norm_w.shape[0] == 1 and dn_norm_w.shape[0] == 1
    lp = seq + ROW_TILE
    meta = jnp.broadcast_to(meta_tokens.astype(x.dtype)[None], (b, N_META, d))
    hp = jnp.concatenate([jnp.zeros((b, PAD, d), x.dtype), meta, x], axis=1)

    q, kv, gate = _attn_in(hp.reshape(b * lp, d), attn_norm_w[0][None],
                           attn_w_in[0].astype(BF16))
    og = _attention(q.reshape(b, lp, -1), kv.reshape(b, lp, -1), gate.reshape(b, lp, -1),
                    attn_sinks[0], attn_q_norm_w[0][None], attn_k_norm_w[0][None])
    h1 = _attn_out(hp.reshape(b * lp, d), og.reshape(b * lp, -1),
                   attn_w_out[0].astype(BF16)).reshape(b, lp, d)

    w_in = dn_w_in[0]
    w_qkv = w_in[:, :DN_CONV_WIDTH].astype(BF16)
    w_z = w_in[:, DN_CONV_WIDTH:DN_CONV_WIDTH + DN_VALUE_WIDTH].astype(BF16)
    n_ba = 2 * DN_V_HEADS
    w_ba = jnp.pad(w_in[:, DN_CONV_WIDTH + DN_VALUE_WIDTH:], ((0, 0), (0, LANES - n_ba))).astype(BF16)
    lane_pad = lambda t: jnp.pad(t[None], ((0, 0), (DN_V_HEADS, LANES - n_ba)))
    qd, kd, vd, zd, bg = _dn_in(h1, dn_norm_w[0][None], w_qkv, w_z, w_ba, dn_conv_w[0],
                                lane_pad(dn_a_log[0]), lane_pad(dn_dt_bias[0]))
    od = _dn_core(qd, kd, vd, bg)
    return _dn_out(h1, od, zd, dn_o_norm_w[0][None], dn_w_out[0].astype(BF16), seq)
```

```python
import numpy as np
import jax
import jax.numpy as jnp
from jax import lax
from jax.experimental import pallas as pl
from jax.experimental.pallas import tpu as pltpu

F32 = jnp.float32
BF16 = jnp.bfloat16

N_META = 16
NORM_EPS = 1e-6

ATTN_HEAD_DIM = 64
ATTN_HEADS = 16
ATTN_KV_HEADS = 2
ATTN_GROUPS = ATTN_HEADS // ATTN_KV_HEADS
ATTN_WIDTH = ATTN_HEADS * ATTN_HEAD_DIM
ATTN_KV_WIDTH = ATTN_KV_HEADS * ATTN_HEAD_DIM
WINDOW = 128
ATTN_BLOCK = 128

DN_HEAD_DIM = 128
DN_K_HEADS = 8
DN_V_HEADS = 16
DN_KEY_WIDTH = DN_K_HEADS * DN_HEAD_DIM
DN_VALUE_WIDTH = DN_V_HEADS * DN_HEAD_DIM
DN_CONV = 4
DN_CHUNK = 64
DN_CONV_WIDTH = 2 * DN_KEY_WIDTH + DN_VALUE_WIDTH

LANES = 128
SUBLANES = 8
ROW_TILE = 256
PAD = ROW_TILE - N_META
META_BLOCK = PAD // ATTN_BLOCK
META_OFFSET = PAD % ATTN_BLOCK
CONV_COLS = 512
CONV_PHASES = 4
VMEM_LIMIT = 56 * 1024 * 1024

NEG_BIG = -1e30
CLIP_INVALID = 1e30


def _alibi_slopes(n_heads):
    return np.exp2(-8.0 * np.arange(1, n_heads + 1) / n_heads).astype(np.float32)


def _bdot(a, b):
    return jnp.dot(a.astype(BF16), b.astype(BF16), preferred_element_type=F32)


def _bdot_nt(a, b):
    return lax.dot_general(a.astype(BF16), b.astype(BF16), (((1,), (1,)), ((), ())),
                           preferred_element_type=F32)


def _rms_rows(x, w):
    return x * lax.rsqrt(jnp.mean(x * x, axis=-1, keepdims=True) + NORM_EPS) * w


def _silu(x):
    return x * jax.nn.sigmoid(x)


def _attn_in_kernel(h_ref, nw_ref, w_ref, q_ref, kv_ref, g_ref):
    xn = _rms_rows(h_ref[...], nw_ref[...]).astype(BF16)
    q_ref[...] = jnp.dot(xn, w_ref[:, :ATTN_WIDTH], preferred_element_type=F32).astype(BF16)
    kv_ref[...] = jnp.dot(xn, w_ref[:, ATTN_WIDTH:ATTN_WIDTH + 2 * ATTN_KV_WIDTH],
                          preferred_element_type=F32).astype(BF16)
    g_ref[...] = jnp.dot(xn, w_ref[:, ATTN_WIDTH + 2 * ATTN_KV_WIDTH:],
                         preferred_element_type=F32).astype(BF16)


def _attn_in(h2d, norm_w, w_in):
    rows, d = h2d.shape
    n_in = w_in.shape[1]
    return pl.pallas_call(
        _attn_in_kernel,
        grid=(rows // ROW_TILE,),
        in_specs=[pl.BlockSpec((ROW_TILE, d), lambda i: (i, 0)),
                  pl.BlockSpec((1, d), lambda i: (0, 0)),
                  pl.BlockSpec((d, n_in), lambda i: (0, 0))],
        out_specs=[pl.BlockSpec((ROW_TILE, ATTN_WIDTH), lambda i: (i, 0)),
                   pl.BlockSpec((ROW_TILE, 2 * ATTN_KV_WIDTH), lambda i: (i, 0)),
                   pl.BlockSpec((ROW_TILE, ATTN_WIDTH), lambda i: (i, 0))],
        out_shape=[jax.ShapeDtypeStruct((rows, ATTN_WIDTH), BF16),
                   jax.ShapeDtypeStruct((rows, 2 * ATTN_KV_WIDTH), BF16),
                   jax.ShapeDtypeStruct((rows, ATTN_WIDTH), BF16)],
        compiler_params=pltpu.CompilerParams(dimension_semantics=("parallel",),
                                             vmem_limit_bytes=VMEM_LIMIT),
        name="attn_in",
    )(h2d, norm_w, w_in)


def _attn_kernel(sink_ref, q_ref, kvm_ref, kvp_ref, kvc_ref, gate_ref, qnw_ref, knw_ref, o_ref):
    nref = pl.program_id(1) - META_BLOCK
    qi = lax.broadcasted_iota(jnp.int32, (ATTN_BLOCK, ATTN_BLOCK), 0)
    c = lax.broadcasted_iota(jnp.int32, (ATTN_BLOCK, ATTN_BLOCK), 1)
    dist_m = nref * ATTN_BLOCK + qi - c
    valid_m = jnp.logical_and(c >= META_OFFSET, dist_m >= 0)
    dist_p = ATTN_BLOCK + qi - c
    valid_p = jnp.logical_and(c > qi, nref >= 2)
    dist_c = qi - c
    valid_c = jnp.logical_and(c <= qi, nref >= 1)

    def clip(dist, valid):
        return jnp.where(valid, jnp.minimum(dist, WINDOW).astype(F32), CLIP_INVALID)

    clipdist = jnp.concatenate([clip(dist_m, valid_m), clip(dist_p, valid_p),
                                clip(dist_c, valid_c)], axis=1)
    slopes = _alibi_slopes(ATTN_HEADS)
    scale = ATTN_HEAD_DIM ** -0.5
    qnw = qnw_ref[...]
    knw = knw_ref[...]

    for h in range(ATTN_KV_HEADS):
        ks = slice(h * ATTN_HEAD_DIM, (h + 1) * ATTN_HEAD_DIM)
        vs = slice(ATTN_KV_WIDTH + h * ATTN_HEAD_DIM, ATTN_KV_WIDTH + (h + 1) * ATTN_HEAD_DIM)
        k_ext = jnp.concatenate([kvm_ref[:, ks], kvp_ref[:, ks], kvc_ref[:, ks]], axis=0)
        k_ext = _rms_rows(k_ext.astype(F32), knw).astype(BF16)
        v_ext = jnp.concatenate([kvm_ref[:, vs], kvp_ref[:, vs], kvc_ref[:, vs]], axis=0)
        heads = [h * ATTN_GROUPS + g for g in range(ATTN_GROUPS)]
        qs = jnp.concatenate(
            [q_ref[:, hq * ATTN_HEAD_DIM:(hq + 1) * ATTN_HEAD_DIM] for hq in heads], axis=0)
        qs = (_rms_rows(qs.astype(F32), qnw) * scale).astype(BF16)
        s = _bdot_nt(qs, k_ext)
        s = jnp.concatenate(
            [s[g * ATTN_BLOCK:(g + 1) * ATTN_BLOCK] - float(slopes[hq]) * clipdist
             for g, hq in enumerate(heads)], axis=0)
        sink = jnp.concatenate(
            [jnp.full((ATTN_BLOCK, 1), sink_ref[hq], F32) for hq in heads], axis=0)
        m = jnp.maximum(jnp.max(s, axis=-1, keepdims=True), sink)
        p = jnp.exp(s - m)
        denom = jnp.sum(p, axis=-1, keepdims=True) + jnp.exp(sink - m)
        o = _bdot(p, v_ext) / denom
        for j in range(ATTN_GROUPS // 2):
            pair = jnp.concatenate([o[(2 * j) * ATTN_BLOCK:(2 * j + 1) * ATTN_BLOCK],
                                    o[(2 * j + 1) * ATTN_BLOCK:(2 * j + 2) * ATTN_BLOCK]], axis=1)
            cols = slice(heads[2 * j] * ATTN_HEAD_DIM, (heads[2 * j] + 2) * ATTN_HEAD_DIM)
            o_ref[:, cols] = (pair * _silu(gate_ref[:, cols].astype(F32))).astype(BF16)


def _attention(q, kv, gate, sinks, qnw, knw):
    b, lp, _ = q.shape
    nb = lp // ATTN_BLOCK
    kv_block = (None, ATTN_BLOCK, 2 * ATTN_KV_WIDTH)
    wide_block = (None, ATTN_BLOCK, ATTN_WIDTH)
    return pl.pallas_call(
        _attn_kernel,
        grid=(b, nb),
        in_specs=[pl.BlockSpec(memory_space=pltpu.SMEM),
                  pl.BlockSpec(wide_block, lambda bi, n: (bi, n, 0)),
                  pl.BlockSpec(kv_block, lambda bi, n: (bi, META_BLOCK, 0)),
                  pl.BlockSpec(kv_block, lambda bi, n: (bi, jnp.maximum(n - 1, 0), 0)),
                  pl.BlockSpec(kv_block, lambda bi, n: (bi, n, 0)),
                  pl.BlockSpec(wide_block, lambda bi, n: (bi, n, 0)),
                  pl.BlockSpec((1, ATTN_HEAD_DIM), lambda bi, n: (0, 0)),
                  pl.BlockSpec((1, ATTN_HEAD_DIM), lambda bi, n: (0, 0))],
        out_specs=pl.BlockSpec(wide_block, lambda bi, n: (bi, n, 0)),
        out_shape=jax.ShapeDtypeStruct((b, lp, ATTN_WIDTH), BF16),
        compiler_params=pltpu.CompilerParams(dimension_semantics=("parallel", "parallel"),
                                             vmem_limit_bytes=VMEM_LIMIT),
        name="attn_core",
    )(sinks, q, kv, kv, kv, gate, qnw, knw)


def _attn_out_kernel(h_ref, og_ref, w_ref, o_ref):
    o_ref[...] = h_ref[...] + jnp.dot(og_ref[...], w_ref[...], preferred_element_type=F32)


def _attn_out(h2d, og2d, w_out):
    rows, d = h2d.shape
    kdim = og2d.shape[1]
    return pl.pallas_call(
        _attn_out_kernel,
        grid=(rows // ROW_TILE,),
        in_specs=[pl.BlockSpec((ROW_TILE, d), lambda i: (i, 0)),
                  pl.BlockSpec((ROW_TILE, kdim), lambda i: (i, 0)),
                  pl.BlockSpec((kdim, d), lambda i: (0, 0))],
        out_specs=pl.BlockSpec((ROW_TILE, d), lambda i: (i, 0)),
        out_shape=jax.ShapeDtypeStruct((rows, d), F32),
        compiler_params=pltpu.CompilerParams(dimension_semantics=("parallel",),
                                             vmem_limit_bytes=VMEM_LIMIT),
        name="attn_out",
    )(h2d, og2d, w_out)


def _dn_in_kernel(h_ref, nw_ref, wqkv_ref, wz_ref, wba_ref, convw_ref, alog_ref, dtb_ref,
                  q_ref, k_ref, v_ref, z_ref, bg_ref, buf_ref, ybuf_ref):
    i = pl.program_id(1)
    halo = SUBLANES

    @pl.when(i == 0)
    def _():
        buf_ref[:, 0:halo, :] = jnp.zeros((buf_ref.shape[0], halo, LANES), F32)

    xn = _rms_rows(h_ref[...], nw_ref[...]).astype(BF16)
    z_ref[...] = jnp.dot(xn, wz_ref[...], preferred_element_type=F32).astype(BF16)

    ba = jnp.dot(xn, wba_ref[...], preferred_element_type=F32)
    lane = lax.broadcasted_iota(jnp.int32, ba.shape, 1)
    row = i * ROW_TILE + lax.broadcasted_iota(jnp.int32, ba.shape, 0)
    x = ba + dtb_ref[...]
    softplus = jnp.maximum(x, 0.0) + jnp.log1p(jnp.exp(-jnp.abs(x)))
    g = -jnp.exp(alog_ref[...]) * softplus
    bg = jnp.where(lane < DN_V_HEADS, jax.nn.sigmoid(ba), g)
    bg_ref[...] = jnp.where(row >= PAD, bg, 0.0)

    q_scale = DN_HEAD_DIM ** -0.5
    rows_per_phase = ROW_TILE // CONV_PHASES
    slabs_per_chunk = CONV_COLS // LANES
    for cj in range(DN_CONV_WIDTH // CONV_COLS):
        u = jnp.dot(xn, wqkv_ref[:, cj * CONV_COLS:(cj + 1) * CONV_COLS],
                    preferred_element_type=F32)
        for sl in range(slabs_per_chunk):
            slab = cj * slabs_per_chunk + sl
            col0 = slab * LANES
            buf_ref[slab, halo:halo + ROW_TILE, :] = u[:, sl * LANES:(sl + 1) * LANES]
            w4 = convw_ref[:, col0:col0 + LANES]
            taps = {}
            for start in range(halo - (DN_CONV - 1), halo + CONV_PHASES):
                taps[start] = buf_ref[slab, pl.ds(start, rows_per_phase, stride=CONV_PHASES), :]
            for a in range(CONV_PHASES):
                acc = None
                for j in range(DN_CONV):
                    term = w4[DN_CONV - 1 - j:DN_CONV - j] * taps[halo + a - j]
                    acc = term if acc is None else acc + term
                ybuf_ref[sl, pl.ds(a, rows_per_phase, stride=CONV_PHASES), :] = _silu(acc)
            buf_ref[slab, 0:halo, :] = buf_ref[slab, ROW_TILE:ROW_TILE + halo, :]
            y = ybuf_ref[sl]
            if col0 < 2 * DN_KEY_WIDTH:
                dst, base, mul = ((q_ref, col0, q_scale) if col0 < DN_KEY_WIDTH
                                  else (k_ref, col0 - DN_KEY_WIDTH, 1.0))
                inv = lax.rsqrt(jnp.sum(y * y, axis=-1, keepdims=True) + NORM_EPS) * mul
                dst[:, base:base + LANES] = (y * inv).astype(BF16)
            else:
                base = col0 - 2 * DN_KEY_WIDTH
                v_ref[:, base:base + LANES] = y.astype(BF16)


def _dn_in(h, norm_w, w_qkv, w_z, w_ba, conv_w, alog_row, dtb_row):
    b, lp, d = h.shape
    nt = lp // ROW_TILE
    row_block = lambda width: pl.BlockSpec((None, ROW_TILE, width), lambda bi, i: (bi, i, 0))
    full = lambda shape: pl.BlockSpec(shape, lambda bi, i: (0,) * len(shape))
    return pl.pallas_call(
        _dn_in_kernel,
        grid=(b, nt),
        in_specs=[row_block(d), full((1, d)), full(w_qkv.shape), full(w_z.shape), full(w_ba.shape),
                  full(conv_w.shape), full((1, LANES)), full((1, LANES))],
        out_specs=[row_block(DN_KEY_WIDTH), row_block(DN_KEY_WIDTH), row_block(DN_VALUE_WIDTH),
                   row_block(DN_VALUE_WIDTH), row_block(LANES)],
        out_shape=[jax.ShapeDtypeStruct((b, lp, DN_KEY_WIDTH), BF16),
                   jax.ShapeDtypeStruct((b, lp, DN_KEY_WIDTH), BF16),
                   jax.ShapeDtypeStruct((b, lp, DN_VALUE_WIDTH), BF16),
                   jax.ShapeDtypeStruct((b, lp, DN_VALUE_WIDTH), BF16),
                   jax.ShapeDtypeStruct((b, lp, LANES), F32)],
        scratch_shapes=[pltpu.VMEM((DN_CONV_WIDTH // LANES, ROW_TILE + SUBLANES, LANES), F32),
                        pltpu.VMEM((CONV_COLS // LANES, ROW_TILE, LANES), F32)],
        compiler_params=pltpu.CompilerParams(dimension_semantics=("parallel", "arbitrary"),
                                             vmem_limit_bytes=VMEM_LIMIT),
        name="dn_in",
    )(h, norm_w, w_qkv, w_z, w_ba, conv_w, alog_row, dtb_row)


def _dn_core_kernel(q_ref, k_ref, v_ref, bg_ref, o_ref, s_ref):
    @pl.when(pl.program_id(1) == 0)
    def _():
        s_ref[...] = jnp.zeros(s_ref.shape, F32)

    ri = lax.broadcasted_iota(jnp.int32, (DN_CHUNK, DN_CHUNK), 0)
    ci = lax.broadcasted_iota(jnp.int32, (DN_CHUNK, DN_CHUNK), 1)
    causal = ri >= ci
    strict = ri > ci
    eye = (ri == ci).astype(F32)

    bg = bg_ref[...]
    gcum = jnp.dot(causal.astype(F32), bg, preferred_element_type=F32,
                   precision=lax.Precision.HIGHEST)
    egc = jnp.exp(gcum)
    gcum_t = gcum.T

    reps = DN_V_HEADS // DN_K_HEADS
    heads = range(DN_V_HEADS)
    q_l, k_l, qk_l, kk_l, kt_l = [], [], [], [], []
    for kh in range(DN_K_HEADS):
        kcols = slice(kh * DN_HEAD_DIM, (kh + 1) * DN_HEAD_DIM)
        q = q_ref[:, kcols]
        k = k_ref[:, kcols]
        qk_kk = _bdot_nt(jnp.concatenate([q, k], axis=0), k)
        q_l.append(q)
        k_l.append(k)
        qk_l.append(qk_kk[:DN_CHUNK])
        kk_l.append(qk_kk[DN_CHUNK:])
        kt_l.append(k.astype(F32).T)

    decay_l, n1_l, rhs_l, egc_col_l, gc_row_l, g_last_l = [], [], [], [], [], []
    for h in heads:
        kh = h // reps
        gl = DN_V_HEADS + h
        beta_col = bg[:, h:h + 1]
        gc_col = gcum[:, gl:gl + 1]
        gc_row = gcum_t[gl:gl + 1, :]
        egc_col = egc[:, gl:gl + 1]
        decay = jnp.exp(jnp.where(causal, gc_col - gc_row, NEG_BIG))
        decay_l.append(decay)
        n1_l.append(jnp.where(strict, kk_l[kh] * decay, 0.0) * (-beta_col))
        v = v_ref[:, h * DN_HEAD_DIM:(h + 1) * DN_HEAD_DIM].astype(F32)
        rhs_l.append(jnp.concatenate(
            [(v * beta_col).astype(BF16),
             (k_l[kh].astype(F32) * (beta_col * egc_col)).astype(BF16)], axis=1))
        egc_col_l.append(egc_col)
        gc_row_l.append(gc_row)
        g_last_l.append(gc_col[DN_CHUNK - 1:DN_CHUNK, :])

    npow_l = [_bdot(n1_l[h], n1_l[h]) for h in heads]
    t_l = [eye + n1_l[h] for h in heads]
    for _ in range(4):
        res_l = [_bdot(jnp.concatenate([t_l[h], npow_l[h]], axis=0), npow_l[h]) for h in heads]
        t_l = [t_l[h] + res_l[h][:DN_CHUNK] for h in heads]
        npow_l = [res_l[h][DN_CHUNK:] for h in heads]
    t_l = [t_l[h] + _bdot(t_l[h], npow_l[h]) for h in heads]

    uw_l = [_bdot(t_l[h], rhs_l[h]) for h in heads]
    s_l = [s_ref[h] for h in heads]
    ws_qs_l = [_bdot(jnp.concatenate([uw_l[h][:, DN_HEAD_DIM:].astype(BF16), q_l[h // reps]], axis=0),
                     s_l[h]) for h in heads]
    vnew_l = [uw_l[h][:, :DN_HEAD_DIM] - ws_qs_l[h][:DN_CHUNK] for h in heads]
    av_l = []
    for h in heads:
        kh = h // reps
        attn = jnp.where(causal, qk_l[kh] * decay_l[h], 0.0)
        k_state_t = kt_l[kh] * jnp.exp(g_last_l[h] - gc_row_l[h])
        av_l.append(_bdot(jnp.concatenate([attn, k_state_t], axis=0), vnew_l[h]))
    for h in heads:
        o = egc_col_l[h] * ws_qs_l[h][DN_CHUNK:] + av_l[h][:DN_CHUNK]
        o_ref[:, h * DN_HEAD_DIM:(h + 1) * DN_HEAD_DIM] = o.astype(BF16)
        s_ref[h] = s_l[h] * jnp.exp(g_last_l[h]) + av_l[h][DN_CHUNK:]


def _dn_core(q, k, v, bg):
    b, lp, _ = q.shape
    nc = lp // DN_CHUNK
    blk = lambda width: pl.BlockSpec((None, DN_CHUNK, width), lambda bi, c: (bi, c, 0))
    return pl.pallas_call(
        _dn_core_kernel,
        grid=(b, nc),
        in_specs=[blk(DN_KEY_WIDTH), blk(DN_KEY_WIDTH), blk(DN_VALUE_WIDTH), blk(LANES)],
        out_specs=blk(DN_VALUE_WIDTH),
        out_shape=jax.ShapeDtypeStruct((b, lp, DN_VALUE_WIDTH), BF16),
        scratch_shapes=[pltpu.VMEM((DN_V_HEADS, DN_HEAD_DIM, DN_HEAD_DIM), F32)],
        compiler_params=pltpu.CompilerParams(dimension_semantics=("parallel", "arbitrary"),
                                             vmem_limit_bytes=VMEM_LIMIT),
        name="dn_core",
    )(q, k, v, bg)


def _dn_out_kernel(h_ref, o_ref, z_ref, onw_ref, w_ref, out_ref):
    onw = onw_ref[...]
    parts = []
    for hh in range(DN_V_HEADS):
        cols = slice(hh * DN_HEAD_DIM, (hh + 1) * DN_HEAD_DIM)
        y = _rms_rows(o_ref[:, cols].astype(F32), onw) * _silu(z_ref[:, cols].astype(F32))
        parts.append(y.astype(BF16))
    y = jnp.concatenate(parts, axis=1)
    out_ref[...] = h_ref[...] + jnp.dot(y, w_ref[...], preferred_element_type=F32)


def _dn_out(h, o, z, onw, w_out, seq):
    b, lp, d = h.shape
    skip = (lp - seq) // ROW_TILE
    in_block = lambda width: pl.BlockSpec((None, ROW_TILE, width), lambda bi, i: (bi, i + skip, 0))
    return pl.pallas_call(
        _dn_out_kernel,
        grid=(b, seq // ROW_TILE),
        in_specs=[in_block(d), in_block(DN_VALUE_WIDTH), in_block(DN_VALUE_WIDTH),
                  pl.BlockSpec((1, DN_HEAD_DIM), lambda bi, i: (0, 0)),
                  pl.BlockSpec(w_out.shape, lambda bi, i: (0, 0))],
        out_specs=pl.BlockSpec((None, ROW_TILE, d), lambda bi, i: (bi, i, 0)),
        out_shape=jax.ShapeDtypeStruct((b, seq, d), F32),
        compiler_params=pltpu.CompilerParams(dimension_semantics=("parallel", "parallel"),
                                             vmem_limit_bytes=VMEM_LIMIT),
        name="dn_out",
    )(h, o, z, onw, w_out)


def kernel(x, meta_tokens, attn_norm_w, attn_w_in, attn_q_norm_w, attn_k_norm_w, attn_sinks,
           attn_w_out, dn_norm_w, dn_w_in, dn_conv_w, dn_a_log, dn_dt_bias, dn_o_norm_w, dn_w_out):
    b, seq, d = x.shape
    assert seq % ROW_TILE == 0 and attn_norm_w.shape[0] == 1 and dn_norm_w.shape[0] == 1
    lp = seq + ROW_TILE
    meta = jnp.broadcast_to(meta_tokens.astype(x.dtype)[None], (b, N_META, d))
    hp = jnp.concatenate([jnp.zeros((b, PAD, d), x.dtype), meta, x], axis=1)

    q, kv, gate = _attn_in(hp.reshape(b * lp, d), attn_norm_w[0][None],
                           attn_w_in[0].astype(BF16))
    og = _attention(q.reshape(b, lp, -1), kv.reshape(b, lp, -1), gate.reshape(b, lp, -1),
                    attn_sinks[0], attn_q_norm_w[0][None], attn_k_norm_w[0][None])
    h1 = _attn_out(hp.reshape(b * lp, d), og.reshape(b * lp, -1),
                   attn_w_out[0].astype(BF16)).reshape(b, lp, d)

    w_in = dn_w_in[0]
    w_qkv = w_in[:, :DN_CONV_WIDTH].astype(BF16)
    w_z = w_in[:, DN_CONV_WIDTH:DN_CONV_WIDTH + DN_VALUE_WIDTH].astype(BF16)
    n_ba = 2 * DN_V_HEADS
    w_ba = jnp.pad(w_in[:, DN_CONV_WIDTH + DN_VALUE_WIDTH:], ((0, 0), (0, LANES - n_ba))).astype(BF16)
    lane_pad = lambda t: jnp.pad(t[None], ((0, 0), (DN_V_HEADS, LANES - n_ba)))
    qd, kd, vd, zd, bg = _dn_in(h1, dn_norm_w[0][None], w_qkv, w_z, w_ba, dn_conv_w[0],
                                lane_pad(dn_a_log[0]), lane_pad(dn_dt_bias[0]))
    od = _dn_core(qd, kd, vd, bg)
    return _dn_out(h1, od, zd, dn_o_norm_w[0][None], dn_w_out[0].astype(BF16), seq)
```

```python
import numpy as np
import jax
import jax.numpy as jnp
from jax import lax
from jax.experimental import pallas as pl
from jax.experimental.pallas import tpu as pltpu

F32 = jnp.float32
BF16 = jnp.bfloat16

N_META = 16
NORM_EPS = 1e-6

ATTN_HEAD_DIM = 64
ATTN_HEADS = 16
ATTN_KV_HEADS = 2
ATTN_GROUPS = ATTN_HEADS // ATTN_KV_HEADS
ATTN_WIDTH = ATTN_HEADS * ATTN_HEAD_DIM
ATTN_KV_WIDTH = ATTN_KV_HEADS * ATTN_HEAD_DIM
WINDOW = 128
ATTN_BLOCK = 128

DN_HEAD_DIM = 128
DN_K_HEADS = 8
DN_V_HEADS = 16
DN_KEY_WIDTH = DN_K_HEADS * DN_HEAD_DIM
DN_VALUE_WIDTH = DN_V_HEADS * DN_HEAD_DIM
DN_CONV = 4
DN_CHUNK = 64
DN_CONV_WIDTH = 2 * DN_KEY_WIDTH + DN_VALUE_WIDTH

LANES = 128
SUBLANES = 8
ROW_TILE = 256
PAD = ROW_TILE - N_META
META_BLOCK = PAD // ATTN_BLOCK
META_OFFSET = PAD % ATTN_BLOCK
CONV_COLS = 512
CONV_PHASES = 4
DN_OUT_K_CHUNK = 512
DN_STEP_CHUNKS = 2
VMEM_LIMIT = 56 * 1024 * 1024

LOG2E = 1.4426950408889634
NEG_BIG = -1e30
CLIP_INVALID = 1e30


def _alibi_slopes(n_heads):
    return np.exp2(-8.0 * np.arange(1, n_heads + 1) / n_heads).astype(np.float32)


def _bdot(a, b):
    return jnp.dot(a.astype(BF16), b.astype(BF16), preferred_element_type=F32)


def _bdot_nt(a, b):
    return lax.dot_general(a.astype(BF16), b.astype(BF16), (((1,), (1,)), ((), ())),
                           preferred_element_type=F32)


def _rms_rows(x, w):
    return x * lax.rsqrt(jnp.mean(x * x, axis=-1, keepdims=True) + NORM_EPS) * w


def _silu(x):
    return x * jax.nn.sigmoid(x)


def _attn_in_kernel(h_ref, nw_ref, w_ref, q_ref, kv_ref, g_ref):
    xn = _rms_rows(h_ref[...], nw_ref[...]).astype(BF16)
    q_ref[...] = jnp.dot(xn, w_ref[:, :ATTN_WIDTH], preferred_element_type=F32).astype(BF16)
    kv_ref[...] = jnp.dot(xn, w_ref[:, ATTN_WIDTH:ATTN_WIDTH + 2 * ATTN_KV_WIDTH],
                          preferred_element_type=F32).astype(BF16)
    g_ref[...] = jnp.dot(xn, w_ref[:, ATTN_WIDTH + 2 * ATTN_KV_WIDTH:],
                         preferred_element_type=F32).astype(BF16)


def _attn_in(h2d, norm_w, w_in):
    rows, d = h2d.shape
    n_in = w_in.shape[1]
    return pl.pallas_call(
        _attn_in_kernel,
        grid=(rows // ROW_TILE,),
        in_specs=[pl.BlockSpec((ROW_TILE, d), lambda i: (i, 0)),
                  pl.BlockSpec((1, d), lambda i: (0, 0)),
                  pl.BlockSpec((d, n_in), lambda i: (0, 0))],
        out_specs=[pl.BlockSpec((ROW_TILE, ATTN_WIDTH), lambda i: (i, 0)),
                   pl.BlockSpec((ROW_TILE, 2 * ATTN_KV_WIDTH), lambda i: (i, 0)),
                   pl.BlockSpec((ROW_TILE, ATTN_WIDTH), lambda i: (i, 0))],
        out_shape=[jax.ShapeDtypeStruct((rows, ATTN_WIDTH), BF16),
                   jax.ShapeDtypeStruct((rows, 2 * ATTN_KV_WIDTH), BF16),
                   jax.ShapeDtypeStruct((rows, ATTN_WIDTH), BF16)],
        compiler_params=pltpu.CompilerParams(dimension_semantics=("parallel",),
                                             vmem_limit_bytes=VMEM_LIMIT),
        name="attn_in",
    )(h2d, norm_w, w_in)


def _attn_kernel(sink_ref, q_ref, kvm_ref, kvp_ref, kvc_ref, gate_ref, qnw_ref, knw_ref, o_ref):
    nref = pl.program_id(1) - META_BLOCK
    qi = lax.broadcasted_iota(jnp.int32, (ATTN_BLOCK, ATTN_BLOCK), 0)
    c = lax.broadcasted_iota(jnp.int32, (ATTN_BLOCK, ATTN_BLOCK), 1)
    dist_m = nref * ATTN_BLOCK + qi - c
    valid_m = jnp.logical_and(c >= META_OFFSET, dist_m >= 0)
    dist_p = ATTN_BLOCK + qi - c
    valid_p = jnp.logical_and(c > qi, nref >= 2)
    dist_c = qi - c
    valid_c = jnp.logical_and(c <= qi, nref >= 1)

    def clip(dist, valid):
        return jnp.where(valid, jnp.minimum(dist, WINDOW).astype(F32), CLIP_INVALID)

    clipdist = jnp.concatenate([clip(dist_m, valid_m), clip(dist_p, valid_p),
                                clip(dist_c, valid_c)], axis=1)
    slopes = _alibi_slopes(ATTN_HEADS) * np.float32(LOG2E)
    qnw = qnw_ref[...] * (ATTN_HEAD_DIM ** -0.5 * LOG2E)
    knw = knw_ref[...]

    s_l, v_l = [], []
    for h in range(ATTN_KV_HEADS):
        ks = slice(h * ATTN_HEAD_DIM, (h + 1) * ATTN_HEAD_DIM)
        vs = slice(ATTN_KV_WIDTH + h * ATTN_HEAD_DIM, ATTN_KV_WIDTH + (h + 1) * ATTN_HEAD_DIM)
        k_ext = jnp.concatenate([kvm_ref[:, ks], kvp_ref[:, ks], kvc_ref[:, ks]], axis=0)
        k_ext = _rms_rows(k_ext.astype(F32), knw).astype(BF16)
        v_l.append(jnp.concatenate([kvm_ref[:, vs], kvp_ref[:, vs], kvc_ref[:, vs]], axis=0))
        qs = jnp.concatenate(
            [q_ref[:, hq * ATTN_HEAD_DIM:(hq + 1) * ATTN_HEAD_DIM]
             for hq in range(h * ATTN_GROUPS, (h + 1) * ATTN_GROUPS)], axis=0)
        qs = _rms_rows(qs.astype(F32), qnw).astype(BF16)
        s_l.append(_bdot_nt(qs, k_ext))

    for h in range(ATTN_KV_HEADS):
        heads = [h * ATTN_GROUPS + g for g in range(ATTN_GROUPS)]
        s = jnp.concatenate(
            [s_l[h][g * ATTN_BLOCK:(g + 1) * ATTN_BLOCK] - float(slopes[hq]) * clipdist
             for g, hq in enumerate(heads)], axis=0)
        sink = jnp.concatenate(
            [jnp.full((ATTN_BLOCK, 1), sink_ref[hq] * LOG2E, F32) for hq in heads], axis=0)
        m = jnp.maximum(jnp.max(s, axis=-1, keepdims=True), sink)
        p = jnp.exp2(s - m)
        denom = jnp.sum(p, axis=-1, keepdims=True) + jnp.exp2(sink - m)
        o = _bdot(p, v_l[h]) / denom
        for j in range(ATTN_GROUPS // 2):
            pair = jnp.concatenate([o[(2 * j) * ATTN_BLOCK:(2 * j + 1) * ATTN_BLOCK],
                                    o[(2 * j + 1) * ATTN_BLOCK:(2 * j + 2) * ATTN_BLOCK]], axis=1)
            cols = slice(heads[2 * j] * ATTN_HEAD_DIM, (heads[2 * j] + 2) * ATTN_HEAD_DIM)
            o_ref[:, cols] = (pair * _silu(gate_ref[:, cols].astype(F32))).astype(BF16)


def _attention(q, kv, gate, sinks, qnw, knw):
    b, lp, _ = q.shape
    nb = lp // ATTN_BLOCK
    kv_block = (None, ATTN_BLOCK, 2 * ATTN_KV_WIDTH)
    wide_block = (None, ATTN_BLOCK, ATTN_WIDTH)
    return pl.pallas_call(
        _attn_kernel,
        grid=(b, nb),
        in_specs=[pl.BlockSpec(memory_space=pltpu.SMEM),
                  pl.BlockSpec(wide_block, lambda bi, n: (bi, n, 0)),
                  pl.BlockSpec(kv_block, lambda bi, n: (bi, META_BLOCK, 0)),
                  pl.BlockSpec(kv_block, lambda bi, n: (bi, jnp.maximum(n - 1, 0), 0)),
                  pl.BlockSpec(kv_block, lambda bi, n: (bi, n, 0)),
                  pl.BlockSpec(wide_block, lambda bi, n: (bi, n, 0)),
                  pl.BlockSpec((1, ATTN_HEAD_DIM), lambda bi, n: (0, 0)),
                  pl.BlockSpec((1, ATTN_HEAD_DIM), lambda bi, n: (0, 0))],
        out_specs=pl.BlockSpec(wide_block, lambda bi, n: (bi, n, 0)),
        out_shape=jax.ShapeDtypeStruct((b, lp, ATTN_WIDTH), BF16),
        compiler_params=pltpu.CompilerParams(dimension_semantics=("parallel", "parallel"),
                                             vmem_limit_bytes=VMEM_LIMIT),
        name="attn_core",
    )(sinks, q, kv, kv, kv, gate, qnw, knw)


def _attn_out_kernel(h_ref, og_ref, w_ref, o_ref):
    o_ref[...] = h_ref[...] + jnp.dot(og_ref[...], w_ref[...], preferred_element_type=F32)


def _attn_out(h2d, og2d, w_out):
    rows, d = h2d.shape
    kdim = og2d.shape[1]
    return pl.pallas_call(
        _attn_out_kernel,
        grid=(rows // ROW_TILE,),
        in_specs=[pl.BlockSpec((ROW_TILE, d), lambda i: (i, 0)),
                  pl.BlockSpec((ROW_TILE, kdim), lambda i: (i, 0)),
                  pl.BlockSpec((kdim, d), lambda i: (0, 0))],
        out_specs=pl.BlockSpec((ROW_TILE, d), lambda i: (i, 0)),
        out_shape=jax.ShapeDtypeStruct((rows, d), F32),
        compiler_params=pltpu.CompilerParams(dimension_semantics=("parallel",),
                                             vmem_limit_bytes=VMEM_LIMIT),
        name="attn_out",
    )(h2d, og2d, w_out)


def _dn_in_kernel(h_ref, nw_ref, wqkv_ref, wz_ref, wba_ref, convw_ref, alog_ref, dtb_ref, tri_ref,
                  q_ref, k_ref, v_ref, z_ref, bg_ref, buf_ref, ybuf_ref):
    i = pl.program_id(1)
    halo = SUBLANES

    @pl.when(i == 0)
    def _():
        buf_ref[:, 0:halo, :] = jnp.zeros((buf_ref.shape[0], halo, LANES), F32)

    xn = _rms_rows(h_ref[...], nw_ref[...]).astype(BF16)
    z_ref[...] = jnp.dot(xn, wz_ref[...], preferred_element_type=F32).astype(BF16)

    ba = jnp.dot(xn, wba_ref[...], preferred_element_type=F32)
    lane = lax.broadcasted_iota(jnp.int32, ba.shape, 1)
    row = i * ROW_TILE + lax.broadcasted_iota(jnp.int32, ba.shape, 0)
    x = ba + dtb_ref[...]
    softplus = jnp.maximum(x, 0.0) + jnp.log1p(jnp.exp(-jnp.abs(x)))
    g = -jnp.exp(alog_ref[...]) * softplus
    bg = jnp.where(lane < DN_V_HEADS, jax.nn.sigmoid(ba), g)
    bg = jnp.where(row >= PAD, bg, 0.0)
    csum = jnp.dot(tri_ref[...], bg, preferred_element_type=F32, precision=lax.Precision.HIGHEST)
    bg_ref[...] = jnp.where(lane < DN_V_HEADS, bg, csum)

    q_scale = DN_HEAD_DIM ** -0.5
    rows_per_phase = ROW_TILE // CONV_PHASES
    slabs_per_chunk = CONV_COLS // LANES
    for cj in range(DN_CONV_WIDTH // CONV_COLS):
        u = jnp.dot(xn, wqkv_ref[:, cj * CONV_COLS:(cj + 1) * CONV_COLS],
                    preferred_element_type=F32)
        for sl in range(slabs_per_chunk):
            slab = cj * slabs_per_chunk + sl
            col0 = slab * LANES
            buf_ref[slab, halo:halo + ROW_TILE, :] = u[:, sl * LANES:(sl + 1) * LANES]
            w4 = convw_ref[:, col0:col0 + LANES]
            taps = {}
            for start in range(halo - (DN_CONV - 1), halo + CONV_PHASES):
                taps[start] = buf_ref[slab, pl.ds(start, rows_per_phase, stride=CONV_PHASES), :]
            for a in range(CONV_PHASES):
                acc = None
                for j in range(DN_CONV):
                    term = w4[DN_CONV - 1 - j:DN_CONV - j] * taps[halo + a - j]
                    acc = term if acc is None else acc + term
                ybuf_ref[sl, pl.ds(a, rows_per_phase, stride=CONV_PHASES), :] = _silu(acc)
            buf_ref[slab, 0:halo, :] = buf_ref[slab, ROW_TILE:ROW_TILE + halo, :]
            y = ybuf_ref[sl]
            if col0 < 2 * DN_KEY_WIDTH:
                dst, base, mul = ((q_ref, col0, q_scale) if col0 < DN_KEY_WIDTH
                                  else (k_ref, col0 - DN_KEY_WIDTH, 1.0))
                inv = lax.rsqrt(jnp.sum(y * y, axis=-1, keepdims=True) + NORM_EPS) * mul
                dst[:, base:base + LANES] = (y * inv).astype(BF16)
            else:
                base = col0 - 2 * DN_KEY_WIDTH
                v_ref[:, base:base + LANES] = y.astype(BF16)


def _dn_in(h, norm_w, w_qkv, w_z, w_ba, conv_w, alog_row, dtb_row):
    b, lp, d = h.shape
    nt = lp // ROW_TILE
    idx = np.arange(ROW_TILE)
    tri = jnp.asarray((idx[:, None] >= idx[None, :])
                      & (idx[:, None] // DN_CHUNK == idx[None, :] // DN_CHUNK), F32)
    row_block = lambda width: pl.BlockSpec((None, ROW_TILE, width), lambda bi, i: (bi, i, 0))
    full = lambda shape: pl.BlockSpec(shape, lambda bi, i: (0,) * len(shape))
    return pl.pallas_call(
        _dn_in_kernel,
        grid=(b, nt),
        in_specs=[row_block(d), full((1, d)), full(w_qkv.shape), full(w_z.shape), full(w_ba.shape),
                  full(conv_w.shape), full((1, LANES)), full((1, LANES)), full(tri.shape)],
        out_specs=[row_block(DN_KEY_WIDTH), row_block(DN_KEY_WIDTH), row_block(DN_VALUE_WIDTH),
                   row_block(DN_VALUE_WIDTH), row_block(LANES)],
        out_shape=[jax.ShapeDtypeStruct((b, lp, DN_KEY_WIDTH), BF16),
                   jax.ShapeDtypeStruct((b, lp, DN_KEY_WIDTH), BF16),
                   jax.ShapeDtypeStruct((b, lp, DN_VALUE_WIDTH), BF16),
                   jax.ShapeDtypeStruct((b, lp, DN_VALUE_WIDTH), BF16),
                   jax.ShapeDtypeStruct((b, lp, LANES), F32)],
        scratch_shapes=[pltpu.VMEM((DN_CONV_WIDTH // LANES, ROW_TILE + SUBLANES, LANES), F32),
                        pltpu.VMEM((CONV_COLS // LANES, ROW_TILE, LANES), F32)],
        compiler_params=pltpu.CompilerParams(dimension_semantics=("parallel", "arbitrary"),
                                             vmem_limit_bytes=VMEM_LIMIT),
        name="dn_in",
    )(h, norm_w, w_qkv, w_z, w_ba, conv_w, alog_row, dtb_row, tri)


def _dn_core_kernel(q_ref, k_ref, v_ref, bg_ref, o_ref, s_ref):
    @pl.when(pl.program_id(0) == 0)
    def _():
        s_ref[...] = jnp.zeros(s_ref.shape, F32)

    n_seq = q_ref.shape[0]
    reps = DN_V_HEADS // DN_K_HEADS
    ri = lax.broadcasted_iota(jnp.int32, (DN_CHUNK, DN_CHUNK), 0)
    ci = lax.broadcasted_iota(jnp.int32, (DN_CHUNK, DN_CHUNK), 1)
    causal = ri >= ci
    strict = ri > ci
    eye = (ri == ci).astype(F32)
    lane_lo = lax.broadcasted_iota(jnp.int32, (DN_CHUNK, 2 * DN_CHUNK), 1) < DN_CHUNK

    insts = [(b, c) for c in range(DN_STEP_CHUNKS) for b in range(n_seq)]
    q_d, k_d, qk_d, kk_d, kt_d = {}, {}, {}, {}, {}
    for (b, c) in insts:
        rows = slice(c * DN_CHUNK, (c + 1) * DN_CHUNK)
        for kh in range(DN_K_HEADS):
            kcols = slice(kh * DN_HEAD_DIM, (kh + 1) * DN_HEAD_DIM)
            q = q_ref[b, rows, kcols]
            k = k_ref[b, rows, kcols]
            qk_kk = _bdot_nt(jnp.concatenate([q, k], axis=0), k)
            key = (b, c, kh)
            q_d[key], k_d[key] = q, k
            qk_d[key], kk_d[key] = qk_kk[:DN_CHUNK], qk_kk[DN_CHUNK:]
            kt_d[key] = k.astype(F32).T

    all_heads = [(b, c, h) for (b, c) in insts for h in range(DN_V_HEADS)]
    decay_d, y_d, rhs_d, egc_col_d, gc_row_d, g_last_d = {}, {}, {}, {}, {}, {}
    for (b, c) in insts:
        rows = slice(c * DN_CHUNK, (c + 1) * DN_CHUNK)
        bg = bg_ref[b, rows, :]
        egc = jnp.exp(bg)
        bg_t = bg.T
        for h in range(DN_V_HEADS):
            key, kkey = (b, c, h), (b, c, h // reps)
            gl = DN_V_HEADS + h
            beta_col = bg[:, h:h + 1]
            gc_col = bg[:, gl:gl + 1]
            gc_row = bg_t[gl:gl + 1, :]
            egc_col = egc[:, gl:gl + 1]
            decay = jnp.exp(jnp.where(causal, gc_col - gc_row, NEG_BIG))
            decay_d[key] = decay
            n1 = jnp.where(strict, kk_d[kkey] * decay, 0.0) * (-beta_col)
            y_d[key] = jnp.concatenate([n1, eye], axis=1)
            v = v_ref[b, rows, h * DN_HEAD_DIM:(h + 1) * DN_HEAD_DIM].astype(F32)
            rhs_d[key] = jnp.concatenate(
                [(v * beta_col).astype(BF16),
                 (k_d[kkey].astype(F32) * (beta_col * egc_col)).astype(BF16)], axis=1)
            egc_col_d[key] = egc_col
            gc_row_d[key] = gc_row
            g_last_d[key] = gc_col[DN_CHUNK - 1:DN_CHUNK, :]

    for _ in range(6):
        r_d = {key: _bdot(y_d[key][:, :DN_CHUNK], y_d[key]) for key in all_heads}
        y_d = {key: jnp.where(lane_lo, r_d[key], y_d[key] + r_d[key]) for key in all_heads}
    uw_d = {key: _bdot(y_d[key][:, DN_CHUNK:], rhs_d[key]) for key in all_heads}

    s_d = {(b, h): s_ref[b, h] for b in range(n_seq) for h in range(DN_V_HEADS)}
    for c in range(DN_STEP_CHUNKS):
        rows = slice(c * DN_CHUNK, (c + 1) * DN_CHUNK)
        chunk_heads = [(b, h) for b in range(n_seq) for h in range(DN_V_HEADS)]
        ws_qs_d = {}
        for (b, h) in chunk_heads:
            key = (b, c, h)
            lhs = jnp.concatenate([uw_d[key][:, DN_HEAD_DIM:].astype(BF16), q_d[(b, c, h // reps)]], axis=0)
            ws_qs_d[key] = _bdot(lhs, s_d[(b, h)])
        av_d = {}
        for (b, h) in chunk_heads:
            key, kkey = (b, c, h), (b, c, h // reps)
            v_new = uw_d[key][:, :DN_HEAD_DIM] - ws_qs_d[key][:DN_CHUNK]
            attn = jnp.where(causal, qk_d[kkey] * decay_d[key], 0.0)
            k_state_t = kt_d[kkey] * jnp.exp(g_last_d[key] - gc_row_d[key])
            av_d[key] = _bdot(jnp.concatenate([attn, k_state_t], axis=0), v_new)
        for (b, h) in chunk_heads:
            key = (b, c, h)
            o = egc_col_d[key] * ws_qs_d[key][DN_CHUNK:] + av_d[key][:DN_CHUNK]
            o_ref[b, rows, h * DN_HEAD_DIM:(h + 1) * DN_HEAD_DIM] = o.astype(BF16)
            s_d[(b, h)] = s_d[(b, h)] * jnp.exp(g_last_d[key]) + av_d[key][DN_CHUNK:]
    for (b, h), s_new in s_d.items():
        s_ref[b, h] = s_new


def _dn_core(q, k, v, bg):
    b, lp, _ = q.shape
    rows = DN_STEP_CHUNKS * DN_CHUNK
    blk = lambda width: pl.BlockSpec((b, rows, width), lambda i: (0, i, 0))
    return pl.pallas_call(
        _dn_core_kernel,
        grid=(lp // rows,),
        in_specs=[blk(DN_KEY_WIDTH), blk(DN_KEY_WIDTH), blk(DN_VALUE_WIDTH), blk(LANES)],
        out_specs=blk(DN_VALUE_WIDTH),
        out_shape=jax.ShapeDtypeStruct((b, lp, DN_VALUE_WIDTH), BF16),
        scratch_shapes=[pltpu.VMEM((b, DN_V_HEADS, DN_HEAD_DIM, DN_HEAD_DIM), F32)],
        compiler_params=pltpu.CompilerParams(dimension_semantics=("arbitrary",),
                                             vmem_limit_bytes=VMEM_LIMIT),
        name="dn_core",
    )(q, k, v, bg)


def _dn_out_kernel(h_ref, o_ref, z_ref, onw_ref, w_ref, out_ref):
    onw = onw_ref[...]
    acc = h_ref[...]
    heads_per_chunk = DN_OUT_K_CHUNK // DN_HEAD_DIM
    for c in range(DN_VALUE_WIDTH // DN_OUT_K_CHUNK):
        parts = []
        for hh in range(c * heads_per_chunk, (c + 1) * heads_per_chunk):
            cols = slice(hh * DN_HEAD_DIM, (hh + 1) * DN_HEAD_DIM)
            y = _rms_rows(o_ref[:, cols].astype(F32), onw) * _silu(z_ref[:, cols].astype(F32))
            parts.append(y.astype(BF16))
        acc = acc + jnp.dot(jnp.concatenate(parts, axis=1),
                            w_ref[c * DN_OUT_K_CHUNK:(c + 1) * DN_OUT_K_CHUNK, :],
                            preferred_element_type=F32)
    out_ref[...] = acc


def _dn_out(h, o, z, onw, w_out, seq):
    b, lp, d = h.shape
    skip = (lp - seq) // ROW_TILE
    in_block = lambda width: pl.BlockSpec((None, ROW_TILE, width), lambda bi, i: (bi, i + skip, 0))
    return pl.pallas_call(
        _dn_out_kernel,
        grid=(b, seq // ROW_TILE),
        in_specs=[in_block(d), in_block(DN_VALUE_WIDTH), in_block(DN_VALUE_WIDTH),
                  pl.BlockSpec((1, DN_HEAD_DIM), lambda bi, i: (0, 0)),
                  pl.BlockSpec(w_out.shape, lambda bi, i: (0, 0))],
        out_specs=pl.BlockSpec((None, ROW_TILE, d), lambda bi, i: (bi, i, 0)),
        out_shape=jax.ShapeDtypeStruct((b, seq, d), F32),
        compiler_params=pltpu.CompilerParams(dimension_semantics=("parallel", "parallel"),
                                             vmem_limit_bytes=VMEM_LIMIT),
        name="dn_out",
    )(h, o, z, onw, w_out)


def kernel(x, meta_tokens, attn_norm_w, attn_w_in, attn_q_norm_w, attn_k_norm_w, attn_sinks,
           attn_w_out, dn_norm_w, dn_w_in, dn_conv_w, dn_a_log, dn_dt_bias, dn_o_norm_w, dn_w_out):
    b, seq, d = x.shape
    assert seq % ROW_TILE == 0 and attn_norm_w.shape[0] == 1 and dn_norm_w.shape[0] == 1
    lp = seq + ROW_TILE
    meta = jnp.broadcast_to(meta_tokens.astype(x.dtype)[None], (b, N_META, d))
    hp = jnp.concatenate([jnp.zeros((b, PAD, d), x.dtype), meta, x], axis=1)

    q, kv, gate = _attn_in(hp.reshape(b * lp, d), attn_norm_w[0][None],
                           attn_w_in[0].astype(BF16))
    og = _attention(q.reshape(b, lp, -1), kv.reshape(b, lp, -1), gate.reshape(b, lp, -1),
                    attn_sinks[0], attn_q_norm_w[0][None], attn_k_norm_w[0][None])
    h1 = _attn_out(hp.reshape(b * lp, d), og.reshape(b * lp, -1),
                   attn_w_out[0].astype(BF16)).reshape(b, lp, d)

    w_in = dn_w_in[0]
    w_qkv = w_in[:, :DN_CONV_WIDTH].astype(BF16)
    w_z = w_in[:, DN_CONV_WIDTH:DN_CONV_WIDTH + DN_VALUE_WIDTH].astype(BF16)
    n_ba = 2 * DN_V_HEADS
    w_ba = jnp.pad(w_in[:, DN_CONV_WIDTH + DN_VALUE_WIDTH:], ((0, 0), (0, LANES - n_ba))).astype(BF16)
    lane_pad = lambda t: jnp.pad(t[None], ((0, 0), (DN_V_HEADS, LANES - n_ba)))
    qd, kd, vd, zd, bg = _dn_in(h1, dn_norm_w[0][None], w_qkv, w_z, w_ba, dn_conv_w[0],
                                lane_pad(dn_a_log[0]), lane_pad(dn_dt_bias[0]))
    od = _dn_core(qd, kd, vd, bg)
    return _dn_out(h1, od, zd, dn_o_norm_w[0][None], dn_w_out[0].astype(BF16), seq)
```

```python
import numpy as np
import jax
import jax.numpy as jnp
from jax import lax
from jax.experimental import pallas as pl
from jax.experimental.pallas import tpu as pltpu

F32 = jnp.float32
BF16 = jnp.bfloat16

N_META = 16
NORM_EPS = 1e-6

ATTN_HEAD_DIM = 64
ATTN_HEADS = 16
ATTN_KV_HEADS = 2
ATTN_GROUPS = ATTN_HEADS // ATTN_KV_HEADS
ATTN_WIDTH = ATTN_HEADS * ATTN_HEAD_DIM
ATTN_KV_WIDTH = ATTN_KV_HEADS * ATTN_HEAD_DIM
WINDOW = 128
ATTN_BLOCK = 128

DN_HEAD_DIM = 128
DN_K_HEADS = 8
DN_V_HEADS = 16
DN_KEY_WIDTH = DN_K_HEADS * DN_HEAD_DIM
DN_VALUE_WIDTH = DN_V_HEADS * DN_HEAD_DIM
DN_CONV = 4
DN_CHUNK = 64
DN_CONV_WIDTH = 2 * DN_KEY_WIDTH + DN_VALUE_WIDTH

LANES = 128
SUBLANES = 8
ROW_TILE = 256
PAD = ROW_TILE - N_META
META_BLOCK = PAD // ATTN_BLOCK
META_OFFSET = PAD % ATTN_BLOCK
CONV_COLS = 512
CONV_PHASES = 4
DN_OUT_K_CHUNK = 512
DN_STEP_CHUNKS = 2
VMEM_LIMIT = 56 * 1024 * 1024

LOG2E = 1.4426950408889634
NEG_BIG = -1e30
CLIP_INVALID = 1e30


def _alibi_slopes(n_heads):
    return np.exp2(-8.0 * np.arange(1, n_heads + 1) / n_heads).astype(np.float32)


def _bdot(a, b):
    return jnp.dot(a.astype(BF16), b.astype(BF16), preferred_element_type=F32)


def _bdot_nt(a, b):
    return lax.dot_general(a.astype(BF16), b.astype(BF16), (((1,), (1,)), ((), ())),
                           preferred_element_type=F32)


def _rms_rows(x, w):
    return x * lax.rsqrt(jnp.mean(x * x, axis=-1, keepdims=True) + NORM_EPS) * w


def _silu(x):
    return x * jax.nn.sigmoid(x)


def _layer_input(head_ref, x_ref):
    return jnp.where(pl.program_id(1) == 0, head_ref[...], x_ref[...])


def _attn_in_kernel(head_ref, x_ref, nw_ref, w_ref, q_ref, kv_ref, g_ref):
    xn = _rms_rows(_layer_input(head_ref, x_ref), nw_ref[...]).astype(BF16)
    q_ref[...] = jnp.dot(xn, w_ref[:, :ATTN_WIDTH], preferred_element_type=F32).astype(BF16)
    kv_ref[...] = jnp.dot(xn, w_ref[:, ATTN_WIDTH:ATTN_WIDTH + 2 * ATTN_KV_WIDTH],
                          preferred_element_type=F32).astype(BF16)
    g_ref[...] = jnp.dot(xn, w_ref[:, ATTN_WIDTH + 2 * ATTN_KV_WIDTH:],
                         preferred_element_type=F32).astype(BF16)


def _padded_input_specs(d):
    return [pl.BlockSpec((None, ROW_TILE, d), lambda bi, i: (bi, 0, 0)),
            pl.BlockSpec((None, ROW_TILE, d), lambda bi, i: (bi, jnp.maximum(i - 1, 0), 0))]


def _attn_in(head, x, norm_w, w_in):
    b, seq, d = x.shape
    lp = seq + ROW_TILE
    n_in = w_in.shape[1]
    row_block = lambda width: pl.BlockSpec((None, ROW_TILE, width), lambda bi, i: (bi, i, 0))
    return pl.pallas_call(
        _attn_in_kernel,
        grid=(b, lp // ROW_TILE),
        in_specs=_padded_input_specs(d) + [pl.BlockSpec((1, d), lambda bi, i: (0, 0)),
                                           pl.BlockSpec((d, n_in), lambda bi, i: (0, 0))],
        out_specs=[row_block(ATTN_WIDTH), row_block(2 * ATTN_KV_WIDTH), row_block(ATTN_WIDTH)],
        out_shape=[jax.ShapeDtypeStruct((b, lp, ATTN_WIDTH), BF16),
                   jax.ShapeDtypeStruct((b, lp, 2 * ATTN_KV_WIDTH), BF16),
                   jax.ShapeDtypeStruct((b, lp, ATTN_WIDTH), BF16)],
        compiler_params=pltpu.CompilerParams(dimension_semantics=("parallel", "parallel"),
                                             vmem_limit_bytes=VMEM_LIMIT),
        name="attn_in",
    )(head, x, norm_w, w_in)


def _attn_kernel(sink_ref, q_ref, kvm_ref, kvp_ref, kvc_ref, gate_ref, qnw_ref, knw_ref, o_ref):
    nref = pl.program_id(1) - META_BLOCK

    def clip(dist, valid):
        return jnp.where(valid, jnp.minimum(dist, WINDOW).astype(F32), CLIP_INVALID)

    jm = lax.broadcasted_iota(jnp.int32, (N_META, ATTN_BLOCK), 0)
    qm = lax.broadcasted_iota(jnp.int32, (N_META, ATTN_BLOCK), 1)
    dist_m = nref * ATTN_BLOCK + qm - (META_OFFSET + jm)
    c = lax.broadcasted_iota(jnp.int32, (ATTN_BLOCK, ATTN_BLOCK), 0)
    qi = lax.broadcasted_iota(jnp.int32, (ATTN_BLOCK, ATTN_BLOCK), 1)
    clipdist = jnp.concatenate(
        [clip(dist_m, dist_m >= 0),
         clip(ATTN_BLOCK + qi - c, jnp.logical_and(c > qi, nref >= 2)),
         clip(qi - c, jnp.logical_and(c <= qi, nref >= 1))], axis=0)
    slopes = _alibi_slopes(ATTN_HEADS) * np.float32(LOG2E)
    qnw = qnw_ref[...] * (ATTN_HEAD_DIM ** -0.5 * LOG2E)
    knw = knw_ref[...]

    s_l, v_l = [], []
    for h in range(ATTN_KV_HEADS):
        ks = slice(h * ATTN_HEAD_DIM, (h + 1) * ATTN_HEAD_DIM)
        vs = slice(ATTN_KV_WIDTH + h * ATTN_HEAD_DIM, ATTN_KV_WIDTH + (h + 1) * ATTN_HEAD_DIM)
        k_ext = jnp.concatenate([kvm_ref[:, ks], kvp_ref[:, ks], kvc_ref[:, ks]], axis=0)
        k_ext = _rms_rows(k_ext.astype(F32), knw).astype(BF16)
        v_l.append(jnp.concatenate([kvm_ref[:, vs], kvp_ref[:, vs], kvc_ref[:, vs]], axis=0))
        group_cols = slice(h * ATTN_GROUPS * ATTN_HEAD_DIM, (h + 1) * ATTN_GROUPS * ATTN_HEAD_DIM)
        q_t = q_ref[:, group_cols].astype(F32).T
        qn = []
        for g in range(ATTN_GROUPS):
            qg = q_t[g * ATTN_HEAD_DIM:(g + 1) * ATTN_HEAD_DIM]
            inv = lax.rsqrt(jnp.mean(qg * qg, axis=0, keepdims=True) + NORM_EPS)
            qn.append((qg * inv * qnw).astype(BF16))
        s_l.append(jnp.dot(k_ext, jnp.concatenate(qn, axis=1),
                           preferred_element_type=F32))

    for h in range(ATTN_KV_HEADS):
        p_l, rinv_l = [], []
        for g in range(ATTN_GROUPS):
            hq = h * ATTN_GROUPS + g
            s = s_l[h][:, g * ATTN_BLOCK:(g + 1) * ATTN_BLOCK] - float(slopes[hq]) * clipdist
            sink = sink_ref[hq] * LOG2E
            m = jnp.maximum(jnp.max(s, axis=0, keepdims=True), sink)
            p = jnp.exp2(s - m)
            rinv_l.append(1.0 / (jnp.sum(p, axis=0, keepdims=True) + jnp.exp2(sink - m)))
            p_l.append(p.astype(BF16))
        o_t = lax.dot_general(v_l[h], jnp.concatenate(p_l, axis=1), (((0,), (0,)), ((), ())),
                              preferred_element_type=F32)
        o_t = jnp.concatenate(
            [o_t[:, g * ATTN_BLOCK:(g + 1) * ATTN_BLOCK] * rinv_l[g] for g in range(ATTN_GROUPS)],
            axis=0)
        group_cols = slice(h * ATTN_GROUPS * ATTN_HEAD_DIM, (h + 1) * ATTN_GROUPS * ATTN_HEAD_DIM)
        o_ref[:, group_cols] = (o_t.T * _silu(gate_ref[:, group_cols].astype(F32))).astype(BF16)


def _attention(q, kv, gate, sinks, qnw, knw):
    b, lp, _ = q.shape
    nb = lp // ATTN_BLOCK
    kv_block = (None, ATTN_BLOCK, 2 * ATTN_KV_WIDTH)
    wide_block = (None, ATTN_BLOCK, ATTN_WIDTH)
    qnw_lanes = jnp.broadcast_to(qnw.reshape(ATTN_HEAD_DIM, 1), (ATTN_HEAD_DIM, LANES))
    return pl.pallas_call(
        _attn_kernel,
        grid=(b, nb),
        in_specs=[pl.BlockSpec(memory_space=pltpu.SMEM),
                  pl.BlockSpec(wide_block, lambda bi, n: (bi, n, 0)),
                  pl.BlockSpec((None, N_META, 2 * ATTN_KV_WIDTH), lambda bi, n: (bi, PAD // N_META, 0)),
                  pl.BlockSpec(kv_block, lambda bi, n: (bi, jnp.maximum(n - 1, 0), 0)),
                  pl.BlockSpec(kv_block, lambda bi, n: (bi, n, 0)),
                  pl.BlockSpec(wide_block, lambda bi, n: (bi, n, 0)),
                  pl.BlockSpec((ATTN_HEAD_DIM, LANES), lambda bi, n: (0, 0)),
                  pl.BlockSpec((1, ATTN_HEAD_DIM), lambda bi, n: (0, 0))],
        out_specs=pl.BlockSpec(wide_block, lambda bi, n: (bi, n, 0)),
        out_shape=jax.ShapeDtypeStruct((b, lp, ATTN_WIDTH), BF16),
        compiler_params=pltpu.CompilerParams(dimension_semantics=("parallel", "parallel"),
                                             vmem_limit_bytes=VMEM_LIMIT),
        name="attn_core",
    )(sinks, q, kv, kv, kv, gate, qnw_lanes, knw)


def _dn_in_kernel(head_ref, og0_ref, xnext_ref, ognext_ref, wout_ref, nw_ref, wqkv_ref, wz_ref, wba_ref,
                  convw_ref, alog_ref, dtb_ref, tri_ref,
                  h1_ref, q_ref, k_ref, v_ref, z_ref, bg_ref, buf_ref, ybuf_ref, halo_ref, h1s_ref):
    i = pl.program_id(1)
    halo = SUBLANES

    @pl.when(i == 0)
    def _():
        halo_ref[...] = jnp.zeros(halo_ref.shape, F32)
        h1s_ref[...] = head_ref[...] + jnp.dot(og0_ref[...], wout_ref[...], preferred_element_type=F32)

    h1 = h1s_ref[...]
    h1_ref[...] = h1
    xn = _rms_rows(h1, nw_ref[...]).astype(BF16)

    def z_part(c0):
        def run():
            z_ref[:, c0:c0 + CONV_COLS] = jnp.dot(xn, wz_ref[:, c0:c0 + CONV_COLS],
                                                  preferred_element_type=F32).astype(BF16)
        return run

    def next_h1_part(c0):
        def run():
            h1s_ref[:, c0:c0 + CONV_COLS] = xnext_ref[:, c0:c0 + CONV_COLS] + jnp.dot(
                ognext_ref[...], wout_ref[:, c0:c0 + CONV_COLS], preferred_element_type=F32)
        return run

    fillers = ([z_part(c0) for c0 in range(0, DN_VALUE_WIDTH, CONV_COLS)]
               + [next_h1_part(c0) for c0 in range(0, h1.shape[1], CONV_COLS)])

    ba = jnp.dot(xn, wba_ref[...], preferred_element_type=F32)
    lane = lax.broadcasted_iota(jnp.int32, ba.shape, 1)
    row = i * ROW_TILE + lax.broadcasted_iota(jnp.int32, ba.shape, 0)
    x = ba + dtb_ref[...]
    softplus = jnp.maximum(x, 0.0) + jnp.log1p(jnp.exp(-jnp.abs(x)))
    g = -jnp.exp(alog_ref[...]) * softplus
    bg = jnp.where(lane < DN_V_HEADS, jax.nn.sigmoid(ba), g)
    bg = jnp.where(row >= PAD, bg, 0.0)
    csum = jnp.dot(tri_ref[...], bg, preferred_element_type=F32, precision=lax.Precision.HIGHEST)
    bg_ref[...] = jnp.where(lane < DN_V_HEADS, bg, csum)

    q_scale = DN_HEAD_DIM ** -0.5
    rows_per_phase = ROW_TILE // CONV_PHASES
    slabs_per_chunk = CONV_COLS // LANES
    for cj in range(DN_CONV_WIDTH // CONV_COLS):
        u = jnp.dot(xn, wqkv_ref[:, cj * CONV_COLS:(cj + 1) * CONV_COLS],
                    preferred_element_type=F32)
        if cj >= 1 and fillers:
            fillers.pop(0)()
        for sl in range(slabs_per_chunk):
            slab = cj * slabs_per_chunk + sl
            col0 = slab * LANES
            buf_ref[sl, 0:halo, :] = halo_ref[slab]
            buf_ref[sl, halo:halo + ROW_TILE, :] = u[:, sl * LANES:(sl + 1) * LANES]
            w4 = convw_ref[:, col0:col0 + LANES]
            taps = {}
            for start in range(halo - (DN_CONV - 1), halo + CONV_PHASES):
                taps[start] = buf_ref[sl, pl.ds(start, rows_per_phase, stride=CONV_PHASES), :]
            for a in range(CONV_PHASES):
                acc = None
                for j in range(DN_CONV):
                    term = w4[DN_CONV - 1 - j:DN_CONV - j] * taps[halo + a - j]
                    acc = term if acc is None else acc + term
                ybuf_ref[sl, pl.ds(a, rows_per_phase, stride=CONV_PHASES), :] = _silu(acc)
            halo_ref[slab] = buf_ref[sl, ROW_TILE:ROW_TILE + halo, :]
            y = ybuf_ref[sl]
            if col0 < 2 * DN_KEY_WIDTH:
                dst, base, mul = ((q_ref, col0, q_scale) if col0 < DN_KEY_WIDTH
                                  else (k_ref, col0 - DN_KEY_WIDTH, 1.0))
                inv = lax.rsqrt(jnp.sum(y * y, axis=-1, keepdims=True) + NORM_EPS) * mul
                dst[:, base:base + LANES] = (y * inv).astype(BF16)
            else:
                base = col0 - 2 * DN_KEY_WIDTH
                v_ref[:, base:base + LANES] = y.astype(BF16)

    for run in fillers:
        run()


def _dn_in(head, x, og, w_out, norm_w, w_qkv, w_z, w_ba, conv_w, alog_row, dtb_row):
    b, seq, d = x.shape
    lp = seq + ROW_TILE
    nt = lp // ROW_TILE
    n_x = seq // ROW_TILE
    idx = np.arange(ROW_TILE)
    tri = jnp.asarray((idx[:, None] >= idx[None, :])
                      & (idx[:, None] // DN_CHUNK == idx[None, :] // DN_CHUNK), F32)
    row_block = lambda width: pl.BlockSpec((None, ROW_TILE, width), lambda bi, i: (bi, i, 0))
    full = lambda shape: pl.BlockSpec(shape, lambda bi, i: (0,) * len(shape),
                                      pipeline_mode=pl.Buffered(1))
    return pl.pallas_call(
        _dn_in_kernel,
        grid=(b, nt),
        in_specs=[
            pl.BlockSpec((None, ROW_TILE, d), lambda bi, i: (bi, 0, 0)),
            pl.BlockSpec((None, ROW_TILE, og.shape[-1]), lambda bi, i: (bi, 0, 0)),
            pl.BlockSpec((None, ROW_TILE, d), lambda bi, i: (bi, jnp.minimum(i, n_x - 1), 0)),
            pl.BlockSpec((None, ROW_TILE, og.shape[-1]), lambda bi, i: (bi, jnp.minimum(i + 1, nt - 1), 0)),
            full(w_out.shape), full((1, d)), full(w_qkv.shape), full(w_z.shape),
            full(w_ba.shape), full(conv_w.shape), full((1, LANES)), full((1, LANES)), full(tri.shape)],
        out_specs=[row_block(d), row_block(DN_KEY_WIDTH), row_block(DN_KEY_WIDTH),
                   row_block(DN_VALUE_WIDTH), row_block(DN_VALUE_WIDTH), row_block(LANES)],
        out_shape=[jax.ShapeDtypeStruct((b, lp, d), F32),
                   jax.ShapeDtypeStruct((b, lp, DN_KEY_WIDTH), BF16),
                   jax.ShapeDtypeStruct((b, lp, DN_KEY_WIDTH), BF16),
                   jax.ShapeDtypeStruct((b, lp, DN_VALUE_WIDTH), BF16),
                   jax.ShapeDtypeStruct((b, lp, DN_VALUE_WIDTH), BF16),
                   jax.ShapeDtypeStruct((b, lp, LANES), F32)],
        scratch_shapes=[pltpu.VMEM((CONV_COLS // LANES, ROW_TILE + SUBLANES, LANES), F32),
                        pltpu.VMEM((CONV_COLS // LANES, ROW_TILE, LANES), F32),
                        pltpu.VMEM((DN_CONV_WIDTH // LANES, SUBLANES, LANES), F32),
                        pltpu.VMEM((ROW_TILE, d), F32)],
        compiler_params=pltpu.CompilerParams(dimension_semantics=("parallel", "arbitrary"),
                                             vmem_limit_bytes=VMEM_LIMIT),
        name="dn_in",
    )(head, og, x, og, w_out, norm_w, w_qkv, w_z, w_ba, conv_w, alog_row, dtb_row, tri)


def _dn_core_kernel(q_ref, k_ref, v_ref, bg_ref, o_ref, s_ref):
    @pl.when(pl.program_id(0) == 0)
    def _():
        s_ref[...] = jnp.zeros(s_ref.shape, F32)

    n_seq = q_ref.shape[0]
    reps = DN_V_HEADS // DN_K_HEADS
    ri = lax.broadcasted_iota(jnp.int32, (DN_CHUNK, DN_CHUNK), 0)
    ci = lax.broadcasted_iota(jnp.int32, (DN_CHUNK, DN_CHUNK), 1)
    causal = ri >= ci
    strict = ri > ci
    eye = (ri == ci).astype(F32)
    lane_lo = lax.broadcasted_iota(jnp.int32, (DN_CHUNK, 2 * DN_CHUNK), 1) < DN_CHUNK

    insts = [(b, c) for c in range(DN_STEP_CHUNKS) for b in range(n_seq)]
    q_d, k_d, qk_d, kk_d, kt_d = {}, {}, {}, {}, {}
    for (b, c) in insts:
        rows = slice(c * DN_CHUNK, (c + 1) * DN_CHUNK)
        for kh in range(DN_K_HEADS):
            kcols = slice(kh * DN_HEAD_DIM, (kh + 1) * DN_HEAD_DIM)
            q = q_ref[b, rows, kcols]
            k = k_ref[b, rows, kcols]
            qk_kk = _bdot_nt(jnp.concatenate([q, k], axis=0), k)
            key = (b, c, kh)
            q_d[key], k_d[key] = q, k
            qk_d[key], kk_d[key] = qk_kk[:DN_CHUNK], qk_kk[DN_CHUNK:]
            kt_d[key] = k.astype(F32).T

    all_heads = [(b, c, h) for (b, c) in insts for h in range(DN_V_HEADS)]
    decay_d, y_d, rhs_d, egc_col_d, gc_row_d, g_last_d = {}, {}, {}, {}, {}, {}
    for (b, c) in insts:
        rows = slice(c * DN_CHUNK, (c + 1) * DN_CHUNK)
        bg = bg_ref[b, rows, :]
        egc = jnp.exp(bg)
        bg_t = bg.T
        for h in range(DN_V_HEADS):
            key, kkey = (b, c, h), (b, c, h // reps)
            gl = DN_V_HEADS + h
            beta_col = bg[:, h:h + 1]
            gc_col = bg[:, gl:gl + 1]
            gc_row = bg_t[gl:gl + 1, :]
            egc_col = egc[:, gl:gl + 1]
            decay = jnp.exp(jnp.where(causal, gc_col - gc_row, NEG_BIG))
            decay_d[key] = decay
            n1 = jnp.where(strict, kk_d[kkey] * decay, 0.0) * (-beta_col)
            y_d[key] = jnp.concatenate([n1, eye], axis=1)
            v = v_ref[b, rows, h * DN_HEAD_DIM:(h + 1) * DN_HEAD_DIM].astype(F32)
            rhs_d[key] = jnp.concatenate(
                [(v * beta_col).astype(BF16),
                 (k_d[kkey].astype(F32) * (beta_col * egc_col)).astype(BF16)], axis=1)
            egc_col_d[key] = egc_col
            gc_row_d[key] = gc_row
            g_last_d[key] = gc_col[DN_CHUNK - 1:DN_CHUNK, :]

    for _ in range(6):
        r_d = {key: _bdot(y_d[key][:, :DN_CHUNK], y_d[key]) for key in all_heads}
        y_d = {key: jnp.where(lane_lo, r_d[key], y_d[key] + r_d[key]) for key in all_heads}
    uw_d = {key: _bdot(y_d[key][:, DN_CHUNK:], rhs_d[key]) for key in all_heads}

    s_d = {(b, h): s_ref[b, h] for b in range(n_seq) for h in range(DN_V_HEADS)}
    for c in range(DN_STEP_CHUNKS):
        rows = slice(c * DN_CHUNK, (c + 1) * DN_CHUNK)
        chunk_heads = [(b, h) for b in range(n_seq) for h in range(DN_V_HEADS)]
        ws_qs_d = {}
        for (b, h) in chunk_heads:
            key = (b, c, h)
            lhs = jnp.concatenate([uw_d[key][:, DN_HEAD_DIM:].astype(BF16), q_d[(b, c, h // reps)]], axis=0)
            ws_qs_d[key] = _bdot(lhs, s_d[(b, h)])
        av_d = {}
        for (b, h) in chunk_heads:
            key, kkey = (b, c, h), (b, c, h // reps)
            v_new = uw_d[key][:, :DN_HEAD_DIM] - ws_qs_d[key][:DN_CHUNK]
            attn = jnp.where(causal, qk_d[kkey] * decay_d[key], 0.0)
            k_state_t = kt_d[kkey] * jnp.exp(g_last_d[key] - gc_row_d[key])
            av_d[key] = _bdot(jnp.concatenate([attn, k_state_t], axis=0), v_new)
        for (b, h) in chunk_heads:
            key = (b, c, h)
            o = egc_col_d[key] * ws_qs_d[key][DN_CHUNK:] + av_d[key][:DN_CHUNK]
            o_ref[b, rows, h * DN_HEAD_DIM:(h + 1) * DN_HEAD_DIM] = o.astype(BF16)
            s_d[(b, h)] = s_d[(b, h)] * jnp.exp(g_last_d[key]) + av_d[key][DN_CHUNK:]
    for (b, h), s_new in s_d.items():
        s_ref[b, h] = s_new


def _dn_core(q, k, v, bg):
    b, lp, _ = q.shape
    rows = DN_STEP_CHUNKS * DN_CHUNK
    blk = lambda width: pl.BlockSpec((b, rows, width), lambda i: (0, i, 0))
    return pl.pallas_call(
        _dn_core_kernel,
        grid=(lp // rows,),
        in_specs=[blk(DN_KEY_WIDTH), blk(DN_KEY_WIDTH), blk(DN_VALUE_WIDTH), blk(LANES)],
        out_specs=blk(DN_VALUE_WIDTH),
        out_shape=jax.ShapeDtypeStruct((b, lp, DN_VALUE_WIDTH), BF16),
        scratch_shapes=[pltpu.VMEM((b, DN_V_HEADS, DN_HEAD_DIM, DN_HEAD_DIM), F32)],
        compiler_params=pltpu.CompilerParams(dimension_semantics=("arbitrary",),
                                             vmem_limit_bytes=VMEM_LIMIT),
        name="dn_core",
    )(q, k, v, bg)


def _dn_out_kernel(h_ref, o_ref, z_ref, onw_ref, w_ref, out_ref):
    onw = onw_ref[...]
    acc = h_ref[...]
    heads_per_chunk = DN_OUT_K_CHUNK // DN_HEAD_DIM
    for c in range(DN_VALUE_WIDTH // DN_OUT_K_CHUNK):
        parts = []
        for hh in range(c * heads_per_chunk, (c + 1) * heads_per_chunk):
            cols = slice(hh * DN_HEAD_DIM, (hh + 1) * DN_HEAD_DIM)
            y = _rms_rows(o_ref[:, cols].astype(F32), onw) * _silu(z_ref[:, cols].astype(F32))
            parts.append(y.astype(BF16))
        acc = acc + jnp.dot(jnp.concatenate(parts, axis=1),
                            w_ref[c * DN_OUT_K_CHUNK:(c + 1) * DN_OUT_K_CHUNK, :],
                            preferred_element_type=F32)
    out_ref[...] = acc


def _dn_out(h, o, z, onw, w_out, seq):
    b, lp, d = h.shape
    skip = (lp - seq) // ROW_TILE
    in_block = lambda width: pl.BlockSpec((None, ROW_TILE, width), lambda bi, i: (bi, i + skip, 0))
    return pl.pallas_call(
        _dn_out_kernel,
        grid=(b, seq // ROW_TILE),
        in_specs=[in_block(d), in_block(DN_VALUE_WIDTH), in_block(DN_VALUE_WIDTH),
                  pl.BlockSpec((1, DN_HEAD_DIM), lambda bi, i: (0, 0)),
                  pl.BlockSpec(w_out.shape, lambda bi, i: (0, 0))],
        out_specs=pl.BlockSpec((None, ROW_TILE, d), lambda bi, i: (bi, i, 0)),
        out_shape=jax.ShapeDtypeStruct((b, seq, d), F32),
        compiler_params=pltpu.CompilerParams(dimension_semantics=("parallel", "parallel"),
                                             vmem_limit_bytes=VMEM_LIMIT),
        name="dn_out",
    )(h, o, z, onw, w_out)


def kernel(x, meta_tokens, attn_norm_w, attn_w_in, attn_q_norm_w, attn_k_norm_w, attn_sinks,
           attn_w_out, dn_norm_w, dn_w_in, dn_conv_w, dn_a_log, dn_dt_bias, dn_o_norm_w, dn_w_out):
    b, seq, d = x.shape
    assert seq % ROW_TILE == 0 and attn_norm_w.shape[0] == 1 and dn_norm_w.shape[0] == 1
    meta = jnp.broadcast_to(meta_tokens.astype(x.dtype)[None], (b, N_META, d))
    head = jnp.concatenate([jnp.zeros((b, PAD, d), x.dtype), meta], axis=1)

    q, kv, gate = _attn_in(head, x, attn_norm_w[0][None], attn_w_in[0].astype(BF16))
    og = _attention(q, kv, gate, attn_sinks[0], attn_q_norm_w[0][None], attn_k_norm_w[0][None])

    w_in = dn_w_in[0]
    w_qkv = w_in[:, :DN_CONV_WIDTH].astype(BF16)
    w_z = w_in[:, DN_CONV_WIDTH:DN_CONV_WIDTH + DN_VALUE_WIDTH].astype(BF16)
    n_ba = 2 * DN_V_HEADS
    w_ba = jnp.pad(w_in[:, DN_CONV_WIDTH + DN_VALUE_WIDTH:], ((0, 0), (0, LANES - n_ba))).astype(BF16)
    lane_pad = lambda t: jnp.pad(t[None], ((0, 0), (DN_V_HEADS, LANES - n_ba)))
    h1, qd, kd, vd, zd, bg = _dn_in(head, x, og, attn_w_out[0].astype(BF16), dn_norm_w[0][None],
                                    w_qkv, w_z, w_ba, dn_conv_w[0],
                                    lane_pad(dn_a_log[0]), lane_pad(dn_dt_bias[0]))
    od = _dn_core(qd, kd, vd, bg)
    return _dn_out(h1, od, zd, dn_o_norm_w[0][None], dn_w_out[0].astype(BF16), seq)
```

```python
import numpy as np
import jax
import jax.numpy as jnp
from jax import lax
from jax.experimental import pallas as pl
from jax.experimental.pallas import tpu as pltpu

F32 = jnp.float32
BF16 = jnp.bfloat16

N_META = 16
NORM_EPS = 1e-6

ATTN_HEAD_DIM = 64
ATTN_HEADS = 16
ATTN_KV_HEADS = 2
ATTN_GROUPS = ATTN_HEADS // ATTN_KV_HEADS
ATTN_WIDTH = ATTN_HEADS * ATTN_HEAD_DIM
ATTN_KV_WIDTH = ATTN_KV_HEADS * ATTN_HEAD_DIM
WINDOW = 128
ATTN_BLOCK = 128

DN_HEAD_DIM = 128
DN_K_HEADS = 8
DN_V_HEADS = 16
DN_KEY_WIDTH = DN_K_HEADS * DN_HEAD_DIM
DN_VALUE_WIDTH = DN_V_HEADS * DN_HEAD_DIM
DN_CONV = 4
DN_CHUNK = 64
DN_CONV_WIDTH = 2 * DN_KEY_WIDTH + DN_VALUE_WIDTH

LANES = 128
SUBLANES = 8
ROW_TILE = 256
PAD = ROW_TILE - N_META
META_BLOCK = PAD // ATTN_BLOCK
META_OFFSET = PAD % ATTN_BLOCK
CONV_COLS = 512
CONV_PHASES = 4
DN_OUT_K_CHUNK = 512
DN_STEP_CHUNKS = 2
VMEM_LIMIT = 56 * 1024 * 1024

LOG2E = 1.4426950408889634
NEG_BIG = -1e30
CLIP_INVALID = 1e30


def _alibi_slopes(n_heads):
    return np.exp2(-8.0 * np.arange(1, n_heads + 1) / n_heads).astype(np.float32)


def _bdot(a, b):
    return jnp.dot(a.astype(BF16), b.astype(BF16), preferred_element_type=F32)


def _bdot_nt(a, b):
    return lax.dot_general(a.astype(BF16), b.astype(BF16), (((1,), (1,)), ((), ())),
                           preferred_element_type=F32)


def _rms_rows(x, w):
    return x * lax.rsqrt(jnp.mean(x * x, axis=-1, keepdims=True) + NORM_EPS) * w


def _silu(x):
    return x * jax.nn.sigmoid(x)


def _layer_input(head_ref, x_ref):
    return jnp.where(pl.program_id(1) == 0, head_ref[...], x_ref[...])


def _attn_in_kernel(head_ref, x_ref, nw_ref, w_ref, q_ref, kv_ref, g_ref):
    xn = _rms_rows(_layer_input(head_ref, x_ref), nw_ref[...]).astype(BF16)
    q_ref[...] = jnp.dot(xn, w_ref[:, :ATTN_WIDTH], preferred_element_type=F32).astype(BF16)
    kv_ref[...] = jnp.dot(xn, w_ref[:, ATTN_WIDTH:ATTN_WIDTH + 2 * ATTN_KV_WIDTH],
                          preferred_element_type=F32).astype(BF16)
    g_ref[...] = jnp.dot(xn, w_ref[:, ATTN_WIDTH + 2 * ATTN_KV_WIDTH:],
                         preferred_element_type=F32).astype(BF16)


def _padded_input_specs(d):
    return [pl.BlockSpec((None, ROW_TILE, d), lambda bi, i: (bi, 0, 0)),
            pl.BlockSpec((None, ROW_TILE, d), lambda bi, i: (bi, jnp.maximum(i - 1, 0), 0))]


def _attn_in(head, x, norm_w, w_in):
    b, seq, d = x.shape
    lp = seq + ROW_TILE
    n_in = w_in.shape[1]
    row_block = lambda width: pl.BlockSpec((None, ROW_TILE, width), lambda bi, i: (bi, i, 0))
    return pl.pallas_call(
        _attn_in_kernel,
        grid=(b, lp // ROW_TILE),
        in_specs=_padded_input_specs(d) + [pl.BlockSpec((1, d), lambda bi, i: (0, 0)),
                                           pl.BlockSpec((d, n_in), lambda bi, i: (0, 0))],
        out_specs=[row_block(ATTN_WIDTH), row_block(2 * ATTN_KV_WIDTH), row_block(ATTN_WIDTH)],
        out_shape=[jax.ShapeDtypeStruct((b, lp, ATTN_WIDTH), BF16),
                   jax.ShapeDtypeStruct((b, lp, 2 * ATTN_KV_WIDTH), BF16),
                   jax.ShapeDtypeStruct((b, lp, ATTN_WIDTH), BF16)],
        compiler_params=pltpu.CompilerParams(dimension_semantics=("parallel", "parallel"),
                                             vmem_limit_bytes=VMEM_LIMIT),
        name="attn_in",
    )(head, x, norm_w, w_in)


def _attn_kernel(sink_ref, q_ref, kvm_ref, kvp_ref, kvc_ref, gate_ref, qnw_ref, knw_ref, o_ref):
    nref = pl.program_id(1) - META_BLOCK

    def clip(dist, valid):
        return jnp.where(valid, jnp.minimum(dist, WINDOW).astype(F32), CLIP_INVALID)

    jm = lax.broadcasted_iota(jnp.int32, (N_META, ATTN_BLOCK), 0)
    qm = lax.broadcasted_iota(jnp.int32, (N_META, ATTN_BLOCK), 1)
    dist_m = nref * ATTN_BLOCK + qm - (META_OFFSET + jm)
    c = lax.broadcasted_iota(jnp.int32, (ATTN_BLOCK, ATTN_BLOCK), 0)
    qi = lax.broadcasted_iota(jnp.int32, (ATTN_BLOCK, ATTN_BLOCK), 1)
    clipdist = jnp.concatenate(
        [clip(dist_m, dist_m >= 0),
         clip(ATTN_BLOCK + qi - c, jnp.logical_and(c > qi, nref >= 2)),
         clip(qi - c, jnp.logical_and(c <= qi, nref >= 1))], axis=0)
    slopes = _alibi_slopes(ATTN_HEADS) * np.float32(LOG2E)
    qnw = qnw_ref[...] * (ATTN_HEAD_DIM ** -0.5 * LOG2E)
    knw = knw_ref[...]

    s_l, v_l = [], []
    for h in range(ATTN_KV_HEADS):
        ks = slice(h * ATTN_HEAD_DIM, (h + 1) * ATTN_HEAD_DIM)
        vs = slice(ATTN_KV_WIDTH + h * ATTN_HEAD_DIM, ATTN_KV_WIDTH + (h + 1) * ATTN_HEAD_DIM)
        k_ext = jnp.concatenate([kvm_ref[:, ks], kvp_ref[:, ks], kvc_ref[:, ks]], axis=0)
        k_ext = _rms_rows(k_ext.astype(F32), knw).astype(BF16)
        v_l.append(jnp.concatenate([kvm_ref[:, vs], kvp_ref[:, vs], kvc_ref[:, vs]], axis=0))
        group_cols = slice(h * ATTN_GROUPS * ATTN_HEAD_DIM, (h + 1) * ATTN_GROUPS * ATTN_HEAD_DIM)
        q_t = q_ref[:, group_cols].astype(F32).T
        qn = []
        for g in range(ATTN_GROUPS):
            qg = q_t[g * ATTN_HEAD_DIM:(g + 1) * ATTN_HEAD_DIM]
            inv = lax.rsqrt(jnp.mean(qg * qg, axis=0, keepdims=True) + NORM_EPS)
            qn.append((qg * inv * qnw).astype(BF16))
        s_l.append(jnp.dot(k_ext, jnp.concatenate(qn, axis=1),
                           preferred_element_type=F32))

    for h in range(ATTN_KV_HEADS):
        p_l, rinv_l = [], []
        for g in range(ATTN_GROUPS):
            hq = h * ATTN_GROUPS + g
            s = s_l[h][:, g * ATTN_BLOCK:(g + 1) * ATTN_BLOCK] - float(slopes[hq]) * clipdist
            sink = sink_ref[hq] * LOG2E
            m = jnp.maximum(jnp.max(s, axis=0, keepdims=True), sink)
            p = jnp.exp2(s - m)
            rinv_l.append(1.0 / (jnp.sum(p, axis=0, keepdims=True) + jnp.exp2(sink - m)))
            p_l.append(p.astype(BF16))
        o_t = lax.dot_general(v_l[h], jnp.concatenate(p_l, axis=1), (((0,), (0,)), ((), ())),
                              preferred_element_type=F32)
        o_t = jnp.concatenate(
            [o_t[:, g * ATTN_BLOCK:(g + 1) * ATTN_BLOCK] * rinv_l[g] for g in range(ATTN_GROUPS)],
            axis=0)
        group_cols = slice(h * ATTN_GROUPS * ATTN_HEAD_DIM, (h + 1) * ATTN_GROUPS * ATTN_HEAD_DIM)
        o_ref[:, group_cols] = (o_t.T * _silu(gate_ref[:, group_cols].astype(F32))).astype(BF16)


def _attention(q, kv, gate, sinks, qnw, knw):
    b, lp, _ = q.shape
    nb = lp // ATTN_BLOCK
    kv_block = (None, ATTN_BLOCK, 2 * ATTN_KV_WIDTH)
    wide_block = (None, ATTN_BLOCK, ATTN_WIDTH)
    qnw_lanes = jnp.broadcast_to(qnw.reshape(ATTN_HEAD_DIM, 1), (ATTN_HEAD_DIM, LANES))
    return pl.pallas_call(
        _attn_kernel,
        grid=(b, nb),
        in_specs=[pl.BlockSpec(memory_space=pltpu.SMEM),
                  pl.BlockSpec(wide_block, lambda bi, n: (bi, n, 0)),
                  pl.BlockSpec((None, N_META, 2 * ATTN_KV_WIDTH), lambda bi, n: (bi, PAD // N_META, 0)),
                  pl.BlockSpec(kv_block, lambda bi, n: (bi, jnp.maximum(n - 1, 0), 0)),
                  pl.BlockSpec(kv_block, lambda bi, n: (bi, n, 0)),
                  pl.BlockSpec(wide_block, lambda bi, n: (bi, n, 0)),
                  pl.BlockSpec((ATTN_HEAD_DIM, LANES), lambda bi, n: (0, 0)),
                  pl.BlockSpec((1, ATTN_HEAD_DIM), lambda bi, n: (0, 0))],
        out_specs=pl.BlockSpec(wide_block, lambda bi, n: (bi, n, 0)),
        out_shape=jax.ShapeDtypeStruct((b, lp, ATTN_WIDTH), BF16),
        compiler_params=pltpu.CompilerParams(dimension_semantics=("parallel", "parallel"),
                                             vmem_limit_bytes=VMEM_LIMIT),
        name="attn_core",
    )(sinks, q, kv, kv, kv, gate, qnw_lanes, knw)


def _dn_in_kernel(head_ref, og0_ref, xnext_ref, ognext_ref, wout_ref, nw_ref, wqkv_ref, wz_ref, wba_ref,
                  convw_ref, alog_ref, dtb_ref, tri_ref,
                  h1_ref, q_ref, k_ref, vb_ref, kbe_ref, qe_ref, z_ref, bg_ref,
                  buf_ref, ybuf_ref, halo_ref, h1s_ref):
    i = pl.program_id(1)
    halo = SUBLANES

    @pl.when(i == 0)
    def _():
        halo_ref[...] = jnp.zeros(halo_ref.shape, F32)
        h1s_ref[...] = head_ref[...] + jnp.dot(og0_ref[...], wout_ref[...], preferred_element_type=F32)

    h1 = h1s_ref[...]
    h1_ref[...] = h1
    xn = _rms_rows(h1, nw_ref[...]).astype(BF16)

    def z_part(c0):
        def run():
            z_ref[:, c0:c0 + CONV_COLS] = jnp.dot(xn, wz_ref[:, c0:c0 + CONV_COLS],
                                                  preferred_element_type=F32).astype(BF16)
        return run

    def next_h1_part(c0):
        def run():
            h1s_ref[:, c0:c0 + CONV_COLS] = xnext_ref[:, c0:c0 + CONV_COLS] + jnp.dot(
                ognext_ref[...], wout_ref[:, c0:c0 + CONV_COLS], preferred_element_type=F32)
        return run

    fillers = ([z_part(c0) for c0 in range(0, DN_VALUE_WIDTH, CONV_COLS)]
               + [next_h1_part(c0) for c0 in range(0, h1.shape[1], CONV_COLS)])

    ba = jnp.dot(xn, wba_ref[...], preferred_element_type=F32)
    lane = lax.broadcasted_iota(jnp.int32, ba.shape, 1)
    row = i * ROW_TILE + lax.broadcasted_iota(jnp.int32, ba.shape, 0)
    x = ba + dtb_ref[...]
    softplus = jnp.maximum(x, 0.0) + jnp.log1p(jnp.exp(-jnp.abs(x)))
    g = -jnp.exp(alog_ref[...]) * softplus
    bg = jnp.where(lane < DN_V_HEADS, jax.nn.sigmoid(ba), g)
    bg = jnp.where(row >= PAD, bg, 0.0)
    csum = jnp.dot(tri_ref[...], bg, preferred_element_type=F32, precision=lax.Precision.HIGHEST)
    bg = jnp.where(lane < DN_V_HEADS, bg, csum)
    bg_ref[...] = bg
    egc = jnp.exp(bg)
    reps = DN_V_HEADS // DN_K_HEADS

    q_scale = DN_HEAD_DIM ** -0.5
    rows_per_phase = ROW_TILE // CONV_PHASES
    slabs_per_chunk = CONV_COLS // LANES
    for cj in range(DN_CONV_WIDTH // CONV_COLS):
        u = jnp.dot(xn, wqkv_ref[:, cj * CONV_COLS:(cj + 1) * CONV_COLS],
                    preferred_element_type=F32)
        if cj >= 1 and fillers:
            fillers.pop(0)()
        for sl in range(slabs_per_chunk):
            slab = cj * slabs_per_chunk + sl
            col0 = slab * LANES
            buf_ref[sl, 0:halo, :] = halo_ref[slab]
            buf_ref[sl, halo:halo + ROW_TILE, :] = u[:, sl * LANES:(sl + 1) * LANES]
            w4 = convw_ref[:, col0:col0 + LANES]
            taps = {}
            for start in range(halo - (DN_CONV - 1), halo + CONV_PHASES):
                taps[start] = buf_ref[sl, pl.ds(start, rows_per_phase, stride=CONV_PHASES), :]
            for a in range(CONV_PHASES):
                acc = None
                for j in range(DN_CONV):
                    term = w4[DN_CONV - 1 - j:DN_CONV - j] * taps[halo + a - j]
                    acc = term if acc is None else acc + term
                ybuf_ref[sl, pl.ds(a, rows_per_phase, stride=CONV_PHASES), :] = _silu(acc)
            halo_ref[slab] = buf_ref[sl, ROW_TILE:ROW_TILE + halo, :]
            y = ybuf_ref[sl]
            if col0 < DN_KEY_WIDTH:
                kh = slab
                qn = y * (lax.rsqrt(jnp.sum(y * y, axis=-1, keepdims=True) + NORM_EPS) * q_scale)
                q_ref[:, col0:col0 + LANES] = qn.astype(BF16)
                for h in range(reps * kh, reps * (kh + 1)):
                    gl = DN_V_HEADS + h
                    qe_ref[:, h * LANES:(h + 1) * LANES] = (qn * egc[:, gl:gl + 1]).astype(BF16)
            elif col0 < 2 * DN_KEY_WIDTH:
                kh = slab - DN_K_HEADS
                kn = y * lax.rsqrt(jnp.sum(y * y, axis=-1, keepdims=True) + NORM_EPS)
                k_ref[:, kh * LANES:(kh + 1) * LANES] = kn.astype(BF16)
                for h in range(reps * kh, reps * (kh + 1)):
                    gl = DN_V_HEADS + h
                    kbe_ref[:, h * LANES:(h + 1) * LANES] = (
                        kn * (bg[:, h:h + 1] * egc[:, gl:gl + 1])).astype(BF16)
            else:
                h = slab - 2 * DN_K_HEADS
                vb_ref[:, h * LANES:(h + 1) * LANES] = (y * bg[:, h:h + 1]).astype(BF16)

    for run in fillers:
        run()


def _dn_in(head, x, og, w_out, norm_w, w_qkv, w_z, w_ba, conv_w, alog_row, dtb_row):
    b, seq, d = x.shape
    lp = seq + ROW_TILE
    nt = lp // ROW_TILE
    n_x = seq // ROW_TILE
    idx = np.arange(ROW_TILE)
    tri = jnp.asarray((idx[:, None] >= idx[None, :])
                      & (idx[:, None] // DN_CHUNK == idx[None, :] // DN_CHUNK), F32)
    row_block = lambda width: pl.BlockSpec((None, ROW_TILE, width), lambda bi, i: (bi, i, 0))
    full = lambda shape: pl.BlockSpec(shape, lambda bi, i: (0,) * len(shape),
                                      pipeline_mode=pl.Buffered(1))
    return pl.pallas_call(
        _dn_in_kernel,
        grid=(b, nt),
        in_specs=[
            pl.BlockSpec((None, ROW_TILE, d), lambda bi, i: (bi, 0, 0)),
            pl.BlockSpec((None, ROW_TILE, og.shape[-1]), lambda bi, i: (bi, 0, 0)),
            pl.BlockSpec((None, ROW_TILE, d), lambda bi, i: (bi, jnp.minimum(i, n_x - 1), 0)),
            pl.BlockSpec((None, ROW_TILE, og.shape[-1]), lambda bi, i: (bi, jnp.minimum(i + 1, nt - 1), 0)),
            full(w_out.shape), full((1, d)), full(w_qkv.shape), full(w_z.shape),
            full(w_ba.shape), full(conv_w.shape), full((1, LANES)), full((1, LANES)), full(tri.shape)],
        out_specs=[row_block(d), row_block(DN_KEY_WIDTH), row_block(DN_KEY_WIDTH),
                   row_block(DN_VALUE_WIDTH), row_block(DN_VALUE_WIDTH), row_block(DN_VALUE_WIDTH),
                   row_block(DN_VALUE_WIDTH), row_block(LANES)],
        out_shape=[jax.ShapeDtypeStruct((b, lp, d), F32),
                   jax.ShapeDtypeStruct((b, lp, DN_KEY_WIDTH), BF16),
                   jax.ShapeDtypeStruct((b, lp, DN_KEY_WIDTH), BF16),
                   jax.ShapeDtypeStruct((b, lp, DN_VALUE_WIDTH), BF16),
                   jax.ShapeDtypeStruct((b, lp, DN_VALUE_WIDTH), BF16),
                   jax.ShapeDtypeStruct((b, lp, DN_VALUE_WIDTH), BF16),
                   jax.ShapeDtypeStruct((b, lp, DN_VALUE_WIDTH), BF16),
                   jax.ShapeDtypeStruct((b, lp, LANES), F32)],
        scratch_shapes=[pltpu.VMEM((CONV_COLS // LANES, ROW_TILE + SUBLANES, LANES), F32),
                        pltpu.VMEM((CONV_COLS // LANES, ROW_TILE, LANES), F32),
                        pltpu.VMEM((DN_CONV_WIDTH // LANES, SUBLANES, LANES), F32),
                        pltpu.VMEM((ROW_TILE, d), F32)],
        compiler_params=pltpu.CompilerParams(dimension_semantics=("parallel", "arbitrary"),
                                             vmem_limit_bytes=VMEM_LIMIT),
        name="dn_in",
    )(head, og, x, og, w_out, norm_w, w_qkv, w_z, w_ba, conv_w, alog_row, dtb_row, tri)


def _dn_core_kernel(q_ref, k_ref, vb_ref, kbe_ref, qe_ref, bg_ref, o_ref, s_ref):
    @pl.when(pl.program_id(0) == 0)
    def _():
        s_ref[...] = jnp.zeros(s_ref.shape, F32)

    n_seq = q_ref.shape[0]
    reps = DN_V_HEADS // DN_K_HEADS
    ri = lax.broadcasted_iota(jnp.int32, (DN_CHUNK, DN_CHUNK), 0)
    ci = lax.broadcasted_iota(jnp.int32, (DN_CHUNK, DN_CHUNK), 1)
    causal = ri >= ci
    strict = ri > ci
    eye = (ri == ci).astype(F32)
    lane_lo = lax.broadcasted_iota(jnp.int32, (DN_CHUNK, 2 * DN_CHUNK), 1) < DN_CHUNK

    insts = [(b, c) for c in range(DN_STEP_CHUNKS) for b in range(n_seq)]
    all_heads = [(b, c, h) for (b, c) in insts for h in range(DN_V_HEADS)]

    qk_d, kk_d, kt_d = {}, {}, {}
    for (b, c) in insts:
        rows = slice(c * DN_CHUNK, (c + 1) * DN_CHUNK)
        for kh in range(DN_K_HEADS):
            kcols = slice(kh * DN_HEAD_DIM, (kh + 1) * DN_HEAD_DIM)
            k = k_ref[b, rows, kcols]
            qk_kk = _bdot_nt(jnp.concatenate([q_ref[b, rows, kcols], k], axis=0), k)
            key = (b, c, kh)
            qk_d[key], kk_d[key] = qk_kk[:DN_CHUNK], qk_kk[DN_CHUNK:]
            kt_d[key] = k.astype(F32).T

    decay_d, y_d, rhs_d, qe_d, gc_row_d, g_last_d = {}, {}, {}, {}, {}, {}
    for (b, c) in insts:
        rows = slice(c * DN_CHUNK, (c + 1) * DN_CHUNK)
        bg = bg_ref[b, rows, :]
        bg_t = bg.T
        for h in range(DN_V_HEADS):
            key, kkey = (b, c, h), (b, c, h // reps)
            gl = DN_V_HEADS + h
            beta_col = bg[:, h:h + 1]
            gc_col = bg[:, gl:gl + 1]
            gc_row = bg_t[gl:gl + 1, :]
            decay = jnp.exp(jnp.where(causal, gc_col - gc_row, NEG_BIG))
            decay_d[key] = decay
            n1 = jnp.where(strict, kk_d[kkey] * decay, 0.0) * (-beta_col)
            y_d[key] = jnp.concatenate([n1, eye], axis=1)
            hcols = slice(h * DN_HEAD_DIM, (h + 1) * DN_HEAD_DIM)
            rhs_d[key] = jnp.concatenate([vb_ref[b, rows, hcols], kbe_ref[b, rows, hcols]], axis=1)
            qe_d[key] = qe_ref[b, rows, hcols]
            gc_row_d[key] = gc_row
            g_last_d[key] = gc_col[DN_CHUNK - 1:DN_CHUNK, :]

    for _ in range(6):
        r_d = {key: _bdot(y_d[key][:, :DN_CHUNK], y_d[key]) for key in all_heads}
        y_d = {key: jnp.where(lane_lo, r_d[key], y_d[key] + r_d[key]) for key in all_heads}
    uw_d = {key: _bdot(y_d[key][:, DN_CHUNK:], rhs_d[key]) for key in all_heads}

    s_d = {(b, h): s_ref[b, h] for b in range(n_seq) for h in range(DN_V_HEADS)}
    for c in range(DN_STEP_CHUNKS):
        rows = slice(c * DN_CHUNK, (c + 1) * DN_CHUNK)
        chunk_heads = [(b, h) for b in range(n_seq) for h in range(DN_V_HEADS)]
        ws_qs_d = {}
        for (b, h) in chunk_heads:
            key = (b, c, h)
            lhs = jnp.concatenate([uw_d[key][:, DN_HEAD_DIM:].astype(BF16), qe_d[key]], axis=0)
            ws_qs_d[key] = _bdot(lhs, s_d[(b, h)])
        av_d = {}
        for (b, h) in chunk_heads:
            key, kkey = (b, c, h), (b, c, h // reps)
            v_new = uw_d[key][:, :DN_HEAD_DIM] - ws_qs_d[key][:DN_CHUNK]
            attn = jnp.where(causal, qk_d[kkey] * decay_d[key], 0.0)
            k_state_t = kt_d[kkey] * jnp.exp(g_last_d[key] - gc_row_d[key])
            av_d[key] = _bdot(jnp.concatenate([attn, k_state_t], axis=0), v_new)
        for (b, h) in chunk_heads:
            key = (b, c, h)
            o = ws_qs_d[key][DN_CHUNK:] + av_d[key][:DN_CHUNK]
            o_ref[b, rows, h * DN_HEAD_DIM:(h + 1) * DN_HEAD_DIM] = o.astype(BF16)
            s_d[(b, h)] = s_d[(b, h)] * jnp.exp(g_last_d[key]) + av_d[key][DN_CHUNK:]
    for (b, h), s_new in s_d.items():
        s_ref[b, h] = s_new


def _dn_core(q, k, vb, kbe, qe, bg):
    b, lp, _ = q.shape
    rows = DN_STEP_CHUNKS * DN_CHUNK
    blk = lambda width: pl.BlockSpec((b, rows, width), lambda i: (0, i, 0))
    return pl.pallas_call(
        _dn_core_kernel,
        grid=(lp // rows,),
        in_specs=[blk(DN_KEY_WIDTH), blk(DN_KEY_WIDTH), blk(DN_VALUE_WIDTH), blk(DN_VALUE_WIDTH),
                  blk(DN_VALUE_WIDTH), blk(LANES)],
        out_specs=blk(DN_VALUE_WIDTH),
        out_shape=jax.ShapeDtypeStruct((b, lp, DN_VALUE_WIDTH), BF16),
        scratch_shapes=[pltpu.VMEM((b, DN_V_HEADS, DN_HEAD_DIM, DN_HEAD_DIM), F32)],
        compiler_params=pltpu.CompilerParams(dimension_semantics=("arbitrary",),
                                             vmem_limit_bytes=VMEM_LIMIT),
        name="dn_core",
    )(q, k, vb, kbe, qe, bg)


def _dn_out_kernel(h_ref, o_ref, z_ref, onw_ref, w_ref, out_ref):
    onw = onw_ref[...]
    acc = h_ref[...]
    heads_per_chunk = DN_OUT_K_CHUNK // DN_HEAD_DIM
    for c in range(DN_VALUE_WIDTH // DN_OUT_K_CHUNK):
        parts = []
        for hh in range(c * heads_per_chunk, (c + 1) * heads_per_chunk):
            cols = slice(hh * DN_HEAD_DIM, (hh + 1) * DN_HEAD_DIM)
            y = _rms_rows(o_ref[:, cols].astype(F32), onw) * _silu(z_ref[:, cols].astype(F32))
            parts.append(y.astype(BF16))
        acc = acc + jnp.dot(jnp.concatenate(parts, axis=1),
                            w_ref[c * DN_OUT_K_CHUNK:(c + 1) * DN_OUT_K_CHUNK, :],
                            preferred_element_type=F32)
    out_ref[...] = acc


def _dn_out(h, o, z, onw, w_out, seq):
    b, lp, d = h.shape
    skip = (lp - seq) // ROW_TILE
    in_block = lambda width: pl.BlockSpec((None, ROW_TILE, width), lambda bi, i: (bi, i + skip, 0))
    return pl.pallas_call(
        _dn_out_kernel,
        grid=(b, seq // ROW_TILE),
        in_specs=[in_block(d), in_block(DN_VALUE_WIDTH), in_block(DN_VALUE_WIDTH),
                  pl.BlockSpec((1, DN_HEAD_DIM), lambda bi, i: (0, 0)),
                  pl.BlockSpec(w_out.shape, lambda bi, i: (0, 0))],
        out_specs=pl.BlockSpec((None, ROW_TILE, d), lambda bi, i: (bi, i, 0)),
        out_shape=jax.ShapeDtypeStruct((b, seq, d), F32),
        compiler_params=pltpu.CompilerParams(dimension_semantics=("parallel", "parallel"),
                                             vmem_limit_bytes=VMEM_LIMIT),
        name="dn_out",
    )(h, o, z, onw, w_out)


def kernel(x, meta_tokens, attn_norm_w, attn_w_in, attn_q_norm_w, attn_k_norm_w, attn_sinks,
           attn_w_out, dn_norm_w, dn_w_in, dn_conv_w, dn_a_log, dn_dt_bias, dn_o_norm_w, dn_w_out):
    b, seq, d = x.shape
    assert seq % ROW_TILE == 0 and attn_norm_w.shape[0] == 1 and dn_norm_w.shape[0] == 1
    meta = jnp.broadcast_to(meta_tokens.astype(x.dtype)[None], (b, N_META, d))
    head = jnp.concatenate([jnp.zeros((b, PAD, d), x.dtype), meta], axis=1)

    q, kv, gate = _attn_in(head, x, attn_norm_w[0][None], attn_w_in[0].astype(BF16))
    og = _attention(q, kv, gate, attn_sinks[0], attn_q_norm_w[0][None], attn_k_norm_w[0][None])

    w_in = dn_w_in[0]
    w_qkv = w_in[:, :DN_CONV_WIDTH].astype(BF16)
    w_z = w_in[:, DN_CONV_WIDTH:DN_CONV_WIDTH + DN_VALUE_WIDTH].astype(BF16)
    n_ba = 2 * DN_V_HEADS
    w_ba = jnp.pad(w_in[:, DN_CONV_WIDTH + DN_VALUE_WIDTH:], ((0, 0), (0, LANES - n_ba))).astype(BF16)
    lane_pad = lambda t: jnp.pad(t[None], ((0, 0), (DN_V_HEADS, LANES - n_ba)))
    h1, qd, kd, vbd, kbed, qed, zd, bg = _dn_in(head, x, og, attn_w_out[0].astype(BF16), dn_norm_w[0][None],
                                    w_qkv, w_z, w_ba, dn_conv_w[0],
                                    lane_pad(dn_a_log[0]), lane_pad(dn_dt_bias[0]))
    od = _dn_core(qd, kd, vbd, kbed, qed, bg)
    return _dn_out(h1, od, zd, dn_o_norm_w[0][None], dn_w_out[0].astype(BF16), seq)
```

```python
import numpy as np
import jax
import jax.numpy as jnp
from jax import lax
from jax.experimental import pallas as pl
from jax.experimental.pallas import tpu as pltpu

F32 = jnp.float32
BF16 = jnp.bfloat16

N_META = 16
NORM_EPS = 1e-6

ATTN_HEAD_DIM = 64
ATTN_HEADS = 16
ATTN_KV_HEADS = 2
ATTN_GROUPS = ATTN_HEADS // ATTN_KV_HEADS
ATTN_WIDTH = ATTN_HEADS * ATTN_HEAD_DIM
ATTN_KV_WIDTH = ATTN_KV_HEADS * ATTN_HEAD_DIM
WINDOW = 128
ATTN_BLOCK = 128

DN_HEAD_DIM = 128
DN_K_HEADS = 8
DN_V_HEADS = 16
DN_KEY_WIDTH = DN_K_HEADS * DN_HEAD_DIM
DN_VALUE_WIDTH = DN_V_HEADS * DN_HEAD_DIM
DN_CONV = 4
DN_CHUNK = 64
DN_CONV_WIDTH = 2 * DN_KEY_WIDTH + DN_VALUE_WIDTH

LANES = 128
SUBLANES = 8
ROW_TILE = 256
PAD = ROW_TILE - N_META
META_BLOCK = PAD // ATTN_BLOCK
META_OFFSET = PAD % ATTN_BLOCK
CONV_COLS = 512
CONV_PHASES = 4
DN_OUT_K_CHUNK = 512
DN_STEP_CHUNKS = 2
VMEM_LIMIT = 56 * 1024 * 1024

LOG2E = 1.4426950408889634
NEG_BIG = -1e30
CLIP_INVALID = 1e30


def _alibi_slopes(n_heads):
    return np.exp2(-8.0 * np.arange(1, n_heads + 1) / n_heads).astype(np.float32)


def _bdot(a, b):
    return jnp.dot(a.astype(BF16), b.astype(BF16), preferred_element_type=F32)


def _bdot_nt(a, b):
    return lax.dot_general(a.astype(BF16), b.astype(BF16), (((1,), (1,)), ((), ())),
                           preferred_element_type=F32)


def _rms_rows(x, w):
    return x * lax.rsqrt(jnp.mean(x * x, axis=-1, keepdims=True) + NORM_EPS) * w


def _silu(x):
    return x * jax.nn.sigmoid(x)


def _layer_input(head_ref, x_ref):
    return jnp.where(pl.program_id(1) == 0, head_ref[...], x_ref[...])


def _attn_in_kernel(head_ref, x_ref, nw_ref, w_ref, q_ref, kv_ref, g_ref):
    xn = _rms_rows(_layer_input(head_ref, x_ref), nw_ref[...]).astype(BF16)
    q_ref[...] = jnp.dot(xn, w_ref[:, :ATTN_WIDTH], preferred_element_type=F32).astype(BF16)
    kv_ref[...] = jnp.dot(xn, w_ref[:, ATTN_WIDTH:ATTN_WIDTH + 2 * ATTN_KV_WIDTH],
                          preferred_element_type=F32).astype(BF16)
    g_ref[...] = jnp.dot(xn, w_ref[:, ATTN_WIDTH + 2 * ATTN_KV_WIDTH:],
                         preferred_element_type=F32).astype(BF16)


def _padded_input_specs(d):
    return [pl.BlockSpec((None, ROW_TILE, d), lambda bi, i: (bi, 0, 0)),
            pl.BlockSpec((None, ROW_TILE, d), lambda bi, i: (bi, jnp.maximum(i - 1, 0), 0))]


def _attn_in(head, x, norm_w, w_in):
    b, seq, d = x.shape
    lp = seq + ROW_TILE
    n_in = w_in.shape[1]
    row_block = lambda width: pl.BlockSpec((None, ROW_TILE, width), lambda bi, i: (bi, i, 0))
    return pl.pallas_call(
        _attn_in_kernel,
        grid=(b, lp // ROW_TILE),
        in_specs=_padded_input_specs(d) + [pl.BlockSpec((1, d), lambda bi, i: (0, 0)),
                                           pl.BlockSpec((d, n_in), lambda bi, i: (0, 0))],
        out_specs=[row_block(ATTN_WIDTH), row_block(2 * ATTN_KV_WIDTH), row_block(ATTN_WIDTH)],
        out_shape=[jax.ShapeDtypeStruct((b, lp, ATTN_WIDTH), BF16),
                   jax.ShapeDtypeStruct((b, lp, 2 * ATTN_KV_WIDTH), BF16),
                   jax.ShapeDtypeStruct((b, lp, ATTN_WIDTH), BF16)],
        compiler_params=pltpu.CompilerParams(dimension_semantics=("parallel", "parallel"),
                                             vmem_limit_bytes=VMEM_LIMIT),
        name="attn_in",
    )(head, x, norm_w, w_in)


def _attn_kernel(sink_ref, q_ref, kvm_ref, kvp_ref, kvc_ref, gate_ref, qnw_ref, knw_ref, o_ref):
    nref = pl.program_id(1) - META_BLOCK

    def clip(dist, valid):
        return jnp.where(valid, jnp.minimum(dist, WINDOW).astype(F32), CLIP_INVALID)

    jm = lax.broadcasted_iota(jnp.int32, (N_META, ATTN_BLOCK), 0)
    qm = lax.broadcasted_iota(jnp.int32, (N_META, ATTN_BLOCK), 1)
    dist_m = nref * ATTN_BLOCK + qm - (META_OFFSET + jm)
    c = lax.broadcasted_iota(jnp.int32, (ATTN_BLOCK, ATTN_BLOCK), 0)
    qi = lax.broadcasted_iota(jnp.int32, (ATTN_BLOCK, ATTN_BLOCK), 1)
    clipdist = jnp.concatenate(
        [clip(dist_m, dist_m >= 0),
         clip(ATTN_BLOCK + qi - c, jnp.logical_and(c > qi, nref >= 2)),
         clip(qi - c, jnp.logical_and(c <= qi, nref >= 1))], axis=0)
    slopes = _alibi_slopes(ATTN_HEADS) * np.float32(LOG2E)
    qnw = qnw_ref[...] * (ATTN_HEAD_DIM ** -0.5 * LOG2E)
    knw = knw_ref[...]

    s_l, v_l = [], []
    for h in range(ATTN_KV_HEADS):
        ks = slice(h * ATTN_HEAD_DIM, (h + 1) * ATTN_HEAD_DIM)
        vs = slice(ATTN_KV_WIDTH + h * ATTN_HEAD_DIM, ATTN_KV_WIDTH + (h + 1) * ATTN_HEAD_DIM)
        k_ext = jnp.concatenate([kvm_ref[:, ks], kvp_ref[:, ks], kvc_ref[:, ks]], axis=0)
        k_ext = _rms_rows(k_ext.astype(F32), knw).astype(BF16)
        v_l.append(jnp.concatenate([kvm_ref[:, vs], kvp_ref[:, vs], kvc_ref[:, vs]], axis=0))
        group_cols = slice(h * ATTN_GROUPS * ATTN_HEAD_DIM, (h + 1) * ATTN_GROUPS * ATTN_HEAD_DIM)
        q_t = q_ref[:, group_cols].astype(F32).T
        qn = []
        for g in range(ATTN_GROUPS):
            qg = q_t[g * ATTN_HEAD_DIM:(g + 1) * ATTN_HEAD_DIM]
            inv = lax.rsqrt(jnp.mean(qg * qg, axis=0, keepdims=True) + NORM_EPS)
            qn.append((qg * inv * qnw).astype(BF16))
        s_l.append(jnp.dot(k_ext, jnp.concatenate(qn, axis=1),
                           preferred_element_type=F32))

    for h in range(ATTN_KV_HEADS):
        p_l, rinv_l = [], []
        for g in range(ATTN_GROUPS):
            hq = h * ATTN_GROUPS + g
            s = s_l[h][:, g * ATTN_BLOCK:(g + 1) * ATTN_BLOCK] - float(slopes[hq]) * clipdist
            sink = sink_ref[hq] * LOG2E
            m = jnp.maximum(jnp.max(s, axis=0, keepdims=True), sink)
            p = jnp.exp2(s - m)
            rinv_l.append(1.0 / (jnp.sum(p, axis=0, keepdims=True) + jnp.exp2(sink - m)))
            p_l.append(p.astype(BF16))
        o_t = lax.dot_general(v_l[h], jnp.concatenate(p_l, axis=1), (((0,), (0,)), ((), ())),
                              preferred_element_type=F32)
        o_t = jnp.concatenate(
            [o_t[:, g * ATTN_BLOCK:(g + 1) * ATTN_BLOCK] * rinv_l[g] for g in range(ATTN_GROUPS)],
            axis=0)
        group_cols = slice(h * ATTN_GROUPS * ATTN_HEAD_DIM, (h + 1) * ATTN_GROUPS * ATTN_HEAD_DIM)
        o_ref[:, group_cols] = (o_t.T * _silu(gate_ref[:, group_cols].astype(F32))).astype(BF16)


def _attention(q, kv, gate, sinks, qnw, knw):
    b, lp, _ = q.shape
    nb = lp // ATTN_BLOCK
    kv_block = (None, ATTN_BLOCK, 2 * ATTN_KV_WIDTH)
    wide_block = (None, ATTN_BLOCK, ATTN_WIDTH)
    qnw_lanes = jnp.broadcast_to(qnw.reshape(ATTN_HEAD_DIM, 1), (ATTN_HEAD_DIM, LANES))
    return pl.pallas_call(
        _attn_kernel,
        grid=(b, nb),
        in_specs=[pl.BlockSpec(memory_space=pltpu.SMEM),
                  pl.BlockSpec(wide_block, lambda bi, n: (bi, n, 0)),
                  pl.BlockSpec((None, N_META, 2 * ATTN_KV_WIDTH), lambda bi, n: (bi, PAD // N_META, 0)),
                  pl.BlockSpec(kv_block, lambda bi, n: (bi, jnp.maximum(n - 1, 0), 0)),
                  pl.BlockSpec(kv_block, lambda bi, n: (bi, n, 0)),
                  pl.BlockSpec(wide_block, lambda bi, n: (bi, n, 0)),
                  pl.BlockSpec((ATTN_HEAD_DIM, LANES), lambda bi, n: (0, 0)),
                  pl.BlockSpec((1, ATTN_HEAD_DIM), lambda bi, n: (0, 0))],
        out_specs=pl.BlockSpec(wide_block, lambda bi, n: (bi, n, 0)),
        out_shape=jax.ShapeDtypeStruct((b, lp, ATTN_WIDTH), BF16),
        compiler_params=pltpu.CompilerParams(dimension_semantics=("parallel", "parallel"),
                                             vmem_limit_bytes=VMEM_LIMIT),
        name="attn_core",
    )(sinks, q, kv, kv, kv, gate, qnw_lanes, knw)


def _dn_in_kernel(head_ref, og0_ref, xnext_ref, ognext_ref, wout_ref, nw_ref, wqkv_ref, wz_ref, wba_ref,
                  convw_ref, alog_ref, dtb_ref, tri_ref,
                  h1_ref, q_ref, k_ref, vb_ref, kbe_ref, qe_ref, z_ref, bg_ref,
                  buf_ref, ybuf_ref, halo_ref, h1s_ref, xn_ref):
    i = pl.program_id(1)
    halo = SUBLANES

    @pl.when(i == 0)
    def _():
        halo_ref[...] = jnp.zeros(halo_ref.shape, F32)
        h1s_ref[...] = head_ref[...] + jnp.dot(og0_ref[...], wout_ref[...], preferred_element_type=F32)
        xn_ref[...] = _rms_rows(h1s_ref[...], nw_ref[...]).astype(BF16)

    h1_ref[...] = h1s_ref[...]
    xn = xn_ref[...]

    def z_part(c0):
        def run():
            z_ref[:, c0:c0 + CONV_COLS] = jnp.dot(xn, wz_ref[:, c0:c0 + CONV_COLS],
                                                  preferred_element_type=F32).astype(BF16)
        return run

    def next_h1_part(c0):
        def run():
            h1s_ref[:, c0:c0 + CONV_COLS] = xnext_ref[:, c0:c0 + CONV_COLS] + jnp.dot(
                ognext_ref[...], wout_ref[:, c0:c0 + CONV_COLS], preferred_element_type=F32)
        return run

    fillers = ([z_part(c0) for c0 in range(0, DN_VALUE_WIDTH, CONV_COLS)]
               + [next_h1_part(c0) for c0 in range(0, h1s_ref.shape[1], CONV_COLS)])

    ba = jnp.dot(xn, wba_ref[...], preferred_element_type=F32)
    lane = lax.broadcasted_iota(jnp.int32, ba.shape, 1)
    row = i * ROW_TILE + lax.broadcasted_iota(jnp.int32, ba.shape, 0)
    x = ba + dtb_ref[...]
    softplus = jnp.maximum(x, 0.0) + jnp.log1p(jnp.exp(-jnp.abs(x)))
    g = -jnp.exp(alog_ref[...]) * softplus
    bg = jnp.where(lane < DN_V_HEADS, jax.nn.sigmoid(ba), g)
    bg = jnp.where(row >= PAD, bg, 0.0)
    csum = jnp.dot(tri_ref[...], bg, preferred_element_type=F32, precision=lax.Precision.HIGHEST)
    bg = jnp.where(lane < DN_V_HEADS, bg, csum)
    bg_ref[...] = bg
    egc = jnp.exp(bg)
    reps = DN_V_HEADS // DN_K_HEADS

    q_scale = DN_HEAD_DIM ** -0.5
    rows_per_phase = ROW_TILE // CONV_PHASES
    slabs_per_chunk = CONV_COLS // LANES
    for cj in range(DN_CONV_WIDTH // CONV_COLS):
        u = jnp.dot(xn, wqkv_ref[:, cj * CONV_COLS:(cj + 1) * CONV_COLS],
                    preferred_element_type=F32)
        if cj >= 1 and fillers:
            fillers.pop(0)()
        for sl in range(slabs_per_chunk):
            slab = cj * slabs_per_chunk + sl
            col0 = slab * LANES
            buf_ref[sl, 0:halo, :] = halo_ref[slab]
            buf_ref[sl, halo:halo + ROW_TILE, :] = u[:, sl * LANES:(sl + 1) * LANES]
            w4 = convw_ref[:, col0:col0 + LANES]
            taps = {}
            for start in range(halo - (DN_CONV - 1), halo + CONV_PHASES):
                taps[start] = buf_ref[sl, pl.ds(start, rows_per_phase, stride=CONV_PHASES), :]
            for a in range(CONV_PHASES):
                acc = None
                for j in range(DN_CONV):
                    term = w4[DN_CONV - 1 - j:DN_CONV - j] * taps[halo + a - j]
                    acc = term if acc is None else acc + term
                ybuf_ref[sl, pl.ds(a, rows_per_phase, stride=CONV_PHASES), :] = _silu(acc)
            halo_ref[slab] = buf_ref[sl, ROW_TILE:ROW_TILE + halo, :]
            y = ybuf_ref[sl]
            if col0 < DN_KEY_WIDTH:
                kh = slab
                qn = y * (lax.rsqrt(jnp.sum(y * y, axis=-1, keepdims=True) + NORM_EPS) * q_scale)
                q_ref[:, col0:col0 + LANES] = qn.astype(BF16)
                for h in range(reps * kh, reps * (kh + 1)):
                    gl = DN_V_HEADS + h
                    qe_ref[:, h * LANES:(h + 1) * LANES] = (qn * egc[:, gl:gl + 1]).astype(BF16)
            elif col0 < 2 * DN_KEY_WIDTH:
                kh = slab - DN_K_HEADS
                kn = y * lax.rsqrt(jnp.sum(y * y, axis=-1, keepdims=True) + NORM_EPS)
                k_ref[:, kh * LANES:(kh + 1) * LANES] = kn.astype(BF16)
                for h in range(reps * kh, reps * (kh + 1)):
                    gl = DN_V_HEADS + h
                    kbe_ref[:, h * LANES:(h + 1) * LANES] = (
                        kn * (bg[:, h:h + 1] * egc[:, gl:gl + 1])).astype(BF16)
            else:
                h = slab - 2 * DN_K_HEADS
                vb_ref[:, h * LANES:(h + 1) * LANES] = (y * bg[:, h:h + 1]).astype(BF16)

    for run in fillers:
        run()
    xn_ref[...] = _rms_rows(h1s_ref[...], nw_ref[...]).astype(BF16)


def _dn_in(head, x, og, w_out, norm_w, w_qkv, w_z, w_ba, conv_w, alog_row, dtb_row):
    b, seq, d = x.shape
    lp = seq + ROW_TILE
    nt = lp // ROW_TILE
    n_x = seq // ROW_TILE
    idx = np.arange(ROW_TILE)
    tri = jnp.asarray((idx[:, None] >= idx[None, :])
                      & (idx[:, None] // DN_CHUNK == idx[None, :] // DN_CHUNK), F32)
    row_block = lambda width: pl.BlockSpec((None, ROW_TILE, width), lambda bi, i: (bi, i, 0))
    full = lambda shape: pl.BlockSpec(shape, lambda bi, i: (0,) * len(shape),
                                      pipeline_mode=pl.Buffered(1))
    return pl.pallas_call(
        _dn_in_kernel,
        grid=(b, nt),
        in_specs=[
            pl.BlockSpec((None, ROW_TILE, d), lambda bi, i: (bi, 0, 0)),
            pl.BlockSpec((None, ROW_TILE, og.shape[-1]), lambda bi, i: (bi, 0, 0)),
            pl.BlockSpec((None, ROW_TILE, d), lambda bi, i: (bi, jnp.minimum(i, n_x - 1), 0)),
            pl.BlockSpec((None, ROW_TILE, og.shape[-1]), lambda bi, i: (bi, jnp.minimum(i + 1, nt - 1), 0)),
            full(w_out.shape), full((1, d)), full(w_qkv.shape), full(w_z.shape),
            full(w_ba.shape), full(conv_w.shape), full((1, LANES)), full((1, LANES)), full(tri.shape)],
        out_specs=[row_block(d), row_block(DN_KEY_WIDTH), row_block(DN_KEY_WIDTH),
                   row_block(DN_VALUE_WIDTH), row_block(DN_VALUE_WIDTH), row_block(DN_VALUE_WIDTH),
                   row_block(DN_VALUE_WIDTH), row_block(LANES)],
        out_shape=[jax.ShapeDtypeStruct((b, lp, d), F32),
                   jax.ShapeDtypeStruct((b, lp, DN_KEY_WIDTH), BF16),
                   jax.ShapeDtypeStruct((b, lp, DN_KEY_WIDTH), BF16),
                   jax.ShapeDtypeStruct((b, lp, DN_VALUE_WIDTH), BF16),
                   jax.ShapeDtypeStruct((b, lp, DN_VALUE_WIDTH), BF16),
                   jax.ShapeDtypeStruct((b, lp, DN_VALUE_WIDTH), BF16),
                   jax.ShapeDtypeStruct((b, lp, DN_VALUE_WIDTH), BF16),
                   jax.ShapeDtypeStruct((b, lp, LANES), F32)],
        scratch_shapes=[pltpu.VMEM((CONV_COLS // LANES, ROW_TILE + SUBLANES, LANES), F32),
                        pltpu.VMEM((CONV_COLS // LANES, ROW_TILE, LANES), F32),
                        pltpu.VMEM((DN_CONV_WIDTH // LANES, SUBLANES, LANES), F32),
                        pltpu.VMEM((ROW_TILE, d), F32),
                        pltpu.VMEM((ROW_TILE, d), BF16)],
        compiler_params=pltpu.CompilerParams(dimension_semantics=("parallel", "arbitrary"),
                                             vmem_limit_bytes=VMEM_LIMIT),
        name="dn_in",
    )(head, og, x, og, w_out, norm_w, w_qkv, w_z, w_ba, conv_w, alog_row, dtb_row, tri)


def _dn_core_kernel(q_ref, k_ref, vb_ref, kbe_ref, qe_ref, bg_ref, o_ref, s_ref):
    @pl.when(pl.program_id(0) == 0)
    def _():
        s_ref[...] = jnp.zeros(s_ref.shape, F32)

    n_seq = q_ref.shape[0]
    reps = DN_V_HEADS // DN_K_HEADS
    ri = lax.broadcasted_iota(jnp.int32, (DN_CHUNK, DN_CHUNK), 0)
    ci = lax.broadcasted_iota(jnp.int32, (DN_CHUNK, DN_CHUNK), 1)
    causal = ri >= ci
    neg_strict = -(ri > ci).astype(F32)
    eye = (ri == ci).astype(F32)
    keep_t = (lax.broadcasted_iota(jnp.int32, (DN_CHUNK, 2 * DN_CHUNK), 1) >= DN_CHUNK).astype(F32)

    insts = [(b, c) for c in range(DN_STEP_CHUNKS) for b in range(n_seq)]

    qk_d, kk_d, kt_d = {}, {}, {}
    for (b, c) in insts:
        rows = slice(c * DN_CHUNK, (c + 1) * DN_CHUNK)
        for kh in range(DN_K_HEADS):
            kcols = slice(kh * DN_HEAD_DIM, (kh + 1) * DN_HEAD_DIM)
            k = k_ref[b, rows, kcols]
            qk_kk = _bdot_nt(jnp.concatenate([q_ref[b, rows, kcols], k], axis=0), k)
            key = (b, c, kh)
            qk_d[key], kk_d[key] = qk_kk[:DN_CHUNK], qk_kk[DN_CHUNK:]
            kt_d[key] = k.astype(F32).T

    lhs2_d, y_d, rhs_d, qe_d, g_last_d = {}, {}, {}, {}, {}
    for (b, c) in insts:
        rows = slice(c * DN_CHUNK, (c + 1) * DN_CHUNK)
        bg = bg_ref[b, rows, :]
        bg_t = bg.T
        for h in range(DN_V_HEADS):
            key, kkey = (b, c, h), (b, c, h // reps)
            gl = DN_V_HEADS + h
            beta_col = bg[:, h:h + 1]
            gc_col = bg[:, gl:gl + 1]
            gc_row = bg_t[gl:gl + 1, :]
            decay = jnp.exp(jnp.where(causal, gc_col - gc_row, NEG_BIG))
            g_last = gc_col[DN_CHUNK - 1:DN_CHUNK, :]
            lhs2_d[key] = jnp.concatenate(
                [(qk_d[kkey] * decay).astype(BF16),
                 (kt_d[kkey] * jnp.exp(g_last - gc_row)).astype(BF16)], axis=0)
            n1 = kk_d[kkey] * decay * (neg_strict * beta_col)
            y_d[key] = jnp.concatenate([n1, eye], axis=1)
            hcols = slice(h * DN_HEAD_DIM, (h + 1) * DN_HEAD_DIM)
            rhs_d[key] = jnp.concatenate([vb_ref[b, rows, hcols], kbe_ref[b, rows, hcols]], axis=1)
            qe_d[key] = qe_ref[b, rows, hcols]
            g_last_d[key] = g_last

    chunk_heads = [(b, h) for b in range(n_seq) for h in range(DN_V_HEADS)]
    u_d, w_d, ws_qs_d = {}, {}, {}
    s_d = {(b, h): s_ref[b, h] for (b, h) in chunk_heads}

    def inverse_round(c):
        for (b, h) in chunk_heads:
            key = (b, c, h)
            r = _bdot(y_d[key][:, :DN_CHUNK], y_d[key])
            y_d[key] = r + y_d[key] * keep_t

    def solve(c):
        for (b, h) in chunk_heads:
            key = (b, c, h)
            uw = _bdot(y_d[key][:, DN_CHUNK:], rhs_d[key])
            u_d[key], w_d[key] = uw[:, :DN_HEAD_DIM], uw[:, DN_HEAD_DIM:].astype(BF16)

    def state_read(c):
        for (b, h) in chunk_heads:
            key = (b, c, h)
            ws_qs_d[key] = _bdot(jnp.concatenate([w_d[key], qe_d[key]], axis=0), s_d[(b, h)])

    def state_update(c):
        rows = slice(c * DN_CHUNK, (c + 1) * DN_CHUNK)
        for (b, h) in chunk_heads:
            key = (b, c, h)
            v_new = u_d[key] - ws_qs_d[key][:DN_CHUNK]
            av = _bdot(lhs2_d[key], v_new)
            o = ws_qs_d[key][DN_CHUNK:] + av[:DN_CHUNK]
            o_ref[b, rows, h * DN_HEAD_DIM:(h + 1) * DN_HEAD_DIM] = o.astype(BF16)
            s_d[(b, h)] = s_d[(b, h)] * jnp.exp(g_last_d[key]) + av[DN_CHUNK:]

    for c in range(DN_STEP_CHUNKS):
        between = [lambda: state_read(c - 1), lambda: state_update(c - 1)] if c > 0 else []
        for rnd in range(6):
            inverse_round(c)
            if rnd in (0, 2) and between:
                between.pop(0)()
        solve(c)
    state_read(DN_STEP_CHUNKS - 1)
    state_update(DN_STEP_CHUNKS - 1)
    for (b, h), s_new in s_d.items():
        s_ref[b, h] = s_new


def _dn_core(q, k, vb, kbe, qe, bg):
    b, lp, _ = q.shape
    rows = DN_STEP_CHUNKS * DN_CHUNK
    blk = lambda width: pl.BlockSpec((b, rows, width), lambda i: (0, i, 0))
    return pl.pallas_call(
        _dn_core_kernel,
        grid=(lp // rows,),
        in_specs=[blk(DN_KEY_WIDTH), blk(DN_KEY_WIDTH), blk(DN_VALUE_WIDTH), blk(DN_VALUE_WIDTH),
                  blk(DN_VALUE_WIDTH), blk(LANES)],
        out_specs=blk(DN_VALUE_WIDTH),
        out_shape=jax.ShapeDtypeStruct((b, lp, DN_VALUE_WIDTH), BF16),
        scratch_shapes=[pltpu.VMEM((b, DN_V_HEADS, DN_HEAD_DIM, DN_HEAD_DIM), F32)],
        compiler_params=pltpu.CompilerParams(dimension_semantics=("arbitrary",),
                                             vmem_limit_bytes=VMEM_LIMIT),
        name="dn_core",
    )(q, k, vb, kbe, qe, bg)


def _dn_out_kernel(h_ref, o_ref, z_ref, onw_ref, w_ref, out_ref):
    onw = onw_ref[...]
    acc = h_ref[...]
    heads_per_chunk = DN_OUT_K_CHUNK // DN_HEAD_DIM
    for c in range(DN_VALUE_WIDTH // DN_OUT_K_CHUNK):
        parts = []
        for hh in range(c * heads_per_chunk, (c + 1) * heads_per_chunk):
            cols = slice(hh * DN_HEAD_DIM, (hh + 1) * DN_HEAD_DIM)
            y = _rms_rows(o_ref[:, cols].astype(F32), onw) * _silu(z_ref[:, cols].astype(F32))
            parts.append(y.astype(BF16))
        acc = acc + jnp.dot(jnp.concatenate(parts, axis=1),
                            w_ref[c * DN_OUT_K_CHUNK:(c + 1) * DN_OUT_K_CHUNK, :],
                            preferred_element_type=F32)
    out_ref[...] = acc


def _dn_out(h, o, z, onw, w_out, seq):
    b, lp, d = h.shape
    skip = (lp - seq) // ROW_TILE
    in_block = lambda width: pl.BlockSpec((None, ROW_TILE, width), lambda bi, i: (bi, i + skip, 0))
    return pl.pallas_call(
        _dn_out_kernel,
        grid=(b, seq // ROW_TILE),
        in_specs=[in_block(d), in_block(DN_VALUE_WIDTH), in_block(DN_VALUE_WIDTH),
                  pl.BlockSpec((1, DN_HEAD_DIM), lambda bi, i: (0, 0)),
                  pl.BlockSpec(w_out.shape, lambda bi, i: (0, 0))],
        out_specs=pl.BlockSpec((None, ROW_TILE, d), lambda bi, i: (bi, i, 0)),
        out_shape=jax.ShapeDtypeStruct((b, seq, d), F32),
        compiler_params=pltpu.CompilerParams(dimension_semantics=("parallel", "parallel"),
                                             vmem_limit_bytes=VMEM_LIMIT),
        name="dn_out",
    )(h, o, z, onw, w_out)


def kernel(x, meta_tokens, attn_norm_w, attn_w_in, attn_q_norm_w, attn_k_norm_w, attn_sinks,
           attn_w_out, dn_norm_w, dn_w_in, dn_conv_w, dn_a_log, dn_dt_bias, dn_o_norm_w, dn_w_out):
    b, seq, d = x.shape
    assert seq % ROW_TILE == 0 and attn_norm_w.shape[0] == 1 and dn_norm_w.shape[0] == 1
    meta = jnp.broadcast_to(meta_tokens.astype(x.dtype)[None], (b, N_META, d))
    head = jnp.concatenate([jnp.zeros((b, PAD, d), x.dtype), meta], axis=1)

    q, kv, gate = _attn_in(head, x, attn_norm_w[0][None], attn_w_in[0].astype(BF16))
    og = _attention(q, kv, gate, attn_sinks[0], attn_q_norm_w[0][None], attn_k_norm_w[0][None])

    w_in = dn_w_in[0]
    w_qkv = w_in[:, :DN_CONV_WIDTH].astype(BF16)
    w_z = w_in[:, DN_CONV_WIDTH:DN_CONV_WIDTH + DN_VALUE_WIDTH].astype(BF16)
    n_ba = 2 * DN_V_HEADS
    w_ba = jnp.pad(w_in[:, DN_CONV_WIDTH + DN_VALUE_WIDTH:], ((0, 0), (0, LANES - n_ba))).astype(BF16)
    lane_pad = lambda t: jnp.pad(t[None], ((0, 0), (DN_V_HEADS, LANES - n_ba)))
    h1, qd, kd, vbd, kbed, qed, zd, bg = _dn_in(head, x, og, attn_w_out[0].astype(BF16), dn_norm_w[0][None],
                                    w_qkv, w_z, w_ba, dn_conv_w[0],
                                    lane_pad(dn_a_log[0]), lane_pad(dn_dt_bias[0]))
    od = _dn_core(qd, kd, vbd, kbed, qed, bg)
    return _dn_out(h1, od, zd, dn_o_norm_w[0][None], dn_w_out[0].astype(BF16), seq)
```

```python
import numpy as np
import jax
import jax.numpy as jnp
from jax import lax
from jax.experimental import pallas as pl
from jax.experimental.pallas import tpu as pltpu

F32 = jnp.float32
BF16 = jnp.bfloat16

N_META = 16
NORM_EPS = 1e-6

ATTN_HEAD_DIM = 64
ATTN_HEADS = 16
ATTN_KV_HEADS = 2
ATTN_GROUPS = ATTN_HEADS // ATTN_KV_HEADS
ATTN_WIDTH = ATTN_HEADS * ATTN_HEAD_DIM
ATTN_KV_WIDTH = ATTN_KV_HEADS * ATTN_HEAD_DIM
WINDOW = 128
ATTN_BLOCK = 128
ATTN_STEP_BLOCKS = 2

DN_HEAD_DIM = 128
DN_K_HEADS = 8
DN_V_HEADS = 16
DN_KEY_WIDTH = DN_K_HEADS * DN_HEAD_DIM
DN_VALUE_WIDTH = DN_V_HEADS * DN_HEAD_DIM
DN_CONV = 4
DN_CHUNK = 64
DN_CONV_WIDTH = 2 * DN_KEY_WIDTH + DN_VALUE_WIDTH

LANES = 128
SUBLANES = 8
ROW_TILE = 256
PAD = ROW_TILE - N_META
META_BLOCK = PAD // ATTN_BLOCK
META_OFFSET = PAD % ATTN_BLOCK
CONV_COLS = 512
FILL_COLS = 1024
CONV_PHASES = 4
DN_OUT_K_CHUNK = 512
DN_STEP_CHUNKS = 2
VMEM_LIMIT = 56 * 1024 * 1024

LOG2E = 1.4426950408889634
NEG_BIG = -1e30
CLIP_INVALID = 1e30


def _alibi_slopes(n_heads):
    return np.exp2(-8.0 * np.arange(1, n_heads + 1) / n_heads).astype(np.float32)


def _bdot(a, b):
    return jnp.dot(a.astype(BF16), b.astype(BF16), preferred_element_type=F32)


def _bdot_nt(a, b):
    return lax.dot_general(a.astype(BF16), b.astype(BF16), (((1,), (1,)), ((), ())),
                           preferred_element_type=F32)


def _rms_rows(x, w):
    return x * lax.rsqrt(jnp.mean(x * x, axis=-1, keepdims=True) + NORM_EPS) * w


def _silu(x):
    return x * jax.nn.sigmoid(x)


def _layer_input(head_ref, x_ref):
    return jnp.where(pl.program_id(1) == 0, head_ref[...], x_ref[...])


def _attn_in_kernel(head_ref, x_ref, nw_ref, w_ref, q_ref, kv_ref, g_ref):
    xn = _rms_rows(_layer_input(head_ref, x_ref), nw_ref[...]).astype(BF16)
    q_ref[...] = jnp.dot(xn, w_ref[:, :ATTN_WIDTH], preferred_element_type=F32).astype(BF16)
    kv_ref[...] = jnp.dot(xn, w_ref[:, ATTN_WIDTH:ATTN_WIDTH + 2 * ATTN_KV_WIDTH],
                          preferred_element_type=F32).astype(BF16)
    g_ref[...] = jnp.dot(xn, w_ref[:, ATTN_WIDTH + 2 * ATTN_KV_WIDTH:],
                         preferred_element_type=F32).astype(BF16)


def _padded_input_specs(d):
    return [pl.BlockSpec((None, ROW_TILE, d), lambda bi, i: (bi, 0, 0)),
            pl.BlockSpec((None, ROW_TILE, d), lambda bi, i: (bi, jnp.maximum(i - 1, 0), 0))]


def _attn_in(head, x, norm_w, w_in):
    b, seq, d = x.shape
    lp = seq + ROW_TILE
    n_in = w_in.shape[1]
    row_block = lambda width: pl.BlockSpec((None, ROW_TILE, width), lambda bi, i: (bi, i, 0))
    return pl.pallas_call(
        _attn_in_kernel,
        grid=(b, lp // ROW_TILE),
        in_specs=_padded_input_specs(d) + [pl.BlockSpec((1, d), lambda bi, i: (0, 0)),
                                           pl.BlockSpec((d, n_in), lambda bi, i: (0, 0))],
        out_specs=[row_block(ATTN_WIDTH), row_block(2 * ATTN_KV_WIDTH), row_block(ATTN_WIDTH)],
        out_shape=[jax.ShapeDtypeStruct((b, lp, ATTN_WIDTH), BF16),
                   jax.ShapeDtypeStruct((b, lp, 2 * ATTN_KV_WIDTH), BF16),
                   jax.ShapeDtypeStruct((b, lp, ATTN_WIDTH), BF16)],
        compiler_params=pltpu.CompilerParams(dimension_semantics=("parallel", "parallel"),
                                             vmem_limit_bytes=VMEM_LIMIT),
        name="attn_in",
    )(head, x, norm_w, w_in)


def _attn_kernel(sink_ref, q_ref, kvm_ref, kvp_ref, kvc_ref, gate_ref, qnw_ref, knw_ref, o_ref):
    def clip(dist, valid):
        return jnp.where(valid, jnp.minimum(dist, WINDOW).astype(F32), CLIP_INVALID)

    jm = lax.broadcasted_iota(jnp.int32, (N_META, ATTN_BLOCK), 0)
    qm = lax.broadcasted_iota(jnp.int32, (N_META, ATTN_BLOCK), 1)
    c = lax.broadcasted_iota(jnp.int32, (ATTN_BLOCK, ATTN_BLOCK), 0)
    qi = lax.broadcasted_iota(jnp.int32, (ATTN_BLOCK, ATTN_BLOCK), 1)
    slopes = _alibi_slopes(ATTN_HEADS) * np.float32(LOG2E)
    qnw = qnw_ref[...] * (ATTN_HEAD_DIM ** -0.5 * LOG2E)
    knw = knw_ref[...]

    def kv_rows(sub, cols):
        cur = kvc_ref[sub * ATTN_BLOCK:(sub + 1) * ATTN_BLOCK, cols]
        prev = kvp_ref[:, cols] if sub == 0 else kvc_ref[(sub - 1) * ATTN_BLOCK:sub * ATTN_BLOCK, cols]
        return jnp.concatenate([kvm_ref[:, cols], prev, cur], axis=0)

    units = [(sub, h) for sub in range(ATTN_STEP_BLOCKS) for h in range(ATTN_KV_HEADS)]
    s_d, v_d, clip_d = {}, {}, {}
    for sub in range(ATTN_STEP_BLOCKS):
        nref = pl.program_id(1) * ATTN_STEP_BLOCKS + sub - META_BLOCK
        dist_m = nref * ATTN_BLOCK + qm - (META_OFFSET + jm)
        clip_d[sub] = jnp.concatenate(
            [clip(dist_m, dist_m >= 0),
             clip(ATTN_BLOCK + qi - c, jnp.logical_and(c > qi, nref >= 2)),
             clip(qi - c, jnp.logical_and(c <= qi, nref >= 1))], axis=0)
    for (sub, h) in units:
        rows = slice(sub * ATTN_BLOCK, (sub + 1) * ATTN_BLOCK)
        k_ext = kv_rows(sub, slice(h * ATTN_HEAD_DIM, (h + 1) * ATTN_HEAD_DIM))
        k_ext = _rms_rows(k_ext.astype(F32), knw).astype(BF16)
        v_d[(sub, h)] = kv_rows(sub, slice(ATTN_KV_WIDTH + h * ATTN_HEAD_DIM,
                                           ATTN_KV_WIDTH + (h + 1) * ATTN_HEAD_DIM))
        group_cols = slice(h * ATTN_GROUPS * ATTN_HEAD_DIM, (h + 1) * ATTN_GROUPS * ATTN_HEAD_DIM)
        q_t = q_ref[rows, group_cols].astype(F32).T
        qn = []
        for g in range(ATTN_GROUPS):
            qg = q_t[g * ATTN_HEAD_DIM:(g + 1) * ATTN_HEAD_DIM]
            inv = lax.rsqrt(jnp.mean(qg * qg, axis=0, keepdims=True) + NORM_EPS)
            qn.append((qg * inv * qnw).astype(BF16))
        s_d[(sub, h)] = jnp.dot(k_ext, jnp.concatenate(qn, axis=1),
                                preferred_element_type=F32)

    for (sub, h) in units:
        rows = slice(sub * ATTN_BLOCK, (sub + 1) * ATTN_BLOCK)
        p_l, rinv_l = [], []
        for g in range(ATTN_GROUPS):
            hq = h * ATTN_GROUPS + g
            s = s_d[(sub, h)][:, g * ATTN_BLOCK:(g + 1) * ATTN_BLOCK] - float(slopes[hq]) * clip_d[sub]
            sink = sink_ref[hq] * LOG2E
            m = jnp.maximum(jnp.max(s, axis=0, keepdims=True), sink)
            p = jnp.exp2(s - m)
            rinv_l.append(1.0 / (jnp.sum(p, axis=0, keepdims=True) + jnp.exp2(sink - m)))
            p_l.append(p.astype(BF16))
        o_t = lax.dot_general(v_d[(sub, h)], jnp.concatenate(p_l, axis=1), (((0,), (0,)), ((), ())),
                              preferred_element_type=F32)
        o_t = jnp.concatenate(
            [o_t[:, g * ATTN_BLOCK:(g + 1) * ATTN_BLOCK] * rinv_l[g] for g in range(ATTN_GROUPS)],
            axis=0)
        group_cols = slice(h * ATTN_GROUPS * ATTN_HEAD_DIM, (h + 1) * ATTN_GROUPS * ATTN_HEAD_DIM)
        o_ref[rows, group_cols] = (o_t.T * _silu(gate_ref[rows, group_cols].astype(F32))).astype(BF16)


def _attention(q, kv, gate, sinks, qnw, knw):
    b, lp, _ = q.shape
    step_rows = ATTN_STEP_BLOCKS * ATTN_BLOCK
    kv_width = 2 * ATTN_KV_WIDTH
    wide_block = (None, step_rows, ATTN_WIDTH)
    qnw_lanes = jnp.broadcast_to(qnw.reshape(ATTN_HEAD_DIM, 1), (ATTN_HEAD_DIM, LANES))
    return pl.pallas_call(
        _attn_kernel,
        grid=(b, lp // step_rows),
        in_specs=[pl.BlockSpec(memory_space=pltpu.SMEM),
                  pl.BlockSpec(wide_block, lambda bi, n: (bi, n, 0)),
                  pl.BlockSpec((None, N_META, kv_width), lambda bi, n: (bi, PAD // N_META, 0)),
                  pl.BlockSpec((None, ATTN_BLOCK, kv_width),
                               lambda bi, n: (bi, jnp.maximum(n * ATTN_STEP_BLOCKS - 1, 0), 0)),
                  pl.BlockSpec((None, step_rows, kv_width), lambda bi, n: (bi, n, 0)),
                  pl.BlockSpec(wide_block, lambda bi, n: (bi, n, 0)),
                  pl.BlockSpec((ATTN_HEAD_DIM, LANES), lambda bi, n: (0, 0)),
                  pl.BlockSpec((1, ATTN_HEAD_DIM), lambda bi, n: (0, 0))],
        out_specs=pl.BlockSpec(wide_block, lambda bi, n: (bi, n, 0)),
        out_shape=jax.ShapeDtypeStruct((b, lp, ATTN_WIDTH), BF16),
        compiler_params=pltpu.CompilerParams(dimension_semantics=("parallel", "parallel"),
                                             vmem_limit_bytes=VMEM_LIMIT),
        name="attn_core",
    )(sinks, q, kv, kv, kv, gate, qnw_lanes, knw)


def _dn_in_kernel(head_ref, og0_ref, xnext_ref, ognext_ref, wout_ref, nw_ref, wqkv_ref, wz_ref, wba_ref,
                  convw_ref, alog_ref, dtb_ref, tri_ref,
                  h1_ref, q_ref, k_ref, vb_ref, kbe_ref, qe_ref, z_ref, bg_ref,
                  buf_ref, ybuf_ref, halo_ref, h1s_ref, xn_ref):
    i = pl.program_id(1)
    halo = SUBLANES

    @pl.when(i == 0)
    def _():
        halo_ref[...] = jnp.zeros(halo_ref.shape, F32)
        h1s_ref[...] = head_ref[...] + jnp.dot(og0_ref[...], wout_ref[...], preferred_element_type=F32)
        xn_ref[...] = _rms_rows(h1s_ref[...], nw_ref[...]).astype(BF16)

    h1_ref[...] = h1s_ref[...]
    xn = xn_ref[...]

    def z_part(c0):
        def run():
            z_ref[:, c0:c0 + FILL_COLS] = jnp.dot(xn, wz_ref[:, c0:c0 + FILL_COLS],
                                                  preferred_element_type=F32).astype(BF16)
        return run

    def next_h1_part(c0):
        def run():
            h1s_ref[:, c0:c0 + FILL_COLS] = xnext_ref[:, c0:c0 + FILL_COLS] + jnp.dot(
                ognext_ref[...], wout_ref[:, c0:c0 + FILL_COLS], preferred_element_type=F32)
        return run

    fillers = ([z_part(c0) for c0 in range(0, DN_VALUE_WIDTH, FILL_COLS)]
               + [next_h1_part(c0) for c0 in range(0, h1s_ref.shape[1], FILL_COLS)])

    ba = jnp.dot(xn, wba_ref[...], preferred_element_type=F32)
    lane = lax.broadcasted_iota(jnp.int32, ba.shape, 1)
    row = i * ROW_TILE + lax.broadcasted_iota(jnp.int32, ba.shape, 0)
    x = ba + dtb_ref[...]
    softplus = jnp.maximum(x, 0.0) + jnp.log1p(jnp.exp(-jnp.abs(x)))
    g = -jnp.exp(alog_ref[...]) * softplus
    bg = jnp.where(lane < DN_V_HEADS, jax.nn.sigmoid(ba), g)
    bg = jnp.where(row >= PAD, bg, 0.0)
    csum = jnp.dot(tri_ref[...], bg, preferred_element_type=F32, precision=lax.Precision.HIGHEST)
    bg = jnp.where(lane < DN_V_HEADS, bg, csum)
    bg_ref[...] = bg
    egc = jnp.exp(bg)
    reps = DN_V_HEADS // DN_K_HEADS

    q_scale = DN_HEAD_DIM ** -0.5
    rows_per_phase = ROW_TILE // CONV_PHASES
    slabs_per_chunk = CONV_COLS // LANES
    for cj in range(DN_CONV_WIDTH // CONV_COLS):
        u = jnp.dot(xn, wqkv_ref[:, cj * CONV_COLS:(cj + 1) * CONV_COLS],
                    preferred_element_type=F32)
        if cj >= 1 and fillers:
            fillers.pop(0)()
        for sl in range(slabs_per_chunk):
            slab = cj * slabs_per_chunk + sl
            col0 = slab * LANES
            buf_ref[sl, 0:halo, :] = halo_ref[slab]
            buf_ref[sl, halo:halo + ROW_TILE, :] = u[:, sl * LANES:(sl + 1) * LANES]
            w4 = convw_ref[:, col0:col0 + LANES]
            taps = {}
            for start in range(halo - (DN_CONV - 1), halo + CONV_PHASES):
                taps[start] = buf_ref[sl, pl.ds(start, rows_per_phase, stride=CONV_PHASES), :]
            for a in range(CONV_PHASES):
                acc = None
                for j in range(DN_CONV):
                    term = w4[DN_CONV - 1 - j:DN_CONV - j] * taps[halo + a - j]
                    acc = term if acc is None else acc + term
                ybuf_ref[sl, pl.ds(a, rows_per_phase, stride=CONV_PHASES), :] = _silu(acc)
            halo_ref[slab] = buf_ref[sl, ROW_TILE:ROW_TILE + halo, :]
            y = ybuf_ref[sl]
            if col0 < DN_KEY_WIDTH:
                kh = slab
                qn = y * (lax.rsqrt(jnp.sum(y * y, axis=-1, keepdims=True) + NORM_EPS) * q_scale)
                q_ref[:, col0:col0 + LANES] = qn.astype(BF16)
                for h in range(reps * kh, reps * (kh + 1)):
                    gl = DN_V_HEADS + h
                    qe_ref[:, h * LANES:(h + 1) * LANES] = (qn * egc[:, gl:gl + 1]).astype(BF16)
            elif col0 < 2 * DN_KEY_WIDTH:
                kh = slab - DN_K_HEADS
                kn = y * lax.rsqrt(jnp.sum(y * y, axis=-1, keepdims=True) + NORM_EPS)
                k_ref[:, kh * LANES:(kh + 1) * LANES] = kn.astype(BF16)
                for h in range(reps * kh, reps * (kh + 1)):
                    gl = DN_V_HEADS + h
                    kbe_ref[:, h * LANES:(h + 1) * LANES] = (
                        kn * (bg[:, h:h + 1] * egc[:, gl:gl + 1])).astype(BF16)
            else:
                h = slab - 2 * DN_K_HEADS
                vb_ref[:, h * LANES:(h + 1) * LANES] = (y * bg[:, h:h + 1]).astype(BF16)

    for run in fillers:
        run()
    xn_ref[...] = _rms_rows(h1s_ref[...], nw_ref[...]).astype(BF16)


def _dn_in(head, x, og, w_out, norm_w, w_qkv, w_z, w_ba, conv_w, alog_row, dtb_row):
    b, seq, d = x.shape
    lp = seq + ROW_TILE
    nt = lp // ROW_TILE
    n_x = seq // ROW_TILE
    idx = np.arange(ROW_TILE)
    tri = jnp.asarray((idx[:, None] >= idx[None, :])
                      & (idx[:, None] // DN_CHUNK == idx[None, :] // DN_CHUNK), F32)
    row_block = lambda width: pl.BlockSpec((None, ROW_TILE, width), lambda bi, i: (bi, i, 0))
    full = lambda shape: pl.BlockSpec(shape, lambda bi, i: (0,) * len(shape),
                                      pipeline_mode=pl.Buffered(1))
    return pl.pallas_call(
        _dn_in_kernel,
        grid=(b, nt),
        in_specs=[
            pl.BlockSpec((None, ROW_TILE, d), lambda bi, i: (bi, 0, 0)),
            pl.BlockSpec((None, ROW_TILE, og.shape[-1]), lambda bi, i: (bi, 0, 0)),
            pl.BlockSpec((None, ROW_TILE, d), lambda bi, i: (bi, jnp.minimum(i, n_x - 1), 0)),
            pl.BlockSpec((None, ROW_TILE, og.shape[-1]), lambda bi, i: (bi, jnp.minimum(i + 1, nt - 1), 0)),
            full(w_out.shape), full((1, d)), full(w_qkv.shape), full(w_z.shape),
            full(w_ba.shape), full(conv_w.shape), full((1, LANES)), full((1, LANES)), full(tri.shape)],
        out_specs=[row_block(d), row_block(DN_KEY_WIDTH), row_block(DN_KEY_WIDTH),
                   row_block(DN_VALUE_WIDTH), row_block(DN_VALUE_WIDTH), row_block(DN_VALUE_WIDTH),
                   row_block(DN_VALUE_WIDTH), row_block(LANES)],
        out_shape=[jax.ShapeDtypeStruct((b, lp, d), F32),
                   jax.ShapeDtypeStruct((b, lp, DN_KEY_WIDTH), BF16),
                   jax.ShapeDtypeStruct((b, lp, DN_KEY_WIDTH), BF16),
                   jax.ShapeDtypeStruct((b, lp, DN_VALUE_WIDTH), BF16),
                   jax.ShapeDtypeStruct((b, lp, DN_VALUE_WIDTH), BF16),
                   jax.ShapeDtypeStruct((b, lp, DN_VALUE_WIDTH), BF16),
                   jax.ShapeDtypeStruct((b, lp, DN_VALUE_WIDTH), BF16),
                   jax.ShapeDtypeStruct((b, lp, LANES), F32)],
        scratch_shapes=[pltpu.VMEM((CONV_COLS // LANES, ROW_TILE + SUBLANES, LANES), F32),
                        pltpu.VMEM((CONV_COLS // LANES, ROW_TILE, LANES), F32),
                        pltpu.VMEM((DN_CONV_WIDTH // LANES, SUBLANES, LANES), F32),
                        pltpu.VMEM((ROW_TILE, d), F32),
                        pltpu.VMEM((ROW_TILE, d), BF16)],
        compiler_params=pltpu.CompilerParams(dimension_semantics=("parallel", "arbitrary"),
                                             vmem_limit_bytes=VMEM_LIMIT),
        name="dn_in",
    )(head, og, x, og, w_out, norm_w, w_qkv, w_z, w_ba, conv_w, alog_row, dtb_row, tri)


def _dn_core_kernel(q_ref, k_ref, vb_ref, kbe_ref, qe_ref, bg_ref, o_ref, s_ref):
    @pl.when(pl.program_id(0) == 0)
    def _():
        s_ref[...] = jnp.zeros(s_ref.shape, F32)

    n_seq = q_ref.shape[0]
    reps = DN_V_HEADS // DN_K_HEADS
    ri = lax.broadcasted_iota(jnp.int32, (DN_CHUNK, DN_CHUNK), 0)
    ci = lax.broadcasted_iota(jnp.int32, (DN_CHUNK, DN_CHUNK), 1)
    causal = ri >= ci
    neg_strict = -(ri > ci).astype(F32)
    eye = (ri == ci).astype(F32)
    keep_t = (lax.broadcasted_iota(jnp.int32, (DN_CHUNK, 2 * DN_CHUNK), 1) >= DN_CHUNK).astype(F32)

    insts = [(b, c) for c in range(DN_STEP_CHUNKS) for b in range(n_seq)]

    qk_d, kk_d, kt_d = {}, {}, {}
    for (b, c) in insts:
        rows = slice(c * DN_CHUNK, (c + 1) * DN_CHUNK)
        for kh in range(DN_K_HEADS):
            kcols = slice(kh * DN_HEAD_DIM, (kh + 1) * DN_HEAD_DIM)
            k = k_ref[b, rows, kcols]
            qk_kk = _bdot_nt(jnp.concatenate([q_ref[b, rows, kcols], k], axis=0), k)
            key = (b, c, kh)
            qk_d[key], kk_d[key] = qk_kk[:DN_CHUNK], qk_kk[DN_CHUNK:]
            kt_d[key] = k.astype(F32).T

    lhs2_d, y_d, rhs_d, qe_d, g_last_d = {}, {}, {}, {}, {}
    for (b, c) in insts:
        rows = slice(c * DN_CHUNK, (c + 1) * DN_CHUNK)
        bg = bg_ref[b, rows, :]
        bg_t = bg.T
        for h in range(DN_V_HEADS):
            key, kkey = (b, c, h), (b, c, h // reps)
            gl = DN_V_HEADS + h
            beta_col = bg[:, h:h + 1]
            gc_col = bg[:, gl:gl + 1]
            gc_row = bg_t[gl:gl + 1, :]
            decay = jnp.exp(jnp.where(causal, gc_col - gc_row, NEG_BIG))
            g_last = gc_col[DN_CHUNK - 1:DN_CHUNK, :]
            lhs2_d[key] = jnp.concatenate(
                [(qk_d[kkey] * decay).astype(BF16),
                 (kt_d[kkey] * jnp.exp(g_last - gc_row)).astype(BF16)], axis=0)
            n1 = kk_d[kkey] * decay * (neg_strict * beta_col)
            y_d[key] = jnp.concatenate([n1, eye], axis=1)
            hcols = slice(h * DN_HEAD_DIM, (h + 1) * DN_HEAD_DIM)
            rhs_d[key] = jnp.concatenate([vb_ref[b, rows, hcols], kbe_ref[b, rows, hcols]], axis=1)
            qe_d[key] = qe_ref[b, rows, hcols]
            g_last_d[key] = g_last

    chunk_heads = [(b, h) for b in range(n_seq) for h in range(DN_V_HEADS)]
    u_d, w_d, ws_qs_d = {}, {}, {}
    s_d = {(b, h): s_ref[b, h] for (b, h) in chunk_heads}

    def inverse_round(c):
        for (b, h) in chunk_heads:
            key = (b, c, h)
            r = _bdot(y_d[key][:, :DN_CHUNK], y_d[key])
            y_d[key] = r + y_d[key] * keep_t

    def solve(c):
        for (b, h) in chunk_heads:
            key = (b, c, h)
            uw = _bdot(y_d[key][:, DN_CHUNK:], rhs_d[key])
            u_d[key], w_d[key] = uw[:, :DN_HEAD_DIM], uw[:, DN_HEAD_DIM:].astype(BF16)

    def state_read(c):
        for (b, h) in chunk_heads:
            key = (b, c, h)
            ws_qs_d[key] = _bdot(jnp.concatenate([w_d[key], qe_d[key]], axis=0), s_d[(b, h)])

    def state_update(c):
        rows = slice(c * DN_CHUNK, (c + 1) * DN_CHUNK)
        for (b, h) in chunk_heads:
            key = (b, c, h)
            v_new = u_d[key] - ws_qs_d[key][:DN_CHUNK]
            av = _bdot(lhs2_d[key], v_new)
            o = ws_qs_d[key][DN_CHUNK:] + av[:DN_CHUNK]
            o_ref[b, rows, h * DN_HEAD_DIM:(h + 1) * DN_HEAD_DIM] = o.astype(BF16)
            s_d[(b, h)] = s_d[(b, h)] * jnp.exp(g_last_d[key]) + av[DN_CHUNK:]

    for c in range(DN_STEP_CHUNKS):
        between = [lambda: state_read(c - 1), lambda: state_update(c - 1)] if c > 0 else []
        for rnd in range(6):
            inverse_round(c)
            if rnd in (0, 2) and between:
                between.pop(0)()
        solve(c)
    state_read(DN_STEP_CHUNKS - 1)
    state_update(DN_STEP_CHUNKS - 1)
    for (b, h), s_new in s_d.items():
        s_ref[b, h] = s_new


def _dn_core(q, k, vb, kbe, qe, bg):
    b, lp, _ = q.shape
    rows = DN_STEP_CHUNKS * DN_CHUNK
    blk = lambda width: pl.BlockSpec((b, rows, width), lambda i: (0, i, 0))
    return pl.pallas_call(
        _dn_core_kernel,
        grid=(lp // rows,),
        in_specs=[blk(DN_KEY_WIDTH), blk(DN_KEY_WIDTH), blk(DN_VALUE_WIDTH), blk(DN_VALUE_WIDTH),
                  blk(DN_VALUE_WIDTH), blk(LANES)],
        out_specs=blk(DN_VALUE_WIDTH),
        out_shape=jax.ShapeDtypeStruct((b, lp, DN_VALUE_WIDTH), BF16),
        scratch_shapes=[pltpu.VMEM((b, DN_V_HEADS, DN_HEAD_DIM, DN_HEAD_DIM), F32)],
        compiler_params=pltpu.CompilerParams(dimension_semantics=("arbitrary",),
                                             vmem_limit_bytes=VMEM_LIMIT),
        name="dn_core",
    )(q, k, vb, kbe, qe, bg)


def _dn_out_kernel(h_ref, o_ref, z_ref, onw_ref, w_ref, out_ref):
    onw = onw_ref[...]
    acc = h_ref[...]
    heads_per_chunk = DN_OUT_K_CHUNK // DN_HEAD_DIM
    for c in range(DN_VALUE_WIDTH // DN_OUT_K_CHUNK):
        parts = []
        for hh in range(c * heads_per_chunk, (c + 1) * heads_per_chunk):
            cols = slice(hh * DN_HEAD_DIM, (hh + 1) * DN_HEAD_DIM)
            y = _rms_rows(o_ref[:, cols].astype(F32), onw) * _silu(z_ref[:, cols].astype(F32))
            parts.append(y.astype(BF16))
        acc = acc + jnp.dot(jnp.concatenate(parts, axis=1),
                            w_ref[c * DN_OUT_K_CHUNK:(c + 1) * DN_OUT_K_CHUNK, :],
                            preferred_element_type=F32)
    out_ref[...] = acc


def _dn_out(h, o, z, onw, w_out, seq):
    b, lp, d = h.shape
    skip = (lp - seq) // ROW_TILE
    in_block = lambda width: pl.BlockSpec((None, ROW_TILE, width), lambda bi, i: (bi, i + skip, 0))
    return pl.pallas_call(
        _dn_out_kernel,
        grid=(b, seq // ROW_TILE),
        in_specs=[in_block(d), in_block(DN_VALUE_WIDTH), in_block(DN_VALUE_WIDTH),
                  pl.BlockSpec((1, DN_HEAD_DIM), lambda bi, i: (0, 0)),
                  pl.BlockSpec(w_out.shape, lambda bi, i: (0, 0))],
        out_specs=pl.BlockSpec((None, ROW_TILE, d), lambda bi, i: (bi, i, 0)),
        out_shape=jax.ShapeDtypeStruct((b, seq, d), F32),
        compiler_params=pltpu.CompilerParams(dimension_semantics=("parallel", "parallel"),
                                             vmem_limit_bytes=VMEM_LIMIT),
        name="dn_out",
    )(h, o, z, onw, w_out)


def kernel(x, meta_tokens, attn_norm_w, attn_w_in, attn_q_norm_w, attn_k_norm_w, attn_sinks,
           attn_w_out, dn_norm_w, dn_w_in, dn_conv_w, dn_a_log, dn_dt_bias, dn_o_norm_w, dn_w_out):
    b, seq, d = x.shape
    assert seq % ROW_TILE == 0 and attn_norm_w.shape[0] == 1 and dn_norm_w.shape[0] == 1
    meta = jnp.broadcast_to(meta_tokens.astype(x.dtype)[None], (b, N_META, d))
    head = jnp.concatenate([jnp.zeros((b, PAD, d), x.dtype), meta], axis=1)

    q, kv, gate = _attn_in(head, x, attn_norm_w[0][None], attn_w_in[0].astype(BF16))
    og = _attention(q, kv, gate, attn_sinks[0], attn_q_norm_w[0][None], attn_k_norm_w[0][None])

    w_in = dn_w_in[0]
    w_qkv = w_in[:, :DN_CONV_WIDTH].astype(BF16)
    w_z = w_in[:, DN_CONV_WIDTH:DN_CONV_WIDTH + DN_VALUE_WIDTH].astype(BF16)
    n_ba = 2 * DN_V_HEADS
    w_ba = jnp.pad(w_in[:, DN_CONV_WIDTH + DN_VALUE_WIDTH:], ((0, 0), (0, LANES - n_ba))).astype(BF16)
    lane_pad = lambda t: jnp.pad(t[None], ((0, 0), (DN_V_HEADS, LANES - n_ba)))
    h1, qd, kd, vbd, kbed, qed, zd, bg = _dn_in(head, x, og, attn_w_out[0].astype(BF16), dn_norm_w[0][None],
                                    w_qkv, w_z, w_ba, dn_conv_w[0],
                                    lane_pad(dn_a_log[0]), lane_pad(dn_dt_bias[0]))
    od = _dn_core(qd, kd, vbd, kbed, qed, bg)
    return _dn_out(h1, od, zd, dn_o_norm_w[0][None], dn_w_out[0].astype(BF16), seq)
```

```python
import numpy as np
import jax
import jax.numpy as jnp
from jax import lax
from jax.experimental import pallas as pl
from jax.experimental.pallas import tpu as pltpu

F32 = jnp.float32
BF16 = jnp.bfloat16

N_META = 16
NORM_EPS = 1e-6

ATTN_HEAD_DIM = 64
ATTN_HEADS = 16
ATTN_KV_HEADS = 2
ATTN_GROUPS = ATTN_HEADS // ATTN_KV_HEADS
ATTN_WIDTH = ATTN_HEADS * ATTN_HEAD_DIM
ATTN_KV_WIDTH = ATTN_KV_HEADS * ATTN_HEAD_DIM
WINDOW = 128
ATTN_BLOCK = 128
ATTN_STEP_BLOCKS = 2

DN_HEAD_DIM = 128
DN_K_HEADS = 8
DN_V_HEADS = 16
DN_KEY_WIDTH = DN_K_HEADS * DN_HEAD_DIM
DN_VALUE_WIDTH = DN_V_HEADS * DN_HEAD_DIM
DN_CONV = 4
DN_CHUNK = 64
DN_CONV_WIDTH = 2 * DN_KEY_WIDTH + DN_VALUE_WIDTH

LANES = 128
SUBLANES = 8
ROW_TILE = 256
PAD = ROW_TILE - N_META
META_BLOCK = PAD // ATTN_BLOCK
META_OFFSET = PAD % ATTN_BLOCK
CONV_COLS = 512
FILL_COLS = 1024
CONV_PHASES = 4
DN_OUT_K_CHUNK = 512
DN_STEP_CHUNKS = 4
VMEM_LIMIT = 56 * 1024 * 1024

LOG2E = 1.4426950408889634
NEG_BIG = -1e30
CLIP_INVALID = 1e30


def _alibi_slopes(n_heads):
    return np.exp2(-8.0 * np.arange(1, n_heads + 1) / n_heads).astype(np.float32)


def _bdot(a, b):
    return jnp.dot(a.astype(BF16), b.astype(BF16), preferred_element_type=F32)


def _bdot_nt(a, b):
    return lax.dot_general(a.astype(BF16), b.astype(BF16), (((1,), (1,)), ((), ())),
                           preferred_element_type=F32)


def _rms_rows(x, w):
    return x * lax.rsqrt(jnp.mean(x * x, axis=-1, keepdims=True) + NORM_EPS) * w


def _silu(x):
    return x * jax.nn.sigmoid(x)


def _layer_input(head_ref, x_ref):
    return jnp.where(pl.program_id(1) == 0, head_ref[...], x_ref[...])


def _attn_in_kernel(head_ref, x_ref, nw_ref, w_ref, q_ref, kv_ref, g_ref):
    xn = _rms_rows(_layer_input(head_ref, x_ref), nw_ref[...]).astype(BF16)
    q_ref[...] = jnp.dot(xn, w_ref[:, :ATTN_WIDTH], preferred_element_type=F32).astype(BF16)
    kv_ref[...] = jnp.dot(xn, w_ref[:, ATTN_WIDTH:ATTN_WIDTH + 2 * ATTN_KV_WIDTH],
                          preferred_element_type=F32).astype(BF16)
    g_ref[...] = jnp.dot(xn, w_ref[:, ATTN_WIDTH + 2 * ATTN_KV_WIDTH:],
                         preferred_element_type=F32).astype(BF16)


def _padded_input_specs(d):
    return [pl.BlockSpec((None, ROW_TILE, d), lambda bi, i: (bi, 0, 0)),
            pl.BlockSpec((None, ROW_TILE, d), lambda bi, i: (bi, jnp.maximum(i - 1, 0), 0))]


def _attn_in(head, x, norm_w, w_in):
    b, seq, d = x.shape
    lp = seq + ROW_TILE
    n_in = w_in.shape[1]
    row_block = lambda width: pl.BlockSpec((None, ROW_TILE, width), lambda bi, i: (bi, i, 0))
    return pl.pallas_call(
        _attn_in_kernel,
        grid=(b, lp // ROW_TILE),
        in_specs=_padded_input_specs(d) + [pl.BlockSpec((1, d), lambda bi, i: (0, 0)),
                                           pl.BlockSpec((d, n_in), lambda bi, i: (0, 0))],
        out_specs=[row_block(ATTN_WIDTH), row_block(2 * ATTN_KV_WIDTH), row_block(ATTN_WIDTH)],
        out_shape=[jax.ShapeDtypeStruct((b, lp, ATTN_WIDTH), BF16),
                   jax.ShapeDtypeStruct((b, lp, 2 * ATTN_KV_WIDTH), BF16),
                   jax.ShapeDtypeStruct((b, lp, ATTN_WIDTH), BF16)],
        compiler_params=pltpu.CompilerParams(dimension_semantics=("parallel", "parallel"),
                                             vmem_limit_bytes=VMEM_LIMIT),
        name="attn_in",
    )(head, x, norm_w, w_in)


def _attn_kernel(sink_ref, q_ref, kvm_ref, kvp_ref, kvc_ref, gate_ref, qnw_ref, knw_ref, o_ref):
    def clip(dist, valid):
        return jnp.where(valid, jnp.minimum(dist, WINDOW).astype(F32), CLIP_INVALID)

    jm = lax.broadcasted_iota(jnp.int32, (N_META, ATTN_BLOCK), 0)
    qm = lax.broadcasted_iota(jnp.int32, (N_META, ATTN_BLOCK), 1)
    c = lax.broadcasted_iota(jnp.int32, (ATTN_BLOCK, ATTN_BLOCK), 0)
    qi = lax.broadcasted_iota(jnp.int32, (ATTN_BLOCK, ATTN_BLOCK), 1)
    slopes = _alibi_slopes(ATTN_HEADS) * np.float32(LOG2E)
    qnw = qnw_ref[...] * (ATTN_HEAD_DIM ** -0.5 * LOG2E)
    knw = knw_ref[...]

    def kv_rows(sub, cols):
        cur = kvc_ref[sub * ATTN_BLOCK:(sub + 1) * ATTN_BLOCK, cols]
        prev = kvp_ref[:, cols] if sub == 0 else kvc_ref[(sub - 1) * ATTN_BLOCK:sub * ATTN_BLOCK, cols]
        return jnp.concatenate([kvm_ref[:, cols], prev, cur], axis=0)

    units = [(sub, h) for sub in range(ATTN_STEP_BLOCKS) for h in range(ATTN_KV_HEADS)]
    s_d, v_d, clip_d = {}, {}, {}
    for sub in range(ATTN_STEP_BLOCKS):
        nref = pl.program_id(1) * ATTN_STEP_BLOCKS + sub - META_BLOCK
        dist_m = nref * ATTN_BLOCK + qm - (META_OFFSET + jm)
        clip_d[sub] = jnp.concatenate(
            [clip(dist_m, dist_m >= 0),
             clip(ATTN_BLOCK + qi - c, jnp.logical_and(c > qi, nref >= 2)),
             clip(qi - c, jnp.logical_and(c <= qi, nref >= 1))], axis=0)
    for (sub, h) in units:
        rows = slice(sub * ATTN_BLOCK, (sub + 1) * ATTN_BLOCK)
        k_ext = kv_rows(sub, slice(h * ATTN_HEAD_DIM, (h + 1) * ATTN_HEAD_DIM))
        k_ext = _rms_rows(k_ext.astype(F32), knw).astype(BF16)
        v_d[(sub, h)] = kv_rows(sub, slice(ATTN_KV_WIDTH + h * ATTN_HEAD_DIM,
                                           ATTN_KV_WIDTH + (h + 1) * ATTN_HEAD_DIM))
        group_cols = slice(h * ATTN_GROUPS * ATTN_HEAD_DIM, (h + 1) * ATTN_GROUPS * ATTN_HEAD_DIM)
        q_t = q_ref[rows, group_cols].astype(F32).T
        qn = []
        for g in range(ATTN_GROUPS):
            qg = q_t[g * ATTN_HEAD_DIM:(g + 1) * ATTN_HEAD_DIM]
            inv = lax.rsqrt(jnp.mean(qg * qg, axis=0, keepdims=True) + NORM_EPS)
            qn.append((qg * inv * qnw).astype(BF16))
        s_d[(sub, h)] = jnp.dot(k_ext, jnp.concatenate(qn, axis=1),
                                preferred_element_type=F32)

    for (sub, h) in units:
        rows = slice(sub * ATTN_BLOCK, (sub + 1) * ATTN_BLOCK)
        p_l, rinv_l = [], []
        for g in range(ATTN_GROUPS):
            hq = h * ATTN_GROUPS + g
            s = s_d[(sub, h)][:, g * ATTN_BLOCK:(g + 1) * ATTN_BLOCK] - float(slopes[hq]) * clip_d[sub]
            sink = sink_ref[hq] * LOG2E
            m = jnp.maximum(jnp.max(s, axis=0, keepdims=True), sink)
            p = jnp.exp2(s - m)
            rinv_l.append(1.0 / (jnp.sum(p, axis=0, keepdims=True) + jnp.exp2(sink - m)))
            p_l.append(p.astype(BF16))
        o_t = lax.dot_general(v_d[(sub, h)], jnp.concatenate(p_l, axis=1), (((0,), (0,)), ((), ())),
                              preferred_element_type=F32)
        o_t = jnp.concatenate(
            [o_t[:, g * ATTN_BLOCK:(g + 1) * ATTN_BLOCK] * rinv_l[g] for g in range(ATTN_GROUPS)],
            axis=0)
        group_cols = slice(h * ATTN_GROUPS * ATTN_HEAD_DIM, (h + 1) * ATTN_GROUPS * ATTN_HEAD_DIM)
        o_ref[rows, group_cols] = (o_t.T * _silu(gate_ref[rows, group_cols].astype(F32))).astype(BF16)


def _attention(q, kv, gate, sinks, qnw, knw):
    b, lp, _ = q.shape
    step_rows = ATTN_STEP_BLOCKS * ATTN_BLOCK
    kv_width = 2 * ATTN_KV_WIDTH
    wide_block = (None, step_rows, ATTN_WIDTH)
    qnw_lanes = jnp.broadcast_to(qnw.reshape(ATTN_HEAD_DIM, 1), (ATTN_HEAD_DIM, LANES))
    return pl.pallas_call(
        _attn_kernel,
        grid=(b, lp // step_rows),
        in_specs=[pl.BlockSpec(memory_space=pltpu.SMEM),
                  pl.BlockSpec(wide_block, lambda bi, n: (bi, n, 0)),
                  pl.BlockSpec((None, N_META, kv_width), lambda bi, n: (bi, PAD // N_META, 0)),
                  pl.BlockSpec((None, ATTN_BLOCK, kv_width),
                               lambda bi, n: (bi, jnp.maximum(n * ATTN_STEP_BLOCKS - 1, 0), 0)),
                  pl.BlockSpec((None, step_rows, kv_width), lambda bi, n: (bi, n, 0)),
                  pl.BlockSpec(wide_block, lambda bi, n: (bi, n, 0)),
                  pl.BlockSpec((ATTN_HEAD_DIM, LANES), lambda bi, n: (0, 0)),
                  pl.BlockSpec((1, ATTN_HEAD_DIM), lambda bi, n: (0, 0))],
        out_specs=pl.BlockSpec(wide_block, lambda bi, n: (bi, n, 0)),
        out_shape=jax.ShapeDtypeStruct((b, lp, ATTN_WIDTH), BF16),
        compiler_params=pltpu.CompilerParams(dimension_semantics=("parallel", "parallel"),
                                             vmem_limit_bytes=VMEM_LIMIT),
        name="attn_core",
    )(sinks, q, kv, kv, kv, gate, qnw_lanes, knw)


def _dn_in_kernel(head_ref, og0_ref, xnext_ref, ognext_ref, wout_ref, nw_ref, wqkv_ref, wz_ref, wba_ref,
                  convw_ref, alog_ref, dtb_ref, tri_ref,
                  h1_ref, q_ref, k_ref, vb_ref, kbe_ref, qe_ref, z_ref, bg_ref,
                  buf_ref, ybuf_ref, halo_ref, h1s_ref, xn_ref):
    i = pl.program_id(1)
    halo = SUBLANES

    @pl.when(i == 0)
    def _():
        halo_ref[...] = jnp.zeros(halo_ref.shape, F32)
        h1s_ref[...] = head_ref[...] + jnp.dot(og0_ref[...], wout_ref[...], preferred_element_type=F32)
        xn_ref[...] = _rms_rows(h1s_ref[...], nw_ref[...]).astype(BF16)

    h1_ref[...] = h1s_ref[...]
    xn = xn_ref[...]

    def z_part(c0):
        def run():
            z_ref[:, c0:c0 + FILL_COLS] = jnp.dot(xn, wz_ref[:, c0:c0 + FILL_COLS],
                                                  preferred_element_type=F32).astype(BF16)
        return run

    def next_h1_part(c0):
        def run():
            h1s_ref[:, c0:c0 + FILL_COLS] = xnext_ref[:, c0:c0 + FILL_COLS] + jnp.dot(
                ognext_ref[...], wout_ref[:, c0:c0 + FILL_COLS], preferred_element_type=F32)
        return run

    fillers = ([z_part(c0) for c0 in range(0, DN_VALUE_WIDTH, FILL_COLS)]
               + [next_h1_part(c0) for c0 in range(0, h1s_ref.shape[1], FILL_COLS)])

    ba = jnp.dot(xn, wba_ref[...], preferred_element_type=F32)
    lane = lax.broadcasted_iota(jnp.int32, ba.shape, 1)
    row = i * ROW_TILE + lax.broadcasted_iota(jnp.int32, ba.shape, 0)
    x = ba + dtb_ref[...]
    softplus = jnp.maximum(x, 0.0) + jnp.log1p(jnp.exp(-jnp.abs(x)))
    g = -jnp.exp(alog_ref[...]) * softplus
    bg = jnp.where(lane < DN_V_HEADS, jax.nn.sigmoid(ba), g)
    bg = jnp.where(row >= PAD, bg, 0.0)
    csum = jnp.dot(tri_ref[...], bg, preferred_element_type=F32, precision=lax.Precision.HIGHEST)
    bg = jnp.where(lane < DN_V_HEADS, bg, csum)
    bg_ref[...] = bg
    egc = jnp.exp(bg)
    reps = DN_V_HEADS // DN_K_HEADS

    q_scale = DN_HEAD_DIM ** -0.5
    rows_per_phase = ROW_TILE // CONV_PHASES
    slabs_per_chunk = CONV_COLS // LANES
    for cj in range(DN_CONV_WIDTH // CONV_COLS):
        u = jnp.dot(xn, wqkv_ref[:, cj * CONV_COLS:(cj + 1) * CONV_COLS],
                    preferred_element_type=F32)
        if cj >= 1 and fillers:
            fillers.pop(0)()
        for sl in range(slabs_per_chunk):
            slab = cj * slabs_per_chunk + sl
            col0 = slab * LANES
            buf_ref[sl, 0:halo, :] = halo_ref[slab]
            buf_ref[sl, halo:halo + ROW_TILE, :] = u[:, sl * LANES:(sl + 1) * LANES]
            w4 = convw_ref[:, col0:col0 + LANES]
            taps = {}
            for start in range(halo - (DN_CONV - 1), halo + CONV_PHASES):
                taps[start] = buf_ref[sl, pl.ds(start, rows_per_phase, stride=CONV_PHASES), :]
            for a in range(CONV_PHASES):
                acc = None
                for j in range(DN_CONV):
                    term = w4[DN_CONV - 1 - j:DN_CONV - j] * taps[halo + a - j]
                    acc = term if acc is None else acc + term
                ybuf_ref[sl, pl.ds(a, rows_per_phase, stride=CONV_PHASES), :] = _silu(acc)
            halo_ref[slab] = buf_ref[sl, ROW_TILE:ROW_TILE + halo, :]
            y = ybuf_ref[sl]
            if col0 < DN_KEY_WIDTH:
                kh = slab
                qn = y * (lax.rsqrt(jnp.sum(y * y, axis=-1, keepdims=True) + NORM_EPS) * q_scale)
                q_ref[:, col0:col0 + LANES] = qn.astype(BF16)
                for h in range(reps * kh, reps * (kh + 1)):
                    gl = DN_V_HEADS + h
                    qe_ref[:, h * LANES:(h + 1) * LANES] = (qn * egc[:, gl:gl + 1]).astype(BF16)
            elif col0 < 2 * DN_KEY_WIDTH:
                kh = slab - DN_K_HEADS
                kn = y * lax.rsqrt(jnp.sum(y * y, axis=-1, keepdims=True) + NORM_EPS)
                k_ref[:, kh * LANES:(kh + 1) * LANES] = kn.astype(BF16)
                for h in range(reps * kh, reps * (kh + 1)):
                    gl = DN_V_HEADS + h
                    kbe_ref[:, h * LANES:(h + 1) * LANES] = (
                        kn * (bg[:, h:h + 1] * egc[:, gl:gl + 1])).astype(BF16)
            else:
                h = slab - 2 * DN_K_HEADS
                vb_ref[:, h * LANES:(h + 1) * LANES] = (y * bg[:, h:h + 1]).astype(BF16)

    for run in fillers:
        run()
    xn_ref[...] = _rms_rows(h1s_ref[...], nw_ref[...]).astype(BF16)


def _dn_in(head, x, og, w_out, norm_w, w_qkv, w_z, w_ba, conv_w, alog_row, dtb_row):
    b, seq, d = x.shape
    lp = seq + ROW_TILE
    nt = lp // ROW_TILE
    n_x = seq // ROW_TILE
    idx = np.arange(ROW_TILE)
    tri = jnp.asarray((idx[:, None] >= idx[None, :])
                      & (idx[:, None] // DN_CHUNK == idx[None, :] // DN_CHUNK), F32)
    row_block = lambda width: pl.BlockSpec((None, ROW_TILE, width), lambda bi, i: (bi, i, 0))
    full = lambda shape: pl.BlockSpec(shape, lambda bi, i: (0,) * len(shape),
                                      pipeline_mode=pl.Buffered(1))
    return pl.pallas_call(
        _dn_in_kernel,
        grid=(b, nt),
        in_specs=[
            pl.BlockSpec((None, ROW_TILE, d), lambda bi, i: (bi, 0, 0)),
            pl.BlockSpec((None, ROW_TILE, og.shape[-1]), lambda bi, i: (bi, 0, 0)),
            pl.BlockSpec((None, ROW_TILE, d), lambda bi, i: (bi, jnp.minimum(i, n_x - 1), 0)),
            pl.BlockSpec((None, ROW_TILE, og.shape[-1]), lambda bi, i: (bi, jnp.minimum(i + 1, nt - 1), 0)),
            full(w_out.shape), full((1, d)), full(w_qkv.shape), full(w_z.shape),
            full(w_ba.shape), full(conv_w.shape), full((1, LANES)), full((1, LANES)), full(tri.shape)],
        out_specs=[row_block(d), row_block(DN_KEY_WIDTH), row_block(DN_KEY_WIDTH),
                   row_block(DN_VALUE_WIDTH), row_block(DN_VALUE_WIDTH), row_block(DN_VALUE_WIDTH),
                   row_block(DN_VALUE_WIDTH), row_block(LANES)],
        out_shape=[jax.ShapeDtypeStruct((b, lp, d), F32),
                   jax.ShapeDtypeStruct((b, lp, DN_KEY_WIDTH), BF16),
                   jax.ShapeDtypeStruct((b, lp, DN_KEY_WIDTH), BF16),
                   jax.ShapeDtypeStruct((b, lp, DN_VALUE_WIDTH), BF16),
                   jax.ShapeDtypeStruct((b, lp, DN_VALUE_WIDTH), BF16),
                   jax.ShapeDtypeStruct((b, lp, DN_VALUE_WIDTH), BF16),
                   jax.ShapeDtypeStruct((b, lp, DN_VALUE_WIDTH), BF16),
                   jax.ShapeDtypeStruct((b, lp, LANES), F32)],
        scratch_shapes=[pltpu.VMEM((CONV_COLS // LANES, ROW_TILE + SUBLANES, LANES), F32),
                        pltpu.VMEM((CONV_COLS // LANES, ROW_TILE, LANES), F32),
                        pltpu.VMEM((DN_CONV_WIDTH // LANES, SUBLANES, LANES), F32),
                        pltpu.VMEM((ROW_TILE, d), F32),
                        pltpu.VMEM((ROW_TILE, d), BF16)],
        compiler_params=pltpu.CompilerParams(dimension_semantics=("parallel", "arbitrary"),
                                             vmem_limit_bytes=VMEM_LIMIT),
        name="dn_in",
    )(head, og, x, og, w_out, norm_w, w_qkv, w_z, w_ba, conv_w, alog_row, dtb_row, tri)


def _dn_core_kernel(q_ref, k_ref, vb_ref, kbe_ref, qe_ref, bg_ref, o_ref, s_ref):
    @pl.when(pl.program_id(0) == 0)
    def _():
        s_ref[...] = jnp.zeros(s_ref.shape, F32)

    n_seq = q_ref.shape[0]
    reps = DN_V_HEADS // DN_K_HEADS
    ri = lax.broadcasted_iota(jnp.int32, (DN_CHUNK, DN_CHUNK), 0)
    ci = lax.broadcasted_iota(jnp.int32, (DN_CHUNK, DN_CHUNK), 1)
    causal = ri >= ci
    neg_strict = -(ri > ci).astype(F32)
    eye = (ri == ci).astype(F32)
    keep_t = (lax.broadcasted_iota(jnp.int32, (DN_CHUNK, 2 * DN_CHUNK), 1) >= DN_CHUNK).astype(F32)

    insts = [(b, c) for c in range(DN_STEP_CHUNKS) for b in range(n_seq)]

    qk_d, kk_d, kt_d = {}, {}, {}
    for (b, c) in insts:
        rows = slice(c * DN_CHUNK, (c + 1) * DN_CHUNK)
        for kh in range(DN_K_HEADS):
            kcols = slice(kh * DN_HEAD_DIM, (kh + 1) * DN_HEAD_DIM)
            k = k_ref[b, rows, kcols]
            qk_kk = _bdot_nt(jnp.concatenate([q_ref[b, rows, kcols], k], axis=0), k)
            key = (b, c, kh)
            qk_d[key], kk_d[key] = qk_kk[:DN_CHUNK], qk_kk[DN_CHUNK:]
            kt_d[key] = k.astype(F32).T

    lhs2_d, y_d, rhs_d, qe_d, g_last_d = {}, {}, {}, {}, {}
    for (b, c) in insts:
        rows = slice(c * DN_CHUNK, (c + 1) * DN_CHUNK)
        bg = bg_ref[b, rows, :]
        bg_t = bg.T
        for h in range(DN_V_HEADS):
            key, kkey = (b, c, h), (b, c, h // reps)
            gl = DN_V_HEADS + h
            beta_col = bg[:, h:h + 1]
            gc_col = bg[:, gl:gl + 1]
            gc_row = bg_t[gl:gl + 1, :]
            decay = jnp.exp(jnp.where(causal, gc_col - gc_row, NEG_BIG))
            g_last = gc_col[DN_CHUNK - 1:DN_CHUNK, :]
            lhs2_d[key] = jnp.concatenate(
                [(qk_d[kkey] * decay).astype(BF16),
                 (kt_d[kkey] * jnp.exp(g_last - gc_row)).astype(BF16)], axis=0)
            n1 = kk_d[kkey] * decay * (neg_strict * beta_col)
            y_d[key] = jnp.concatenate([n1, eye], axis=1)
            hcols = slice(h * DN_HEAD_DIM, (h + 1) * DN_HEAD_DIM)
            rhs_d[key] = jnp.concatenate([vb_ref[b, rows, hcols], kbe_ref[b, rows, hcols]], axis=1)
            qe_d[key] = qe_ref[b, rows, hcols]
            g_last_d[key] = g_last

    chunk_heads = [(b, h) for b in range(n_seq) for h in range(DN_V_HEADS)]
    u_d, w_d, ws_qs_d = {}, {}, {}
    s_d = {(b, h): s_ref[b, h] for (b, h) in chunk_heads}

    def inverse_round(c):
        for (b, h) in chunk_heads:
            key = (b, c, h)
            r = _bdot(y_d[key][:, :DN_CHUNK], y_d[key])
            y_d[key] = r + y_d[key] * keep_t

    def solve(c):
        for (b, h) in chunk_heads:
            key = (b, c, h)
            uw = _bdot(y_d[key][:, DN_CHUNK:], rhs_d[key])
            u_d[key], w_d[key] = uw[:, :DN_HEAD_DIM], uw[:, DN_HEAD_DIM:].astype(BF16)

    def state_read(c):
        for (b, h) in chunk_heads:
            key = (b, c, h)
            ws_qs_d[key] = _bdot(jnp.concatenate([w_d[key], qe_d[key]], axis=0), s_d[(b, h)])

    def state_update(c):
        rows = slice(c * DN_CHUNK, (c + 1) * DN_CHUNK)
        for (b, h) in chunk_heads:
            key = (b, c, h)
            v_new = u_d[key] - ws_qs_d[key][:DN_CHUNK]
            av = _bdot(lhs2_d[key], v_new)
            o = ws_qs_d[key][DN_CHUNK:] + av[:DN_CHUNK]
            o_ref[b, rows, h * DN_HEAD_DIM:(h + 1) * DN_HEAD_DIM] = o.astype(BF16)
            s_d[(b, h)] = s_d[(b, h)] * jnp.exp(g_last_d[key]) + av[DN_CHUNK:]

    for c in range(DN_STEP_CHUNKS):
        between = [lambda: state_read(c - 1), lambda: state_update(c - 1)] if c > 0 else []
        for rnd in range(6):
            inverse_round(c)
            if rnd in (0, 2) and between:
                between.pop(0)()
        solve(c)
    state_read(DN_STEP_CHUNKS - 1)
    state_update(DN_STEP_CHUNKS - 1)
    for (b, h), s_new in s_d.items():
        s_ref[b, h] = s_new


def _dn_core(q, k, vb, kbe, qe, bg):
    b, lp, _ = q.shape
    rows = DN_STEP_CHUNKS * DN_CHUNK
    blk = lambda width: pl.BlockSpec((b, rows, width), lambda i: (0, i, 0))
    return pl.pallas_call(
        _dn_core_kernel,
        grid=(lp // rows,),
        in_specs=[blk(DN_KEY_WIDTH), blk(DN_KEY_WIDTH), blk(DN_VALUE_WIDTH), blk(DN_VALUE_WIDTH),
                  blk(DN_VALUE_WIDTH), blk(LANES)],
        out_specs=blk(DN_VALUE_WIDTH),
        out_shape=jax.ShapeDtypeStruct((b, lp, DN_VALUE_WIDTH), BF16),
        scratch_shapes=[pltpu.VMEM((b, DN_V_HEADS, DN_HEAD_DIM, DN_HEAD_DIM), F32)],
        compiler_params=pltpu.CompilerParams(dimension_semantics=("arbitrary",),
                                             vmem_limit_bytes=VMEM_LIMIT),
        name="dn_core",
    )(q, k, vb, kbe, qe, bg)


def _dn_out_kernel(h_ref, o_ref, z_ref, onw_ref, w_ref, out_ref):
    onw = onw_ref[...]
    acc = h_ref[...]
    heads_per_chunk = DN_OUT_K_CHUNK // DN_HEAD_DIM
    for c in range(DN_VALUE_WIDTH // DN_OUT_K_CHUNK):
        parts = []
        for hh in range(c * heads_per_chunk, (c + 1) * heads_per_chunk):
            cols = slice(hh * DN_HEAD_DIM, (hh + 1) * DN_HEAD_DIM)
            y = _rms_rows(o_ref[:, cols].astype(F32), onw) * _silu(z_ref[:, cols].astype(F32))
            parts.append(y.astype(BF16))
        acc = acc + jnp.dot(jnp.concatenate(parts, axis=1),
                            w_ref[c * DN_OUT_K_CHUNK:(c + 1) * DN_OUT_K_CHUNK, :],
                            preferred_element_type=F32)
    out_ref[...] = acc


def _dn_out(h, o, z, onw, w_out, seq):
    b, lp, d = h.shape
    skip = (lp - seq) // ROW_TILE
    in_block = lambda width: pl.BlockSpec((None, ROW_TILE, width), lambda bi, i: (bi, i + skip, 0))
    return pl.pallas_call(
        _dn_out_kernel,
        grid=(b, seq // ROW_TILE),
        in_specs=[in_block(d), in_block(DN_VALUE_WIDTH), in_block(DN_VALUE_WIDTH),
                  pl.BlockSpec((1, DN_HEAD_DIM), lambda bi, i: (0, 0)),
                  pl.BlockSpec(w_out.shape, lambda bi, i: (0, 0))],
        out_specs=pl.BlockSpec((None, ROW_TILE, d), lambda bi, i: (bi, i, 0)),
        out_shape=jax.ShapeDtypeStruct((b, seq, d), F32),
        compiler_params=pltpu.CompilerParams(dimension_semantics=("parallel", "parallel"),
                                             vmem_limit_bytes=VMEM_LIMIT),
        name="dn_out",
    )(h, o, z, onw, w_out)


def kernel(x, meta_tokens, attn_norm_w, attn_w_in, attn_q_norm_w, attn_k_norm_w, attn_sinks,
           attn_w_out, dn_norm_w, dn_w_in, dn_conv_w, dn_a_log, dn_dt_bias, dn_o_norm_w, dn_w_out):
    b, seq, d = x.shape
    assert seq % ROW_TILE == 0 and attn_norm_w.shape[0] == 1 and dn_norm_w.shape[0] == 1
    meta = jnp.broadcast_to(meta_tokens.astype(x.dtype)[None], (b, N_META, d))
    head = jnp.concatenate([jnp.zeros((b, PAD, d), x.dtype), meta], axis=1)

    q, kv, gate = _attn_in(head, x, attn_norm_w[0][None], attn_w_in[0].astype(BF16))
    og = _attention(q, kv, gate, attn_sinks[0], attn_q_norm_w[0][None], attn_k_norm_w[0][None])

    w_in = dn_w_in[0]
    w_qkv = w_in[:, :DN_CONV_WIDTH].astype(BF16)
    w_z = w_in[:, DN_CONV_WIDTH:DN_CONV_WIDTH + DN_VALUE_WIDTH].astype(BF16)
    n_ba = 2 * DN_V_HEADS
    w_ba = jnp.pad(w_in[:, DN_CONV_WIDTH + DN_VALUE_WIDTH:], ((0, 0), (0, LANES - n_ba))).astype(BF16)
    lane_pad = lambda t: jnp.pad(t[None], ((0, 0), (DN_V_HEADS, LANES - n_ba)))
    h1, qd, kd, vbd, kbed, qed, zd, bg = _dn_in(head, x, og, attn_w_out[0].astype(BF16), dn_norm_w[0][None],
                                    w_qkv, w_z, w_ba, dn_conv_w[0],
                                    lane_pad(dn_a_log[0]), lane_pad(dn_dt_bias[0]))
    od = _dn_core(qd, kd, vbd, kbed, qed, bg)
    return _dn_out(h1, od, zd, dn_o_norm_w[0][None], dn_w_out[0].astype(BF16), seq)
```

```python
import numpy as np
import jax
import jax.numpy as jnp
from jax import lax
from jax.experimental import pallas as pl
from jax.experimental.pallas import tpu as pltpu

F32 = jnp.float32
BF16 = jnp.bfloat16

N_META = 16
NORM_EPS = 1e-6

ATTN_HEAD_DIM = 64
ATTN_HEADS = 16
ATTN_KV_HEADS = 2
ATTN_GROUPS = ATTN_HEADS // ATTN_KV_HEADS
ATTN_WIDTH = ATTN_HEADS * ATTN_HEAD_DIM
ATTN_KV_WIDTH = ATTN_KV_HEADS * ATTN_HEAD_DIM
WINDOW = 128
ATTN_BLOCK = 128
ATTN_STEP_BLOCKS = 2

DN_HEAD_DIM = 128
DN_K_HEADS = 8
DN_V_HEADS = 16
DN_KEY_WIDTH = DN_K_HEADS * DN_HEAD_DIM
DN_VALUE_WIDTH = DN_V_HEADS * DN_HEAD_DIM
DN_CONV = 4
DN_CHUNK = 64
DN_CONV_WIDTH = 2 * DN_KEY_WIDTH + DN_VALUE_WIDTH

LANES = 128
SUBLANES = 8
ROW_TILE = 256
PAD = ROW_TILE - N_META
META_BLOCK = PAD // ATTN_BLOCK
META_OFFSET = PAD % ATTN_BLOCK
CONV_COLS = 512
FILL_COLS = 1024
CONV_PHASES = 4
DN_OUT_K_CHUNK = 512
DN_STEP_CHUNKS = 4
WEIGHT_PAD_COLS = 256
VMEM_LIMIT = 56 * 1024 * 1024

LOG2E = 1.4426950408889634
NEG_BIG = -1e30
CLIP_INVALID = 1e30


def _alibi_slopes(n_heads):
    return np.exp2(-8.0 * np.arange(1, n_heads + 1) / n_heads).astype(np.float32)


def _bdot(a, b):
    return jnp.dot(a.astype(BF16), b.astype(BF16), preferred_element_type=F32)


def _bdot_nt(a, b):
    return lax.dot_general(a.astype(BF16), b.astype(BF16), (((1,), (1,)), ((), ())),
                           preferred_element_type=F32)


def _rms_rows(x, w):
    return x * lax.rsqrt(jnp.mean(x * x, axis=-1, keepdims=True) + NORM_EPS) * w


def _silu(x):
    return x * jax.nn.sigmoid(x)


def _mxu_weight(w):
    return jnp.pad(w.astype(BF16), ((0, 0), (0, WEIGHT_PAD_COLS)))


def _layer_input(head_ref, x_ref):
    return jnp.where(pl.program_id(1) == 0, head_ref[...], x_ref[...])


def _attn_in_kernel(head_ref, x_ref, nw_ref, w_ref, q_ref, kv_ref, g_ref):
    xn = _rms_rows(_layer_input(head_ref, x_ref), nw_ref[...]).astype(BF16)
    q_ref[...] = jnp.dot(xn, w_ref[:, :ATTN_WIDTH], preferred_element_type=F32).astype(BF16)
    kv_ref[...] = jnp.dot(xn, w_ref[:, ATTN_WIDTH:ATTN_WIDTH + 2 * ATTN_KV_WIDTH],
                          preferred_element_type=F32).astype(BF16)
    g_ref[...] = jnp.dot(xn, w_ref[:, ATTN_WIDTH + 2 * ATTN_KV_WIDTH:],
                         preferred_element_type=F32).astype(BF16)


def _padded_input_specs(d):
    return [pl.BlockSpec((None, ROW_TILE, d), lambda bi, i: (bi, 0, 0)),
            pl.BlockSpec((None, ROW_TILE, d), lambda bi, i: (bi, jnp.maximum(i - 1, 0), 0))]


def _attn_in(head, x, norm_w, w_in):
    b, seq, d = x.shape
    lp = seq + ROW_TILE
    n_in = w_in.shape[1]
    row_block = lambda width: pl.BlockSpec((None, ROW_TILE, width), lambda bi, i: (bi, i, 0))
    return pl.pallas_call(
        _attn_in_kernel,
        grid=(b, lp // ROW_TILE),
        in_specs=_padded_input_specs(d) + [pl.BlockSpec((1, d), lambda bi, i: (0, 0)),
                                           pl.BlockSpec((d, n_in), lambda bi, i: (0, 0))],
        out_specs=[row_block(ATTN_WIDTH), row_block(2 * ATTN_KV_WIDTH), row_block(ATTN_WIDTH)],
        out_shape=[jax.ShapeDtypeStruct((b, lp, ATTN_WIDTH), BF16),
                   jax.ShapeDtypeStruct((b, lp, 2 * ATTN_KV_WIDTH), BF16),
                   jax.ShapeDtypeStruct((b, lp, ATTN_WIDTH), BF16)],
        compiler_params=pltpu.CompilerParams(dimension_semantics=("parallel", "parallel"),
                                             vmem_limit_bytes=VMEM_LIMIT),
        name="attn_in",
    )(head, x, norm_w, w_in)


def _attn_kernel(sink_ref, q_ref, kvm_ref, kvp_ref, kvc_ref, gate_ref, qnw_ref, knw_ref, o_ref):
    def clip(dist, valid):
        return jnp.where(valid, jnp.minimum(dist, WINDOW).astype(F32), CLIP_INVALID)

    jm = lax.broadcasted_iota(jnp.int32, (N_META, ATTN_BLOCK), 0)
    qm = lax.broadcasted_iota(jnp.int32, (N_META, ATTN_BLOCK), 1)
    c = lax.broadcasted_iota(jnp.int32, (ATTN_BLOCK, ATTN_BLOCK), 0)
    qi = lax.broadcasted_iota(jnp.int32, (ATTN_BLOCK, ATTN_BLOCK), 1)
    slopes = _alibi_slopes(ATTN_HEADS) * np.float32(LOG2E)
    qnw = qnw_ref[...] * (ATTN_HEAD_DIM ** -0.5 * LOG2E)
    knw = knw_ref[...]

    def kv_rows(sub, cols):
        cur = kvc_ref[sub * ATTN_BLOCK:(sub + 1) * ATTN_BLOCK, cols]
        prev = kvp_ref[:, cols] if sub == 0 else kvc_ref[(sub - 1) * ATTN_BLOCK:sub * ATTN_BLOCK, cols]
        return jnp.concatenate([kvm_ref[:, cols], prev, cur], axis=0)

    units = [(sub, h) for sub in range(ATTN_STEP_BLOCKS) for h in range(ATTN_KV_HEADS)]
    s_d, v_d, clip_d = {}, {}, {}
    for sub in range(ATTN_STEP_BLOCKS):
        nref = pl.program_id(1) * ATTN_STEP_BLOCKS + sub - META_BLOCK
        dist_m = nref * ATTN_BLOCK + qm - (META_OFFSET + jm)
        clip_d[sub] = jnp.concatenate(
            [clip(dist_m, dist_m >= 0),
             clip(ATTN_BLOCK + qi - c, jnp.logical_and(c > qi, nref >= 2)),
             clip(qi - c, jnp.logical_and(c <= qi, nref >= 1))], axis=0)
    for (sub, h) in units:
        rows = slice(sub * ATTN_BLOCK, (sub + 1) * ATTN_BLOCK)
        k_ext = kv_rows(sub, slice(h * ATTN_HEAD_DIM, (h + 1) * ATTN_HEAD_DIM))
        k_ext = _rms_rows(k_ext.astype(F32), knw).astype(BF16)
        v_d[(sub, h)] = kv_rows(sub, slice(ATTN_KV_WIDTH + h * ATTN_HEAD_DIM,
                                           ATTN_KV_WIDTH + (h + 1) * ATTN_HEAD_DIM))
        group_cols = slice(h * ATTN_GROUPS * ATTN_HEAD_DIM, (h + 1) * ATTN_GROUPS * ATTN_HEAD_DIM)
        q_t = q_ref[rows, group_cols].astype(F32).T
        qn = []
        for g in range(ATTN_GROUPS):
            qg = q_t[g * ATTN_HEAD_DIM:(g + 1) * ATTN_HEAD_DIM]
            inv = lax.rsqrt(jnp.mean(qg * qg, axis=0, keepdims=True) + NORM_EPS)
            qn.append((qg * inv * qnw).astype(BF16))
        s_d[(sub, h)] = jnp.dot(k_ext, jnp.concatenate(qn, axis=1),
                                preferred_element_type=F32)

    for (sub, h) in units:
        rows = slice(sub * ATTN_BLOCK, (sub + 1) * ATTN_BLOCK)
        p_l, rinv_l = [], []
        for g in range(ATTN_GROUPS):
            hq = h * ATTN_GROUPS + g
            s = s_d[(sub, h)][:, g * ATTN_BLOCK:(g + 1) * ATTN_BLOCK] - float(slopes[hq]) * clip_d[sub]
            sink = sink_ref[hq] * LOG2E
            m = jnp.maximum(jnp.max(s, axis=0, keepdims=True), sink)
            p = jnp.exp2(s - m)
            rinv_l.append(1.0 / (jnp.sum(p, axis=0, keepdims=True) + jnp.exp2(sink - m)))
            p_l.append(p.astype(BF16))
        o_t = lax.dot_general(v_d[(sub, h)], jnp.concatenate(p_l, axis=1), (((0,), (0,)), ((), ())),
                              preferred_element_type=F32)
        o_t = jnp.concatenate(
            [o_t[:, g * ATTN_BLOCK:(g + 1) * ATTN_BLOCK] * rinv_l[g] for g in range(ATTN_GROUPS)],
            axis=0)
        group_cols = slice(h * ATTN_GROUPS * ATTN_HEAD_DIM, (h + 1) * ATTN_GROUPS * ATTN_HEAD_DIM)
        o_ref[rows, group_cols] = (o_t.T * _silu(gate_ref[rows, group_cols].astype(F32))).astype(BF16)


def _attention(q, kv, gate, sinks, qnw, knw):
    b, lp, _ = q.shape
    step_rows = ATTN_STEP_BLOCKS * ATTN_BLOCK
    kv_width = 2 * ATTN_KV_WIDTH
    wide_block = (None, step_rows, ATTN_WIDTH)
    qnw_lanes = jnp.broadcast_to(qnw.reshape(ATTN_HEAD_DIM, 1), (ATTN_HEAD_DIM, LANES))
    return pl.pallas_call(
        _attn_kernel,
        grid=(b, lp // step_rows),
        in_specs=[pl.BlockSpec(memory_space=pltpu.SMEM),
                  pl.BlockSpec(wide_block, lambda bi, n: (bi, n, 0)),
                  pl.BlockSpec((None, N_META, kv_width), lambda bi, n: (bi, PAD // N_META, 0)),
                  pl.BlockSpec((None, ATTN_BLOCK, kv_width),
                               lambda bi, n: (bi, jnp.maximum(n * ATTN_STEP_BLOCKS - 1, 0), 0)),
                  pl.BlockSpec((None, step_rows, kv_width), lambda bi, n: (bi, n, 0)),
                  pl.BlockSpec(wide_block, lambda bi, n: (bi, n, 0)),
                  pl.BlockSpec((ATTN_HEAD_DIM, LANES), lambda bi, n: (0, 0)),
                  pl.BlockSpec((1, ATTN_HEAD_DIM), lambda bi, n: (0, 0))],
        out_specs=pl.BlockSpec(wide_block, lambda bi, n: (bi, n, 0)),
        out_shape=jax.ShapeDtypeStruct((b, lp, ATTN_WIDTH), BF16),
        compiler_params=pltpu.CompilerParams(dimension_semantics=("parallel", "parallel"),
                                             vmem_limit_bytes=VMEM_LIMIT),
        name="attn_core",
    )(sinks, q, kv, kv, kv, gate, qnw_lanes, knw)


def _dn_in_kernel(head_ref, og0_ref, xnext_ref, ognext_ref, wout_ref, nw_ref, wqkv_ref, wz_ref, wba_ref,
                  convw_ref, alog_ref, dtb_ref, tri_ref,
                  h1_ref, q_ref, k_ref, vb_ref, kbe_ref, qe_ref, z_ref, bg_ref,
                  buf_ref, ybuf_ref, halo_ref, h1s_ref, xn_ref):
    i = pl.program_id(1)
    halo = SUBLANES

    @pl.when(i == 0)
    def _():
        halo_ref[...] = jnp.zeros(halo_ref.shape, F32)
        h1s_ref[...] = head_ref[...] + jnp.dot(og0_ref[...], wout_ref[:, :h1s_ref.shape[1]],
                                               preferred_element_type=F32)
        xn_ref[...] = _rms_rows(h1s_ref[...], nw_ref[...]).astype(BF16)

    h1_ref[...] = h1s_ref[...]
    xn = xn_ref[...]

    def z_part(c0):
        def run():
            z_ref[:, c0:c0 + FILL_COLS] = jnp.dot(xn, wz_ref[:, c0:c0 + FILL_COLS],
                                                  preferred_element_type=F32).astype(BF16)
        return run

    def next_h1_part(c0):
        def run():
            h1s_ref[:, c0:c0 + FILL_COLS] = xnext_ref[:, c0:c0 + FILL_COLS] + jnp.dot(
                ognext_ref[...], wout_ref[:, c0:c0 + FILL_COLS], preferred_element_type=F32)
        return run

    fillers = ([z_part(c0) for c0 in range(0, DN_VALUE_WIDTH, FILL_COLS)]
               + [next_h1_part(c0) for c0 in range(0, h1s_ref.shape[1], FILL_COLS)])

    ba = jnp.dot(xn, wba_ref[...], preferred_element_type=F32)
    lane = lax.broadcasted_iota(jnp.int32, ba.shape, 1)
    row = i * ROW_TILE + lax.broadcasted_iota(jnp.int32, ba.shape, 0)
    x = ba + dtb_ref[...]
    softplus = jnp.maximum(x, 0.0) + jnp.log1p(jnp.exp(-jnp.abs(x)))
    g = -jnp.exp(alog_ref[...]) * softplus
    bg = jnp.where(lane < DN_V_HEADS, jax.nn.sigmoid(ba), g)
    bg = jnp.where(row >= PAD, bg, 0.0)
    csum = jnp.dot(tri_ref[...], bg, preferred_element_type=F32, precision=lax.Precision.HIGHEST)
    bg = jnp.where(lane < DN_V_HEADS, bg, csum)
    bg_ref[...] = bg
    egc = jnp.exp(bg)
    reps = DN_V_HEADS // DN_K_HEADS

    q_scale = DN_HEAD_DIM ** -0.5
    rows_per_phase = ROW_TILE // CONV_PHASES
    slabs_per_chunk = CONV_COLS // LANES
    for cj in range(DN_CONV_WIDTH // CONV_COLS):
        u = jnp.dot(xn, wqkv_ref[:, cj * CONV_COLS:(cj + 1) * CONV_COLS],
                    preferred_element_type=F32)
        if cj >= 1 and fillers:
            fillers.pop(0)()
        for sl in range(slabs_per_chunk):
            slab = cj * slabs_per_chunk + sl
            col0 = slab * LANES
            buf_ref[sl, 0:halo, :] = halo_ref[slab]
            buf_ref[sl, halo:halo + ROW_TILE, :] = u[:, sl * LANES:(sl + 1) * LANES]
            w4 = convw_ref[:, col0:col0 + LANES]
            taps = {}
            for start in range(halo - (DN_CONV - 1), halo + CONV_PHASES):
                taps[start] = buf_ref[sl, pl.ds(start, rows_per_phase, stride=CONV_PHASES), :]
            for a in range(CONV_PHASES):
                acc = None
                for j in range(DN_CONV):
                    term = w4[DN_CONV - 1 - j:DN_CONV - j] * taps[halo + a - j]
                    acc = term if acc is None else acc + term
                ybuf_ref[sl, pl.ds(a, rows_per_phase, stride=CONV_PHASES), :] = _silu(acc)
            halo_ref[slab] = buf_ref[sl, ROW_TILE:ROW_TILE + halo, :]
            y = ybuf_ref[sl]
            if col0 < DN_KEY_WIDTH:
                kh = slab
                qn = y * (lax.rsqrt(jnp.sum(y * y, axis=-1, keepdims=True) + NORM_EPS) * q_scale)
                q_ref[:, col0:col0 + LANES] = qn.astype(BF16)
                for h in range(reps * kh, reps * (kh + 1)):
                    gl = DN_V_HEADS + h
                    qe_ref[:, h * LANES:(h + 1) * LANES] = (qn * egc[:, gl:gl + 1]).astype(BF16)
            elif col0 < 2 * DN_KEY_WIDTH:
                kh = slab - DN_K_HEADS
                kn = y * lax.rsqrt(jnp.sum(y * y, axis=-1, keepdims=True) + NORM_EPS)
                k_ref[:, kh * LANES:(kh + 1) * LANES] = kn.astype(BF16)
                for h in range(reps * kh, reps * (kh + 1)):
                    gl = DN_V_HEADS + h
                    kbe_ref[:, h * LANES:(h + 1) * LANES] = (
                        kn * (bg[:, h:h + 1] * egc[:, gl:gl + 1])).astype(BF16)
            else:
                h = slab - 2 * DN_K_HEADS
                vb_ref[:, h * LANES:(h + 1) * LANES] = (y * bg[:, h:h + 1]).astype(BF16)

    for run in fillers:
        run()
    xn_ref[...] = _rms_rows(h1s_ref[...], nw_ref[...]).astype(BF16)


def _dn_in(head, x, og, w_out, norm_w, w_qkv, w_z, w_ba, conv_w, alog_row, dtb_row):
    b, seq, d = x.shape
    lp = seq + ROW_TILE
    nt = lp // ROW_TILE
    n_x = seq // ROW_TILE
    idx = np.arange(ROW_TILE)
    tri = jnp.asarray((idx[:, None] >= idx[None, :])
                      & (idx[:, None] // DN_CHUNK == idx[None, :] // DN_CHUNK), F32)
    row_block = lambda width: pl.BlockSpec((None, ROW_TILE, width), lambda bi, i: (bi, i, 0))
    full = lambda shape: pl.BlockSpec(shape, lambda bi, i: (0,) * len(shape),
                                      pipeline_mode=pl.Buffered(1))
    return pl.pallas_call(
        _dn_in_kernel,
        grid=(b, nt),
        in_specs=[
            pl.BlockSpec((None, ROW_TILE, d), lambda bi, i: (bi, 0, 0)),
            pl.BlockSpec((None, ROW_TILE, og.shape[-1]), lambda bi, i: (bi, 0, 0)),
            pl.BlockSpec((None, ROW_TILE, d), lambda bi, i: (bi, jnp.minimum(i, n_x - 1), 0)),
            pl.BlockSpec((None, ROW_TILE, og.shape[-1]), lambda bi, i: (bi, jnp.minimum(i + 1, nt - 1), 0)),
            full(w_out.shape), full((1, d)), full(w_qkv.shape), full(w_z.shape),
            full(w_ba.shape), full(conv_w.shape), full((1, LANES)), full((1, LANES)), full(tri.shape)],
        out_specs=[row_block(d), row_block(DN_KEY_WIDTH), row_block(DN_KEY_WIDTH),
                   row_block(DN_VALUE_WIDTH), row_block(DN_VALUE_WIDTH), row_block(DN_VALUE_WIDTH),
                   row_block(DN_VALUE_WIDTH), row_block(LANES)],
        out_shape=[jax.ShapeDtypeStruct((b, lp, d), F32),
                   jax.ShapeDtypeStruct((b, lp, DN_KEY_WIDTH), BF16),
                   jax.ShapeDtypeStruct((b, lp, DN_KEY_WIDTH), BF16),
                   jax.ShapeDtypeStruct((b, lp, DN_VALUE_WIDTH), BF16),
                   jax.ShapeDtypeStruct((b, lp, DN_VALUE_WIDTH), BF16),
                   jax.ShapeDtypeStruct((b, lp, DN_VALUE_WIDTH), BF16),
                   jax.ShapeDtypeStruct((b, lp, DN_VALUE_WIDTH), BF16),
                   jax.ShapeDtypeStruct((b, lp, LANES), F32)],
        scratch_shapes=[pltpu.VMEM((CONV_COLS // LANES, ROW_TILE + SUBLANES, LANES), F32),
                        pltpu.VMEM((CONV_COLS // LANES, ROW_TILE, LANES), F32),
                        pltpu.VMEM((DN_CONV_WIDTH // LANES, SUBLANES, LANES), F32),
                        pltpu.VMEM((ROW_TILE, d), F32),
                        pltpu.VMEM((ROW_TILE, d), BF16)],
        compiler_params=pltpu.CompilerParams(dimension_semantics=("parallel", "arbitrary"),
                                             vmem_limit_bytes=VMEM_LIMIT),
        name="dn_in",
    )(head, og, x, og, w_out, norm_w, w_qkv, w_z, w_ba, conv_w, alog_row, dtb_row, tri)


def _dn_core_kernel(q_ref, k_ref, vb_ref, kbe_ref, qe_ref, bg_ref, o_ref, s_ref):
    @pl.when(pl.program_id(0) == 0)
    def _():
        s_ref[...] = jnp.zeros(s_ref.shape, F32)

    n_seq = q_ref.shape[0]
    reps = DN_V_HEADS // DN_K_HEADS
    ri = lax.broadcasted_iota(jnp.int32, (DN_CHUNK, DN_CHUNK), 0)
    ci = lax.broadcasted_iota(jnp.int32, (DN_CHUNK, DN_CHUNK), 1)
    causal = ri >= ci
    neg_strict = -(ri > ci).astype(F32)
    eye = (ri == ci).astype(F32)
    keep_t = (lax.broadcasted_iota(jnp.int32, (DN_CHUNK, 2 * DN_CHUNK), 1) >= DN_CHUNK).astype(F32)

    insts = [(b, c) for c in range(DN_STEP_CHUNKS) for b in range(n_seq)]

    qk_d, kk_d, kt_d = {}, {}, {}
    for (b, c) in insts:
        rows = slice(c * DN_CHUNK, (c + 1) * DN_CHUNK)
        for kh in range(DN_K_HEADS):
            kcols = slice(kh * DN_HEAD_DIM, (kh + 1) * DN_HEAD_DIM)
            k = k_ref[b, rows, kcols]
            qk_kk = _bdot_nt(jnp.concatenate([q_ref[b, rows, kcols], k], axis=0), k)
            key = (b, c, kh)
            qk_d[key], kk_d[key] = qk_kk[:DN_CHUNK], qk_kk[DN_CHUNK:]
            kt_d[key] = k.astype(F32).T

    lhs2_d, y_d, rhs_d, qe_d, g_last_d = {}, {}, {}, {}, {}
    for (b, c) in insts:
        rows = slice(c * DN_CHUNK, (c + 1) * DN_CHUNK)
        bg = bg_ref[b, rows, :]
        bg_t = bg.T
        for h in range(DN_V_HEADS):
            key, kkey = (b, c, h), (b, c, h // reps)
            gl = DN_V_HEADS + h
            beta_col = bg[:, h:h + 1]
            gc_col = bg[:, gl:gl + 1]
            gc_row = bg_t[gl:gl + 1, :]
            decay = jnp.exp(jnp.where(causal, gc_col - gc_row, NEG_BIG))
            g_last = gc_col[DN_CHUNK - 1:DN_CHUNK, :]
            lhs2_d[key] = jnp.concatenate(
                [(qk_d[kkey] * decay).astype(BF16),
                 (kt_d[kkey] * jnp.exp(g_last - gc_row)).astype(BF16)], axis=0)
            n1 = kk_d[kkey] * decay * (neg_strict * beta_col)
            y_d[key] = jnp.concatenate([n1, eye], axis=1)
            hcols = slice(h * DN_HEAD_DIM, (h + 1) * DN_HEAD_DIM)
            rhs_d[key] = jnp.concatenate([vb_ref[b, rows, hcols], kbe_ref[b, rows, hcols]], axis=1)
            qe_d[key] = qe_ref[b, rows, hcols]
            g_last_d[key] = g_last

    chunk_heads = [(b, h) for b in range(n_seq) for h in range(DN_V_HEADS)]
    u_d, w_d, ws_qs_d = {}, {}, {}
    s_d = {(b, h): s_ref[b, h] for (b, h) in chunk_heads}

    def inverse_round(c):
        for (b, h) in chunk_heads:
            key = (b, c, h)
            r = _bdot(y_d[key][:, :DN_CHUNK], y_d[key])
            y_d[key] = r + y_d[key] * keep_t

    def solve(c):
        for (b, h) in chunk_heads:
            key = (b, c, h)
            uw = _bdot(y_d[key][:, DN_CHUNK:], rhs_d[key])
            u_d[key], w_d[key] = uw[:, :DN_HEAD_DIM], uw[:, DN_HEAD_DIM:].astype(BF16)

    def state_read(c):
        for (b, h) in chunk_heads:
            key = (b, c, h)
            ws_qs_d[key] = _bdot(jnp.concatenate([w_d[key], qe_d[key]], axis=0), s_d[(b, h)])

    def state_update(c):
        rows = slice(c * DN_CHUNK, (c + 1) * DN_CHUNK)
        for (b, h) in chunk_heads:
            key = (b, c, h)
            v_new = u_d[key] - ws_qs_d[key][:DN_CHUNK]
            av = _bdot(lhs2_d[key], v_new)
            o = ws_qs_d[key][DN_CHUNK:] + av[:DN_CHUNK]
            o_ref[b, rows, h * DN_HEAD_DIM:(h + 1) * DN_HEAD_DIM] = o.astype(BF16)
            s_d[(b, h)] = s_d[(b, h)] * jnp.exp(g_last_d[key]) + av[DN_CHUNK:]

    for c in range(DN_STEP_CHUNKS):
        between = [lambda: state_read(c - 1), lambda: state_update(c - 1)] if c > 0 else []
        for rnd in range(6):
            inverse_round(c)
            if rnd in (0, 2) and between:
                between.pop(0)()
        solve(c)
    state_read(DN_STEP_CHUNKS - 1)
    state_update(DN_STEP_CHUNKS - 1)
    for (b, h), s_new in s_d.items():
        s_ref[b, h] = s_new


def _dn_core(q, k, vb, kbe, qe, bg):
    b, lp, _ = q.shape
    rows = DN_STEP_CHUNKS * DN_CHUNK
    blk = lambda width: pl.BlockSpec((b, rows, width), lambda i: (0, i, 0))
    return pl.pallas_call(
        _dn_core_kernel,
        grid=(lp // rows,),
        in_specs=[blk(DN_KEY_WIDTH), blk(DN_KEY_WIDTH), blk(DN_VALUE_WIDTH), blk(DN_VALUE_WIDTH),
                  blk(DN_VALUE_WIDTH), blk(LANES)],
        out_specs=blk(DN_VALUE_WIDTH),
        out_shape=jax.ShapeDtypeStruct((b, lp, DN_VALUE_WIDTH), BF16),
        scratch_shapes=[pltpu.VMEM((b, DN_V_HEADS, DN_HEAD_DIM, DN_HEAD_DIM), F32)],
        compiler_params=pltpu.CompilerParams(dimension_semantics=("arbitrary",),
                                             vmem_limit_bytes=VMEM_LIMIT),
        name="dn_core",
    )(q, k, vb, kbe, qe, bg)


def _dn_out_kernel(h_ref, o_ref, z_ref, onw_ref, w_ref, out_ref):
    onw = onw_ref[...]
    acc = h_ref[...]
    heads_per_chunk = DN_OUT_K_CHUNK // DN_HEAD_DIM
    for c in range(DN_VALUE_WIDTH // DN_OUT_K_CHUNK):
        parts = []
        for hh in range(c * heads_per_chunk, (c + 1) * heads_per_chunk):
            cols = slice(hh * DN_HEAD_DIM, (hh + 1) * DN_HEAD_DIM)
            y = _rms_rows(o_ref[:, cols].astype(F32), onw) * _silu(z_ref[:, cols].astype(F32))
            parts.append(y.astype(BF16))
        acc = acc + jnp.dot(jnp.concatenate(parts, axis=1),
                            w_ref[c * DN_OUT_K_CHUNK:(c + 1) * DN_OUT_K_CHUNK, :acc.shape[1]],
                            preferred_element_type=F32)
    out_ref[...] = acc


def _dn_out(h, o, z, onw, w_out, seq):
    b, lp, d = h.shape
    skip = (lp - seq) // ROW_TILE
    in_block = lambda width: pl.BlockSpec((None, ROW_TILE, width), lambda bi, i: (bi, i + skip, 0))
    return pl.pallas_call(
        _dn_out_kernel,
        grid=(b, seq // ROW_TILE),
        in_specs=[in_block(d), in_block(DN_VALUE_WIDTH), in_block(DN_VALUE_WIDTH),
                  pl.BlockSpec((1, DN_HEAD_DIM), lambda bi, i: (0, 0)),
                  pl.BlockSpec(w_out.shape, lambda bi, i: (0, 0))],
        out_specs=pl.BlockSpec((None, ROW_TILE, d), lambda bi, i: (bi, i, 0)),
        out_shape=jax.ShapeDtypeStruct((b, seq, d), F32),
        compiler_params=pltpu.CompilerParams(dimension_semantics=("parallel", "parallel"),
                                             vmem_limit_bytes=VMEM_LIMIT),
        name="dn_out",
    )(h, o, z, onw, w_out)


def kernel(x, meta_tokens, attn_norm_w, attn_w_in, attn_q_norm_w, attn_k_norm_w, attn_sinks,
           attn_w_out, dn_norm_w, dn_w_in, dn_conv_w, dn_a_log, dn_dt_bias, dn_o_norm_w, dn_w_out):
    b, seq, d = x.shape
    assert seq % ROW_TILE == 0 and attn_norm_w.shape[0] == 1 and dn_norm_w.shape[0] == 1
    meta = jnp.broadcast_to(meta_tokens.astype(x.dtype)[None], (b, N_META, d))
    head = jnp.concatenate([jnp.zeros((b, PAD, d), x.dtype), meta], axis=1)

    q, kv, gate = _attn_in(head, x, attn_norm_w[0][None], attn_w_in[0].astype(BF16))
    og = _attention(q, kv, gate, attn_sinks[0], attn_q_norm_w[0][None], attn_k_norm_w[0][None])

    w_in = dn_w_in[0]
    w_qkv = _mxu_weight(w_in[:, :DN_CONV_WIDTH])
    w_z = _mxu_weight(w_in[:, DN_CONV_WIDTH:DN_CONV_WIDTH + DN_VALUE_WIDTH])
    n_ba = 2 * DN_V_HEADS
    w_ba = jnp.pad(w_in[:, DN_CONV_WIDTH + DN_VALUE_WIDTH:], ((0, 0), (0, LANES - n_ba))).astype(BF16)
    lane_pad = lambda t: jnp.pad(t[None], ((0, 0), (DN_V_HEADS, LANES - n_ba)))
    h1, qd, kd, vbd, kbed, qed, zd, bg = _dn_in(head, x, og, _mxu_weight(attn_w_out[0]), dn_norm_w[0][None],
                                    w_qkv, w_z, w_ba, dn_conv_w[0],
                                    lane_pad(dn_a_log[0]), lane_pad(dn_dt_bias[0]))
    od = _dn_core(qd, kd, vbd, kbed, qed, bg)
    return _dn_out(h1, od, zd, dn_o_norm_w[0][None], _mxu_weight(dn_w_out[0]), seq)
```

```python
import numpy as np
import jax
import jax.numpy as jnp
from jax import lax
from jax.experimental import pallas as pl
from jax.experimental.pallas import tpu as pltpu

F32 = jnp.float32
BF16 = jnp.bfloat16

N_META = 16
NORM_EPS = 1e-6

ATTN_HEAD_DIM = 64
ATTN_HEADS = 16
ATTN_KV_HEADS = 2
ATTN_GROUPS = ATTN_HEADS // ATTN_KV_HEADS
ATTN_WIDTH = ATTN_HEADS * ATTN_HEAD_DIM
ATTN_KV_WIDTH = ATTN_KV_HEADS * ATTN_HEAD_DIM
WINDOW = 128
ATTN_BLOCK = 128
ATTN_STEP_BLOCKS = 2

DN_HEAD_DIM = 128
DN_K_HEADS = 8
DN_V_HEADS = 16
DN_KEY_WIDTH = DN_K_HEADS * DN_HEAD_DIM
DN_VALUE_WIDTH = DN_V_HEADS * DN_HEAD_DIM
DN_CONV = 4
DN_CHUNK = 64
DN_CONV_WIDTH = 2 * DN_KEY_WIDTH + DN_VALUE_WIDTH

LANES = 128
SUBLANES = 8
ROW_TILE = 256
PAD = ROW_TILE - N_META
META_BLOCK = PAD // ATTN_BLOCK
META_OFFSET = PAD % ATTN_BLOCK
CONV_COLS = 512
FILL_COLS = 1024
CONV_PHASES = 4
DN_OUT_K_CHUNK = 512
DN_STEP_CHUNKS = 4
WEIGHT_PAD_COLS = 256
VMEM_LIMIT = 56 * 1024 * 1024

LOG2E = 1.4426950408889634
NEG_BIG = -1e30
CLIP_INVALID = 1e30


def _alibi_slopes(n_heads):
    return np.exp2(-8.0 * np.arange(1, n_heads + 1) / n_heads).astype(np.float32)


def _bdot(a, b):
    return jnp.dot(a.astype(BF16), b.astype(BF16), preferred_element_type=F32)


def _bdot_nt(a, b):
    return lax.dot_general(a.astype(BF16), b.astype(BF16), (((1,), (1,)), ((), ())),
                           preferred_element_type=F32)


def _rms_rows(x, w):
    return x * lax.rsqrt(jnp.mean(x * x, axis=-1, keepdims=True) + NORM_EPS) * w


def _silu(x):
    return x * jax.nn.sigmoid(x)


def _mxu_weight(w):
    return jnp.pad(w.astype(BF16), ((0, 0), (0, WEIGHT_PAD_COLS)))


def _attn_in_kernel(head_ref, xnext_ref, nw_ref, w_ref, q_ref, kv_ref, g_ref, xn_ref):
    @pl.when(pl.program_id(1) == 0)
    def _():
        xn_ref[...] = _rms_rows(head_ref[...], nw_ref[...]).astype(BF16)

    xn = xn_ref[...]
    q_ref[...] = jnp.dot(xn, w_ref[:, :ATTN_WIDTH], preferred_element_type=F32).astype(BF16)
    kv_ref[...] = jnp.dot(xn, w_ref[:, ATTN_WIDTH:ATTN_WIDTH + 2 * ATTN_KV_WIDTH],
                          preferred_element_type=F32).astype(BF16)
    g_ref[...] = jnp.dot(xn, w_ref[:, ATTN_WIDTH + 2 * ATTN_KV_WIDTH:],
                         preferred_element_type=F32).astype(BF16)
    xn_ref[...] = _rms_rows(xnext_ref[...], nw_ref[...]).astype(BF16)


def _attn_in(head, x, norm_w, w_in):
    b, seq, d = x.shape
    lp = seq + ROW_TILE
    n_x = seq // ROW_TILE
    n_in = w_in.shape[1]
    row_block = lambda width: pl.BlockSpec((None, ROW_TILE, width), lambda bi, i: (bi, i, 0))
    return pl.pallas_call(
        _attn_in_kernel,
        grid=(b, lp // ROW_TILE),
        in_specs=[pl.BlockSpec((None, ROW_TILE, d), lambda bi, i: (bi, 0, 0)),
                  pl.BlockSpec((None, ROW_TILE, d), lambda bi, i: (bi, jnp.minimum(i, n_x - 1), 0)),
                  pl.BlockSpec((1, d), lambda bi, i: (0, 0)),
                  pl.BlockSpec((d, n_in), lambda bi, i: (0, 0))],
        out_specs=[row_block(ATTN_WIDTH), row_block(2 * ATTN_KV_WIDTH), row_block(ATTN_WIDTH)],
        out_shape=[jax.ShapeDtypeStruct((b, lp, ATTN_WIDTH), BF16),
                   jax.ShapeDtypeStruct((b, lp, 2 * ATTN_KV_WIDTH), BF16),
                   jax.ShapeDtypeStruct((b, lp, ATTN_WIDTH), BF16)],
        scratch_shapes=[pltpu.VMEM((ROW_TILE, d), BF16)],
        compiler_params=pltpu.CompilerParams(dimension_semantics=("parallel", "arbitrary"),
                                             vmem_limit_bytes=VMEM_LIMIT),
        name="attn_in",
    )(head, x, norm_w, w_in)


def _attn_kernel(sink_ref, q_ref, kvm_ref, kvp_ref, kvc_ref, gate_ref, qnw_ref, knw_ref, o_ref):
    def clip(dist, valid):
        return jnp.where(valid, jnp.minimum(dist, WINDOW).astype(F32), CLIP_INVALID)

    jm = lax.broadcasted_iota(jnp.int32, (N_META, ATTN_BLOCK), 0)
    qm = lax.broadcasted_iota(jnp.int32, (N_META, ATTN_BLOCK), 1)
    c = lax.broadcasted_iota(jnp.int32, (ATTN_BLOCK, ATTN_BLOCK), 0)
    qi = lax.broadcasted_iota(jnp.int32, (ATTN_BLOCK, ATTN_BLOCK), 1)
    slopes = _alibi_slopes(ATTN_HEADS) * np.float32(LOG2E)
    qnw = qnw_ref[...] * (ATTN_HEAD_DIM ** -0.5 * LOG2E)
    knw = knw_ref[...]

    def kv_rows(sub, cols):
        cur = kvc_ref[sub * ATTN_BLOCK:(sub + 1) * ATTN_BLOCK, cols]
        prev = kvp_ref[:, cols] if sub == 0 else kvc_ref[(sub - 1) * ATTN_BLOCK:sub * ATTN_BLOCK, cols]
        return jnp.concatenate([kvm_ref[:, cols], prev, cur], axis=0)

    units = [(sub, h) for sub in range(ATTN_STEP_BLOCKS) for h in range(ATTN_KV_HEADS)]
    s_d, v_d, clip_d = {}, {}, {}
    for sub in range(ATTN_STEP_BLOCKS):
        nref = pl.program_id(1) * ATTN_STEP_BLOCKS + sub - META_BLOCK
        dist_m = nref * ATTN_BLOCK + qm - (META_OFFSET + jm)
        clip_d[sub] = jnp.concatenate(
            [clip(dist_m, dist_m >= 0),
             clip(ATTN_BLOCK + qi - c, jnp.logical_and(c > qi, nref >= 2)),
             clip(qi - c, jnp.logical_and(c <= qi, nref >= 1))], axis=0)
    for (sub, h) in units:
        rows = slice(sub * ATTN_BLOCK, (sub + 1) * ATTN_BLOCK)
        k_ext = kv_rows(sub, slice(h * ATTN_HEAD_DIM, (h + 1) * ATTN_HEAD_DIM))
        k_ext = _rms_rows(k_ext.astype(F32), knw).astype(BF16)
        v_d[(sub, h)] = kv_rows(sub, slice(ATTN_KV_WIDTH + h * ATTN_HEAD_DIM,
                                           ATTN_KV_WIDTH + (h + 1) * ATTN_HEAD_DIM))
        group_cols = slice(h * ATTN_GROUPS * ATTN_HEAD_DIM, (h + 1) * ATTN_GROUPS * ATTN_HEAD_DIM)
        q_t = q_ref[rows, group_cols].astype(F32).T
        qn = []
        for g in range(ATTN_GROUPS):
            qg = q_t[g * ATTN_HEAD_DIM:(g + 1) * ATTN_HEAD_DIM]
            inv = lax.rsqrt(jnp.mean(qg * qg, axis=0, keepdims=True) + NORM_EPS)
            qn.append((qg * inv * qnw).astype(BF16))
        s_d[(sub, h)] = jnp.dot(k_ext, jnp.concatenate(qn, axis=1),
                                preferred_element_type=F32)

    for (sub, h) in units:
        rows = slice(sub * ATTN_BLOCK, (sub + 1) * ATTN_BLOCK)
        p_l, rinv_l = [], []
        for g in range(ATTN_GROUPS):
            hq = h * ATTN_GROUPS + g
            s = s_d[(sub, h)][:, g * ATTN_BLOCK:(g + 1) * ATTN_BLOCK] - float(slopes[hq]) * clip_d[sub]
            sink = sink_ref[hq] * LOG2E
            m = jnp.maximum(jnp.max(s, axis=0, keepdims=True), sink)
            p = jnp.exp2(s - m)
            rinv_l.append(1.0 / (jnp.sum(p, axis=0, keepdims=True) + jnp.exp2(sink - m)))
            p_l.append(p.astype(BF16))
        o_t = lax.dot_general(v_d[(sub, h)], jnp.concatenate(p_l, axis=1), (((0,), (0,)), ((), ())),
                              preferred_element_type=F32)
        o_t = jnp.concatenate(
            [o_t[:, g * ATTN_BLOCK:(g + 1) * ATTN_BLOCK] * rinv_l[g] for g in range(ATTN_GROUPS)],
            axis=0)
        group_cols = slice(h * ATTN_GROUPS * ATTN_HEAD_DIM, (h + 1) * ATTN_GROUPS * ATTN_HEAD_DIM)
        o_ref[rows, group_cols] = (o_t.T * _silu(gate_ref[rows, group_cols].astype(F32))).astype(BF16)


def _attention(q, kv, gate, sinks, qnw, knw):
    b, lp, _ = q.shape
    step_rows = ATTN_STEP_BLOCKS * ATTN_BLOCK
    kv_width = 2 * ATTN_KV_WIDTH
    wide_block = (None, step_rows, ATTN_WIDTH)
    qnw_lanes = jnp.broadcast_to(qnw.reshape(ATTN_HEAD_DIM, 1), (ATTN_HEAD_DIM, LANES))
    return pl.pallas_call(
        _attn_kernel,
        grid=(b, lp // step_rows),
        in_specs=[pl.BlockSpec(memory_space=pltpu.SMEM),
                  pl.BlockSpec(wide_block, lambda bi, n: (bi, n, 0)),
                  pl.BlockSpec((None, N_META, kv_width), lambda bi, n: (bi, PAD // N_META, 0)),
                  pl.BlockSpec((None, ATTN_BLOCK, kv_width),
                               lambda bi, n: (bi, jnp.maximum(n * ATTN_STEP_BLOCKS - 1, 0), 0)),
                  pl.BlockSpec((None, step_rows, kv_width), lambda bi, n: (bi, n, 0)),
                  pl.BlockSpec(wide_block, lambda bi, n: (bi, n, 0)),
                  pl.BlockSpec((ATTN_HEAD_DIM, LANES), lambda bi, n: (0, 0)),
                  pl.BlockSpec((1, ATTN_HEAD_DIM), lambda bi, n: (0, 0))],
        out_specs=pl.BlockSpec(wide_block, lambda bi, n: (bi, n, 0)),
        out_shape=jax.ShapeDtypeStruct((b, lp, ATTN_WIDTH), BF16),
        compiler_params=pltpu.CompilerParams(dimension_semantics=("parallel", "parallel"),
                                             vmem_limit_bytes=VMEM_LIMIT),
        name="attn_core",
    )(sinks, q, kv, kv, kv, gate, qnw_lanes, knw)


def _dn_in_kernel(head_ref, og0_ref, xnext_ref, ognext_ref, wout_ref, nw_ref, wqkv_ref, wz_ref, wba_ref,
                  convw_ref, alog_ref, dtb_ref, tri_ref,
                  h1_ref, q_ref, k_ref, vb_ref, kbe_ref, qe_ref, z_ref, bg_ref,
                  buf_ref, ybuf_ref, halo_ref, h1s_ref, xn_ref):
    i = pl.program_id(1)
    halo = SUBLANES

    @pl.when(i == 0)
    def _():
        halo_ref[...] = jnp.zeros(halo_ref.shape, F32)
        h1s_ref[...] = head_ref[...] + jnp.dot(og0_ref[...], wout_ref[:, :h1s_ref.shape[1]],
                                               preferred_element_type=F32)
        xn_ref[...] = _rms_rows(h1s_ref[...], nw_ref[...]).astype(BF16)

    h1_ref[...] = h1s_ref[...]
    xn = xn_ref[...]

    def z_part(c0):
        def run():
            z_ref[:, c0:c0 + FILL_COLS] = jnp.dot(xn, wz_ref[:, c0:c0 + FILL_COLS],
                                                  preferred_element_type=F32).astype(BF16)
        return run

    def next_h1_part(c0):
        def run():
            h1s_ref[:, c0:c0 + FILL_COLS] = xnext_ref[:, c0:c0 + FILL_COLS] + jnp.dot(
                ognext_ref[...], wout_ref[:, c0:c0 + FILL_COLS], preferred_element_type=F32)
        return run

    fillers = ([z_part(c0) for c0 in range(0, DN_VALUE_WIDTH, FILL_COLS)]
               + [next_h1_part(c0) for c0 in range(0, h1s_ref.shape[1], FILL_COLS)])

    ba = jnp.dot(xn, wba_ref[...], preferred_element_type=F32)
    lane = lax.broadcasted_iota(jnp.int32, ba.shape, 1)
    row = i * ROW_TILE + lax.broadcasted_iota(jnp.int32, ba.shape, 0)
    x = ba + dtb_ref[...]
    softplus = jnp.maximum(x, 0.0) + jnp.log1p(jnp.exp(-jnp.abs(x)))
    g = -jnp.exp(alog_ref[...]) * softplus
    bg = jnp.where(lane < DN_V_HEADS, jax.nn.sigmoid(ba), g)
    bg = jnp.where(row >= PAD, bg, 0.0)
    csum = jnp.dot(tri_ref[...], bg, preferred_element_type=F32, precision=lax.Precision.HIGHEST)
    bg = jnp.where(lane < DN_V_HEADS, bg, csum)
    bg_ref[...] = bg
    egc = jnp.exp(bg)
    reps = DN_V_HEADS // DN_K_HEADS

    q_scale = DN_HEAD_DIM ** -0.5
    rows_per_phase = ROW_TILE // CONV_PHASES
    slabs_per_chunk = CONV_COLS // LANES
    for cj in range(DN_CONV_WIDTH // CONV_COLS):
        u = jnp.dot(xn, wqkv_ref[:, cj * CONV_COLS:(cj + 1) * CONV_COLS],
                    preferred_element_type=F32)
        if cj >= 1 and fillers:
            fillers.pop(0)()
        for sl in range(slabs_per_chunk):
            slab = cj * slabs_per_chunk + sl
            col0 = slab * LANES
            buf_ref[sl, 0:halo, :] = halo_ref[slab]
            buf_ref[sl, halo:halo + ROW_TILE, :] = u[:, sl * LANES:(sl + 1) * LANES]
            w4 = convw_ref[:, col0:col0 + LANES]
            taps = {}
            for start in range(halo - (DN_CONV - 1), halo + CONV_PHASES):
                taps[start] = buf_ref[sl, pl.ds(start, rows_per_phase, stride=CONV_PHASES), :]
            for a in range(CONV_PHASES):
                acc = None
                for j in range(DN_CONV):
                    term = w4[DN_CONV - 1 - j:DN_CONV - j] * taps[halo + a - j]
                    acc = term if acc is None else acc + term
                ybuf_ref[sl, pl.ds(a, rows_per_phase, stride=CONV_PHASES), :] = _silu(acc)
            halo_ref[slab] = buf_ref[sl, ROW_TILE:ROW_TILE + halo, :]
            y = ybuf_ref[sl]
            if col0 < DN_KEY_WIDTH:
                kh = slab
                qn = y * (lax.rsqrt(jnp.sum(y * y, axis=-1, keepdims=True) + NORM_EPS) * q_scale)
                q_ref[:, col0:col0 + LANES] = qn.astype(BF16)
                for h in range(reps * kh, reps * (kh + 1)):
                    gl = DN_V_HEADS + h
                    qe_ref[:, h * LANES:(h + 1) * LANES] = (qn * egc[:, gl:gl + 1]).astype(BF16)
            elif col0 < 2 * DN_KEY_WIDTH:
                kh = slab - DN_K_HEADS
                kn = y * lax.rsqrt(jnp.sum(y * y, axis=-1, keepdims=True) + NORM_EPS)
                k_ref[:, kh * LANES:(kh + 1) * LANES] = kn.astype(BF16)
                for h in range(reps * kh, reps * (kh + 1)):
                    gl = DN_V_HEADS + h
                    kbe_ref[:, h * LANES:(h + 1) * LANES] = (
                        kn * (bg[:, h:h + 1] * egc[:, gl:gl + 1])).astype(BF16)
            else:
                h = slab - 2 * DN_K_HEADS
                vb_ref[:, h * LANES:(h + 1) * LANES] = (y * bg[:, h:h + 1]).astype(BF16)

    for run in fillers:
        run()
    xn_ref[...] = _rms_rows(h1s_ref[...], nw_ref[...]).astype(BF16)


def _dn_in(head, x, og, w_out, norm_w, w_qkv, w_z, w_ba, conv_w, alog_row, dtb_row):
    b, seq, d = x.shape
    lp = seq + ROW_TILE
    nt = lp // ROW_TILE
    n_x = seq // ROW_TILE
    idx = np.arange(ROW_TILE)
    tri = jnp.asarray((idx[:, None] >= idx[None, :])
                      & (idx[:, None] // DN_CHUNK == idx[None, :] // DN_CHUNK), F32)
    row_block = lambda width: pl.BlockSpec((None, ROW_TILE, width), lambda bi, i: (bi, i, 0))
    full = lambda shape: pl.BlockSpec(shape, lambda bi, i: (0,) * len(shape),
                                      pipeline_mode=pl.Buffered(1))
    return pl.pallas_call(
        _dn_in_kernel,
        grid=(b, nt),
        in_specs=[
            pl.BlockSpec((None, ROW_TILE, d), lambda bi, i: (bi, 0, 0)),
            pl.BlockSpec((None, ROW_TILE, og.shape[-1]), lambda bi, i: (bi, 0, 0)),
            pl.BlockSpec((None, ROW_TILE, d), lambda bi, i: (bi, jnp.minimum(i, n_x - 1), 0)),
            pl.BlockSpec((None, ROW_TILE, og.shape[-1]), lambda bi, i: (bi, jnp.minimum(i + 1, nt - 1), 0)),
            full(w_out.shape), full((1, d)), full(w_qkv.shape), full(w_z.shape),
            full(w_ba.shape), full(conv_w.shape), full((1, LANES)), full((1, LANES)), full(tri.shape)],
        out_specs=[row_block(d), row_block(DN_KEY_WIDTH), row_block(DN_KEY_WIDTH),
                   row_block(DN_VALUE_WIDTH), row_block(DN_VALUE_WIDTH), row_block(DN_VALUE_WIDTH),
                   row_block(DN_VALUE_WIDTH), row_block(LANES)],
        out_shape=[jax.ShapeDtypeStruct((b, lp, d), F32),
                   jax.ShapeDtypeStruct((b, lp, DN_KEY_WIDTH), BF16),
                   jax.ShapeDtypeStruct((b, lp, DN_KEY_WIDTH), BF16),
                   jax.ShapeDtypeStruct((b, lp, DN_VALUE_WIDTH), BF16),
                   jax.ShapeDtypeStruct((b, lp, DN_VALUE_WIDTH), BF16),
                   jax.ShapeDtypeStruct((b, lp, DN_VALUE_WIDTH), BF16),
                   jax.ShapeDtypeStruct((b, lp, DN_VALUE_WIDTH), BF16),
                   jax.ShapeDtypeStruct((b, lp, LANES), F32)],
        scratch_shapes=[pltpu.VMEM((CONV_COLS // LANES, ROW_TILE + SUBLANES, LANES), F32),
                        pltpu.VMEM((CONV_COLS // LANES, ROW_TILE, LANES), F32),
                        pltpu.VMEM((DN_CONV_WIDTH // LANES, SUBLANES, LANES), F32),
                        pltpu.VMEM((ROW_TILE, d), F32),
                        pltpu.VMEM((ROW_TILE, d), BF16)],
        compiler_params=pltpu.CompilerParams(dimension_semantics=("parallel", "arbitrary"),
                                             vmem_limit_bytes=VMEM_LIMIT),
        name="dn_in",
    )(head, og, x, og, w_out, norm_w, w_qkv, w_z, w_ba, conv_w, alog_row, dtb_row, tri)


def _dn_core_kernel(q_ref, k_ref, vb_ref, kbe_ref, qe_ref, bg_ref, o_ref, s_ref):
    @pl.when(pl.program_id(0) == 0)
    def _():
        s_ref[...] = jnp.zeros(s_ref.shape, F32)

    n_seq = q_ref.shape[0]
    reps = DN_V_HEADS // DN_K_HEADS
    ri = lax.broadcasted_iota(jnp.int32, (DN_CHUNK, DN_CHUNK), 0)
    ci = lax.broadcasted_iota(jnp.int32, (DN_CHUNK, DN_CHUNK), 1)
    causal = ri >= ci
    neg_strict = -(ri > ci).astype(F32)
    eye = (ri == ci).astype(F32)
    keep_t = (lax.broadcasted_iota(jnp.int32, (DN_CHUNK, 2 * DN_CHUNK), 1) >= DN_CHUNK).astype(F32)

    insts = [(b, c) for c in range(DN_STEP_CHUNKS) for b in range(n_seq)]

    qk_d, kk_d, kt_d = {}, {}, {}
    for (b, c) in insts:
        rows = slice(c * DN_CHUNK, (c + 1) * DN_CHUNK)
        for kh in range(DN_K_HEADS):
            kcols = slice(kh * DN_HEAD_DIM, (kh + 1) * DN_HEAD_DIM)
            k = k_ref[b, rows, kcols]
            qk_kk = _bdot_nt(jnp.concatenate([q_ref[b, rows, kcols], k], axis=0), k)
            key = (b, c, kh)
            qk_d[key], kk_d[key] = qk_kk[:DN_CHUNK], qk_kk[DN_CHUNK:]
            kt_d[key] = k.astype(F32).T

    lhs2_d, y_d, rhs_d, qe_d, g_last_d = {}, {}, {}, {}, {}
    for (b, c) in insts:
        rows = slice(c * DN_CHUNK, (c + 1) * DN_CHUNK)
        bg = bg_ref[b, rows, :]
        bg_t = bg.T
        for h in range(DN_V_HEADS):
            key, kkey = (b, c, h), (b, c, h // reps)
            gl = DN_V_HEADS + h
            beta_col = bg[:, h:h + 1]
            gc_col = bg[:, gl:gl + 1]
            gc_row = bg_t[gl:gl + 1, :]
            decay = jnp.exp(jnp.where(causal, gc_col - gc_row, NEG_BIG))
            g_last = gc_col[DN_CHUNK - 1:DN_CHUNK, :]
            lhs2_d[key] = jnp.concatenate(
                [(qk_d[kkey] * decay).astype(BF16),
                 (kt_d[kkey] * jnp.exp(g_last - gc_row)).astype(BF16)], axis=0)
            n1 = kk_d[kkey] * decay * (neg_strict * beta_col)
            y_d[key] = jnp.concatenate([n1, eye], axis=1)
            hcols = slice(h * DN_HEAD_DIM, (h + 1) * DN_HEAD_DIM)
            rhs_d[key] = jnp.concatenate([vb_ref[b, rows, hcols], kbe_ref[b, rows, hcols]], axis=1)
            qe_d[key] = qe_ref[b, rows, hcols]
            g_last_d[key] = g_last

    chunk_heads = [(b, h) for b in range(n_seq) for h in range(DN_V_HEADS)]
    u_d, w_d, ws_qs_d = {}, {}, {}
    s_d = {(b, h): s_ref[b, h] for (b, h) in chunk_heads}

    def inverse_round(c):
        for (b, h) in chunk_heads:
            key = (b, c, h)
            r = _bdot(y_d[key][:, :DN_CHUNK], y_d[key])
            y_d[key] = r + y_d[key] * keep_t

    def solve(c):
        for (b, h) in chunk_heads:
            key = (b, c, h)
            uw = _bdot(y_d[key][:, DN_CHUNK:], rhs_d[key])
            u_d[key], w_d[key] = uw[:, :DN_HEAD_DIM], uw[:, DN_HEAD_DIM:].astype(BF16)

    def state_read(c):
        for (b, h) in chunk_heads:
            key = (b, c, h)
            ws_qs_d[key] = _bdot(jnp.concatenate([w_d[key], qe_d[key]], axis=0), s_d[(b, h)])

    def state_update(c):
        rows = slice(c * DN_CHUNK, (c + 1) * DN_CHUNK)
        for (b, h) in chunk_heads:
            key = (b, c, h)
            v_new = u_d[key] - ws_qs_d[key][:DN_CHUNK]
            av = _bdot(lhs2_d[key], v_new)
            o = ws_qs_d[key][DN_CHUNK:] + av[:DN_CHUNK]
            o_ref[b, rows, h * DN_HEAD_DIM:(h + 1) * DN_HEAD_DIM] = o.astype(BF16)
            s_d[(b, h)] = s_d[(b, h)] * jnp.exp(g_last_d[key]) + av[DN_CHUNK:]

    for c in range(DN_STEP_CHUNKS):
        between = [lambda: state_read(c - 1), lambda: state_update(c - 1)] if c > 0 else []
        for rnd in range(6):
            inverse_round(c)
            if rnd in (0, 2) and between:
                between.pop(0)()
        solve(c)
    state_read(DN_STEP_CHUNKS - 1)
    state_update(DN_STEP_CHUNKS - 1)
    for (b, h), s_new in s_d.items():
        s_ref[b, h] = s_new


def _dn_core(q, k, vb, kbe, qe, bg):
    b, lp, _ = q.shape
    rows = DN_STEP_CHUNKS * DN_CHUNK
    blk = lambda width: pl.BlockSpec((b, rows, width), lambda i: (0, i, 0))
    return pl.pallas_call(
        _dn_core_kernel,
        grid=(lp // rows,),
        in_specs=[blk(DN_KEY_WIDTH), blk(DN_KEY_WIDTH), blk(DN_VALUE_WIDTH), blk(DN_VALUE_WIDTH),
                  blk(DN_VALUE_WIDTH), blk(LANES)],
        out_specs=blk(DN_VALUE_WIDTH),
        out_shape=jax.ShapeDtypeStruct((b, lp, DN_VALUE_WIDTH), BF16),
        scratch_shapes=[pltpu.VMEM((b, DN_V_HEADS, DN_HEAD_DIM, DN_HEAD_DIM), F32)],
        compiler_params=pltpu.CompilerParams(dimension_semantics=("arbitrary",),
                                             vmem_limit_bytes=VMEM_LIMIT),
        name="dn_core",
    )(q, k, vb, kbe, qe, bg)


def _dn_out_kernel(h_ref, o_ref, z_ref, onw_ref, w_ref, out_ref):
    onw = onw_ref[...]
    acc = h_ref[...]
    heads_per_chunk = DN_OUT_K_CHUNK // DN_HEAD_DIM
    for c in range(DN_VALUE_WIDTH // DN_OUT_K_CHUNK):
        parts = []
        for hh in range(c * heads_per_chunk, (c + 1) * heads_per_chunk):
            cols = slice(hh * DN_HEAD_DIM, (hh + 1) * DN_HEAD_DIM)
            y = _rms_rows(o_ref[:, cols].astype(F32), onw) * _silu(z_ref[:, cols].astype(F32))
            parts.append(y.astype(BF16))
        acc = acc + jnp.dot(jnp.concatenate(parts, axis=1),
                            w_ref[c * DN_OUT_K_CHUNK:(c + 1) * DN_OUT_K_CHUNK, :acc.shape[1]],
                            preferred_element_type=F32)
    out_ref[...] = acc


def _dn_out(h, o, z, onw, w_out, seq):
    b, lp, d = h.shape
    skip = (lp - seq) // ROW_TILE
    in_block = lambda width: pl.BlockSpec((None, ROW_TILE, width), lambda bi, i: (bi, i + skip, 0))
    return pl.pallas_call(
        _dn_out_kernel,
        grid=(b, seq // ROW_TILE),
        in_specs=[in_block(d), in_block(DN_VALUE_WIDTH), in_block(DN_VALUE_WIDTH),
                  pl.BlockSpec((1, DN_HEAD_DIM), lambda bi, i: (0, 0)),
                  pl.BlockSpec(w_out.shape, lambda bi, i: (0, 0))],
        out_specs=pl.BlockSpec((None, ROW_TILE, d), lambda bi, i: (bi, i, 0)),
        out_shape=jax.ShapeDtypeStruct((b, seq, d), F32),
        compiler_params=pltpu.CompilerParams(dimension_semantics=("parallel", "parallel"),
                                             vmem_limit_bytes=VMEM_LIMIT),
        name="dn_out",
    )(h, o, z, onw, w_out)


def kernel(x, meta_tokens, attn_norm_w, attn_w_in, attn_q_norm_w, attn_k_norm_w, attn_sinks,
           attn_w_out, dn_norm_w, dn_w_in, dn_conv_w, dn_a_log, dn_dt_bias, dn_o_norm_w, dn_w_out):
    b, seq, d = x.shape
    assert seq % ROW_TILE == 0 and attn_norm_w.shape[0] == 1 and dn_norm_w.shape[0] == 1
    meta = jnp.broadcast_to(meta_tokens.astype(x.dtype)[None], (b, N_META, d))
    head = jnp.concatenate([jnp.zeros((b, PAD, d), x.dtype), meta], axis=1)

    q, kv, gate = _attn_in(head, x, attn_norm_w[0][None], attn_w_in[0].astype(BF16))
    og = _attention(q, kv, gate, attn_sinks[0], attn_q_norm_w[0][None], attn_k_norm_w[0][None])

    w_in = dn_w_in[0]
    w_qkv = _mxu_weight(w_in[:, :DN_CONV_WIDTH])
    w_z = _mxu_weight(w_in[:, DN_CONV_WIDTH:DN_CONV_WIDTH + DN_VALUE_WIDTH])
    n_ba = 2 * DN_V_HEADS
    w_ba = jnp.pad(w_in[:, DN_CONV_WIDTH + DN_VALUE_WIDTH:], ((0, 0), (0, LANES - n_ba))).astype(BF16)
    lane_pad = lambda t: jnp.pad(t[None], ((0, 0), (DN_V_HEADS, LANES - n_ba)))
    h1, qd, kd, vbd, kbed, qed, zd, bg = _dn_in(head, x, og, _mxu_weight(attn_w_out[0]), dn_norm_w[0][None],
                                    w_qkv, w_z, w_ba, dn_conv_w[0],
                                    lane_pad(dn_a_log[0]), lane_pad(dn_dt_bias[0]))
    od = _dn_core(qd, kd, vbd, kbed, qed, bg)
    return _dn_out(h1, od, zd, dn_o_norm_w[0][None], _mxu_weight(dn_w_out[0]), seq)
```

```python
import numpy as np
import jax
import jax.numpy as jnp
from jax import lax
from jax.experimental import pallas as pl
from jax.experimental.pallas import tpu as pltpu

F32 = jnp.float32
BF16 = jnp.bfloat16

N_META = 16
NORM_EPS = 1e-6

ATTN_HEAD_DIM = 64
ATTN_HEADS = 16
ATTN_KV_HEADS = 2
ATTN_GROUPS = ATTN_HEADS // ATTN_KV_HEADS
ATTN_WIDTH = ATTN_HEADS * ATTN_HEAD_DIM
ATTN_KV_WIDTH = ATTN_KV_HEADS * ATTN_HEAD_DIM
WINDOW = 128
ATTN_BLOCK = 128
ATTN_STEP_BLOCKS = 2

DN_HEAD_DIM = 128
DN_K_HEADS = 8
DN_V_HEADS = 16
DN_KEY_WIDTH = DN_K_HEADS * DN_HEAD_DIM
DN_VALUE_WIDTH = DN_V_HEADS * DN_HEAD_DIM
DN_CONV = 4
DN_CHUNK = 64
DN_CONV_WIDTH = 2 * DN_KEY_WIDTH + DN_VALUE_WIDTH

LANES = 128
SUBLANES = 8
ROW_TILE = 256
PAD = ROW_TILE - N_META
META_BLOCK = PAD // ATTN_BLOCK
META_OFFSET = PAD % ATTN_BLOCK
CONV_COLS = 512
FILL_COLS = 1024
CONV_PHASES = 4
DN_OUT_K_CHUNK = 512
DN_STEP_CHUNKS = 4
WEIGHT_PAD_COLS = 256
VMEM_LIMIT = 56 * 1024 * 1024

LOG2E = 1.4426950408889634
NEG_BIG = -1e30
CLIP_INVALID = 1e30


def _alibi_slopes(n_heads):
    return np.exp2(-8.0 * np.arange(1, n_heads + 1) / n_heads).astype(np.float32)


def _bdot(a, b):
    return jnp.dot(a.astype(BF16), b.astype(BF16), preferred_element_type=F32)


def _bdot_nt(a, b):
    return lax.dot_general(a.astype(BF16), b.astype(BF16), (((1,), (1,)), ((), ())),
                           preferred_element_type=F32)


def _rms_rows(x, w):
    return x * lax.rsqrt(jnp.mean(x * x, axis=-1, keepdims=True) + NORM_EPS) * w


def _silu(x):
    return x * jax.nn.sigmoid(x)


def _mxu_weight(w):
    return jnp.pad(w.astype(BF16), ((0, 0), (0, WEIGHT_PAD_COLS)))


def _layer_input(head_ref, x_ref):
    return jnp.where(pl.program_id(1) == 0, head_ref[...], x_ref[...])


def _attn_in_kernel(head_ref, x_ref, nw_ref, w_ref, q_ref, kv_ref, g_ref):
    xn = _rms_rows(_layer_input(head_ref, x_ref), nw_ref[...]).astype(BF16)
    q_ref[...] = jnp.dot(xn, w_ref[:, :ATTN_WIDTH], preferred_element_type=F32).astype(BF16)
    kv_ref[...] = jnp.dot(xn, w_ref[:, ATTN_WIDTH:ATTN_WIDTH + 2 * ATTN_KV_WIDTH],
                          preferred_element_type=F32).astype(BF16)
    g_ref[...] = jnp.dot(xn, w_ref[:, ATTN_WIDTH + 2 * ATTN_KV_WIDTH:],
                         preferred_element_type=F32).astype(BF16)


def _padded_input_specs(d):
    return [pl.BlockSpec((None, ROW_TILE, d), lambda bi, i: (bi, 0, 0)),
            pl.BlockSpec((None, ROW_TILE, d), lambda bi, i: (bi, jnp.maximum(i - 1, 0), 0))]


def _attn_in(head, x, norm_w, w_in):
    b, seq, d = x.shape
    lp = seq + ROW_TILE
    n_in = w_in.shape[1]
    row_block = lambda width: pl.BlockSpec((None, ROW_TILE, width), lambda bi, i: (bi, i, 0))
    return pl.pallas_call(
        _attn_in_kernel,
        grid=(b, lp // ROW_TILE),
        in_specs=_padded_input_specs(d) + [pl.BlockSpec((1, d), lambda bi, i: (0, 0)),
                                           pl.BlockSpec((d, n_in), lambda bi, i: (0, 0))],
        out_specs=[row_block(ATTN_WIDTH), row_block(2 * ATTN_KV_WIDTH), row_block(ATTN_WIDTH)],
        out_shape=[jax.ShapeDtypeStruct((b, lp, ATTN_WIDTH), BF16),
                   jax.ShapeDtypeStruct((b, lp, 2 * ATTN_KV_WIDTH), BF16),
                   jax.ShapeDtypeStruct((b, lp, ATTN_WIDTH), BF16)],
        compiler_params=pltpu.CompilerParams(dimension_semantics=("parallel", "parallel"),
                                             vmem_limit_bytes=VMEM_LIMIT),
        name="attn_in",
    )(head, x, norm_w, w_in)


def _attn_kernel(sink_ref, q_ref, kvm_ref, kvp_ref, kvc_ref, gate_ref, qnw_ref, knw_ref, o_ref):
    def clip(dist, valid):
        return jnp.where(valid, jnp.minimum(dist, WINDOW).astype(F32), CLIP_INVALID)

    jm = lax.broadcasted_iota(jnp.int32, (N_META, ATTN_BLOCK), 0)
    qm = lax.broadcasted_iota(jnp.int32, (N_META, ATTN_BLOCK), 1)
    c = lax.broadcasted_iota(jnp.int32, (ATTN_BLOCK, ATTN_BLOCK), 0)
    qi = lax.broadcasted_iota(jnp.int32, (ATTN_BLOCK, ATTN_BLOCK), 1)
    slopes = _alibi_slopes(ATTN_HEADS) * np.float32(LOG2E)
    qnw = qnw_ref[...] * (ATTN_HEAD_DIM ** -0.5 * LOG2E)
    knw = knw_ref[...]

    def kv_rows(sub, cols):
        cur = kvc_ref[sub * ATTN_BLOCK:(sub + 1) * ATTN_BLOCK, cols]
        prev = kvp_ref[:, cols] if sub == 0 else kvc_ref[(sub - 1) * ATTN_BLOCK:sub * ATTN_BLOCK, cols]
        return jnp.concatenate([kvm_ref[:, cols], prev, cur], axis=0)

    units = [(sub, h) for sub in range(ATTN_STEP_BLOCKS) for h in range(ATTN_KV_HEADS)]
    s_d, v_d, clip_d = {}, {}, {}
    for sub in range(ATTN_STEP_BLOCKS):
        nref = pl.program_id(1) * ATTN_STEP_BLOCKS + sub - META_BLOCK
        dist_m = nref * ATTN_BLOCK + qm - (META_OFFSET + jm)
        clip_d[sub] = jnp.concatenate(
            [clip(dist_m, dist_m >= 0),
             clip(ATTN_BLOCK + qi - c, jnp.logical_and(c > qi, nref >= 2)),
             clip(qi - c, jnp.logical_and(c <= qi, nref >= 1))], axis=0)
    for (sub, h) in units:
        rows = slice(sub * ATTN_BLOCK, (sub + 1) * ATTN_BLOCK)
        k_ext = kv_rows(sub, slice(h * ATTN_HEAD_DIM, (h + 1) * ATTN_HEAD_DIM))
        k_ext = _rms_rows(k_ext.astype(F32), knw).astype(BF16)
        v_d[(sub, h)] = kv_rows(sub, slice(ATTN_KV_WIDTH + h * ATTN_HEAD_DIM,
                                           ATTN_KV_WIDTH + (h + 1) * ATTN_HEAD_DIM))
        group_cols = slice(h * ATTN_GROUPS * ATTN_HEAD_DIM, (h + 1) * ATTN_GROUPS * ATTN_HEAD_DIM)
        q_t = q_ref[rows, group_cols].astype(F32).T
        qn = []
        for g in range(ATTN_GROUPS):
            qg = q_t[g * ATTN_HEAD_DIM:(g + 1) * ATTN_HEAD_DIM]
            inv = lax.rsqrt(jnp.mean(qg * qg, axis=0, keepdims=True) + NORM_EPS)
            qn.append((qg * inv * qnw).astype(BF16))
        s_d[(sub, h)] = jnp.dot(k_ext, jnp.concatenate(qn, axis=1),
                                preferred_element_type=F32)

    for (sub, h) in units:
        rows = slice(sub * ATTN_BLOCK, (sub + 1) * ATTN_BLOCK)
        p_l, rinv_l = [], []
        for g in range(ATTN_GROUPS):
            hq = h * ATTN_GROUPS + g
            s = s_d[(sub, h)][:, g * ATTN_BLOCK:(g + 1) * ATTN_BLOCK] - float(slopes[hq]) * clip_d[sub]
            sink = sink_ref[hq] * LOG2E
            m = jnp.maximum(jnp.max(s, axis=0, keepdims=True), sink)
            p = jnp.exp2(s - m)
            rinv_l.append(1.0 / (jnp.sum(p, axis=0, keepdims=True) + jnp.exp2(sink - m)))
            p_l.append(p.astype(BF16))
        o_t = lax.dot_general(v_d[(sub, h)], jnp.concatenate(p_l, axis=1), (((0,), (0,)), ((), ())),
                              preferred_element_type=F32)
        o_t = jnp.concatenate(
            [o_t[:, g * ATTN_BLOCK:(g + 1) * ATTN_BLOCK] * rinv_l[g] for g in range(ATTN_GROUPS)],
            axis=0)
        group_cols = slice(h * ATTN_GROUPS * ATTN_HEAD_DIM, (h + 1) * ATTN_GROUPS * ATTN_HEAD_DIM)
        o_ref[rows, group_cols] = (o_t.T * _silu(gate_ref[rows, group_cols].astype(F32))).astype(BF16)


def _attention(q, kv, gate, sinks, qnw, knw):
    b, lp, _ = q.shape
    step_rows = ATTN_STEP_BLOCKS * ATTN_BLOCK
    kv_width = 2 * ATTN_KV_WIDTH
    wide_block = (None, step_rows, ATTN_WIDTH)
    qnw_lanes = jnp.broadcast_to(qnw.reshape(ATTN_HEAD_DIM, 1), (ATTN_HEAD_DIM, LANES))
    return pl.pallas_call(
        _attn_kernel,
        grid=(b, lp // step_rows),
        in_specs=[pl.BlockSpec(memory_space=pltpu.SMEM),
                  pl.BlockSpec(wide_block, lambda bi, n: (bi, n, 0)),
                  pl.BlockSpec((None, N_META, kv_width), lambda bi, n: (bi, PAD // N_META, 0)),
                  pl.BlockSpec((None, ATTN_BLOCK, kv_width),
                               lambda bi, n: (bi, jnp.maximum(n * ATTN_STEP_BLOCKS - 1, 0), 0)),
                  pl.BlockSpec((None, step_rows, kv_width), lambda bi, n: (bi, n, 0)),
                  pl.BlockSpec(wide_block, lambda bi, n: (bi, n, 0)),
                  pl.BlockSpec((ATTN_HEAD_DIM, LANES), lambda bi, n: (0, 0)),
                  pl.BlockSpec((1, ATTN_HEAD_DIM), lambda bi, n: (0, 0))],
        out_specs=pl.BlockSpec(wide_block, lambda bi, n: (bi, n, 0)),
        out_shape=jax.ShapeDtypeStruct((b, lp, ATTN_WIDTH), BF16),
        compiler_params=pltpu.CompilerParams(dimension_semantics=("parallel", "parallel"),
                                             vmem_limit_bytes=VMEM_LIMIT),
        name="attn_core",
    )(sinks, q, kv, kv, kv, gate, qnw_lanes, knw)


def _dn_in_kernel(head_ref, og0_ref, xnext_ref, ognext_ref, wout_ref, nw_ref, wqkv_ref, wz_ref, wba_ref,
                  convw_ref, alog_ref, dtb_ref, tri_ref,
                  h1_ref, qk_ref, wide_ref, bg_ref,
                  buf_ref, ybuf_ref, halo_ref, h1s_ref, xn_ref):
    i = pl.program_id(1)
    halo = SUBLANES
    q_ref = qk_ref.at[:, 0:DN_KEY_WIDTH]
    k_ref = qk_ref.at[:, DN_KEY_WIDTH:2 * DN_KEY_WIDTH]
    vb_ref, kbe_ref, qe_ref, z_ref = (
        wide_ref.at[:, j * DN_VALUE_WIDTH:(j + 1) * DN_VALUE_WIDTH] for j in range(4))

    @pl.when(i == 0)
    def _():
        halo_ref[...] = jnp.zeros(halo_ref.shape, F32)
        h1s_ref[...] = head_ref[...] + jnp.dot(og0_ref[...], wout_ref[:, :h1s_ref.shape[1]],
                                               preferred_element_type=F32)
        xn_ref[...] = _rms_rows(h1s_ref[...], nw_ref[...]).astype(BF16)

    h1_ref[...] = h1s_ref[...]
    xn = xn_ref[...]

    def z_part(c0):
        def run():
            z_ref[:, c0:c0 + FILL_COLS] = jnp.dot(xn, wz_ref[:, c0:c0 + FILL_COLS],
                                                  preferred_element_type=F32).astype(BF16)
        return run

    def next_h1_part(c0):
        def run():
            h1s_ref[:, c0:c0 + FILL_COLS] = xnext_ref[:, c0:c0 + FILL_COLS] + jnp.dot(
                ognext_ref[...], wout_ref[:, c0:c0 + FILL_COLS], preferred_element_type=F32)
        return run

    fillers = ([z_part(c0) for c0 in range(0, DN_VALUE_WIDTH, FILL_COLS)]
               + [next_h1_part(c0) for c0 in range(0, h1s_ref.shape[1], FILL_COLS)])

    ba = jnp.dot(xn, wba_ref[...], preferred_element_type=F32)
    lane = lax.broadcasted_iota(jnp.int32, ba.shape, 1)
    row = i * ROW_TILE + lax.broadcasted_iota(jnp.int32, ba.shape, 0)
    x = ba + dtb_ref[...]
    softplus = jnp.maximum(x, 0.0) + jnp.log1p(jnp.exp(-jnp.abs(x)))
    g = -jnp.exp(alog_ref[...]) * softplus
    bg = jnp.where(lane < DN_V_HEADS, jax.nn.sigmoid(ba), g)
    bg = jnp.where(row >= PAD, bg, 0.0)
    csum = jnp.dot(tri_ref[...], bg, preferred_element_type=F32, precision=lax.Precision.HIGHEST)
    bg = jnp.where(lane < DN_V_HEADS, bg, csum)
    bg_ref[...] = bg
    egc = jnp.exp(bg)
    reps = DN_V_HEADS // DN_K_HEADS

    q_scale = DN_HEAD_DIM ** -0.5
    rows_per_phase = ROW_TILE // CONV_PHASES
    slabs_per_chunk = CONV_COLS // LANES
    for cj in range(DN_CONV_WIDTH // CONV_COLS):
        u = jnp.dot(xn, wqkv_ref[:, cj * CONV_COLS:(cj + 1) * CONV_COLS],
                    preferred_element_type=F32)
        if cj >= 1 and fillers:
            fillers.pop(0)()
        for sl in range(slabs_per_chunk):
            slab = cj * slabs_per_chunk + sl
            col0 = slab * LANES
            buf_ref[sl, 0:halo, :] = halo_ref[slab]
            buf_ref[sl, halo:halo + ROW_TILE, :] = u[:, sl * LANES:(sl + 1) * LANES]
            w4 = convw_ref[:, col0:col0 + LANES]
            taps = {}
            for start in range(halo - (DN_CONV - 1), halo + CONV_PHASES):
                taps[start] = buf_ref[sl, pl.ds(start, rows_per_phase, stride=CONV_PHASES), :]
            for a in range(CONV_PHASES):
                acc = None
                for j in range(DN_CONV):
                    term = w4[DN_CONV - 1 - j:DN_CONV - j] * taps[halo + a - j]
                    acc = term if acc is None else acc + term
                ybuf_ref[sl, pl.ds(a, rows_per_phase, stride=CONV_PHASES), :] = _silu(acc)
            halo_ref[slab] = buf_ref[sl, ROW_TILE:ROW_TILE + halo, :]
            y = ybuf_ref[sl]
            if col0 < DN_KEY_WIDTH:
                kh = slab
                qn = y * (lax.rsqrt(jnp.sum(y * y, axis=-1, keepdims=True) + NORM_EPS) * q_scale)
                q_ref[:, col0:col0 + LANES] = qn.astype(BF16)
                for h in range(reps * kh, reps * (kh + 1)):
                    gl = DN_V_HEADS + h
                    qe_ref[:, h * LANES:(h + 1) * LANES] = (qn * egc[:, gl:gl + 1]).astype(BF16)
            elif col0 < 2 * DN_KEY_WIDTH:
                kh = slab - DN_K_HEADS
                kn = y * lax.rsqrt(jnp.sum(y * y, axis=-1, keepdims=True) + NORM_EPS)
                k_ref[:, kh * LANES:(kh + 1) * LANES] = kn.astype(BF16)
                for h in range(reps * kh, reps * (kh + 1)):
                    gl = DN_V_HEADS + h
                    kbe_ref[:, h * LANES:(h + 1) * LANES] = (
                        kn * (bg[:, h:h + 1] * egc[:, gl:gl + 1])).astype(BF16)
            else:
                h = slab - 2 * DN_K_HEADS
                vb_ref[:, h * LANES:(h + 1) * LANES] = (y * bg[:, h:h + 1]).astype(BF16)

    for run in fillers:
        run()
    xn_ref[...] = _rms_rows(h1s_ref[...], nw_ref[...]).astype(BF16)


def _dn_in(head, x, og, w_out, norm_w, w_qkv, w_z, w_ba, conv_w, alog_row, dtb_row):
    b, seq, d = x.shape
    lp = seq + ROW_TILE
    nt = lp // ROW_TILE
    n_x = seq // ROW_TILE
    idx = np.arange(ROW_TILE)
    tri = jnp.asarray((idx[:, None] >= idx[None, :])
                      & (idx[:, None] // DN_CHUNK == idx[None, :] // DN_CHUNK), F32)
    row_block = lambda width: pl.BlockSpec((None, ROW_TILE, width), lambda bi, i: (bi, i, 0))
    full = lambda shape: pl.BlockSpec(shape, lambda bi, i: (0,) * len(shape),
                                      pipeline_mode=pl.Buffered(1))
    return pl.pallas_call(
        _dn_in_kernel,
        grid=(b, nt),
        in_specs=[
            pl.BlockSpec((None, ROW_TILE, d), lambda bi, i: (bi, 0, 0)),
            pl.BlockSpec((None, ROW_TILE, og.shape[-1]), lambda bi, i: (bi, 0, 0)),
            pl.BlockSpec((None, ROW_TILE, d), lambda bi, i: (bi, jnp.minimum(i, n_x - 1), 0)),
            pl.BlockSpec((None, ROW_TILE, og.shape[-1]), lambda bi, i: (bi, jnp.minimum(i + 1, nt - 1), 0)),
            full(w_out.shape), full((1, d)), full(w_qkv.shape), full(w_z.shape),
            full(w_ba.shape), full(conv_w.shape), full((1, LANES)), full((1, LANES)), full(tri.shape)],
        out_specs=[row_block(d), row_block(2 * DN_KEY_WIDTH), row_block(4 * DN_VALUE_WIDTH), row_block(LANES)],
        out_shape=[jax.ShapeDtypeStruct((b, lp, d), F32),
                   jax.ShapeDtypeStruct((b, lp, 2 * DN_KEY_WIDTH), BF16),
                   jax.ShapeDtypeStruct((b, lp, 4 * DN_VALUE_WIDTH), BF16),
                   jax.ShapeDtypeStruct((b, lp, LANES), F32)],
        scratch_shapes=[pltpu.VMEM((CONV_COLS // LANES, ROW_TILE + SUBLANES, LANES), F32),
                        pltpu.VMEM((CONV_COLS // LANES, ROW_TILE, LANES), F32),
                        pltpu.VMEM((DN_CONV_WIDTH // LANES, SUBLANES, LANES), F32),
                        pltpu.VMEM((ROW_TILE, d), F32),
                        pltpu.VMEM((ROW_TILE, d), BF16)],
        compiler_params=pltpu.CompilerParams(dimension_semantics=("parallel", "arbitrary"),
                                             vmem_limit_bytes=VMEM_LIMIT),
        name="dn_in",
    )(head, og, x, og, w_out, norm_w, w_qkv, w_z, w_ba, conv_w, alog_row, dtb_row, tri)


def _dn_core_kernel(q_ref, k_ref, vb_ref, kbe_ref, qe_ref, bg_ref, o_ref, s_ref):
    @pl.when(pl.program_id(0) == 0)
    def _():
        s_ref[...] = jnp.zeros(s_ref.shape, F32)

    n_seq = q_ref.shape[0]
    reps = DN_V_HEADS // DN_K_HEADS
    ri = lax.broadcasted_iota(jnp.int32, (DN_CHUNK, DN_CHUNK), 0)
    ci = lax.broadcasted_iota(jnp.int32, (DN_CHUNK, DN_CHUNK), 1)
    causal = ri >= ci
    neg_strict = -(ri > ci).astype(F32)
    eye = (ri == ci).astype(F32)
    keep_t = (lax.broadcasted_iota(jnp.int32, (DN_CHUNK, 2 * DN_CHUNK), 1) >= DN_CHUNK).astype(F32)

    insts = [(b, c) for c in range(DN_STEP_CHUNKS) for b in range(n_seq)]

    qk_d, kk_d, kt_d = {}, {}, {}
    for (b, c) in insts:
        rows = slice(c * DN_CHUNK, (c + 1) * DN_CHUNK)
        for kh in range(DN_K_HEADS):
            kcols = slice(kh * DN_HEAD_DIM, (kh + 1) * DN_HEAD_DIM)
            k = k_ref[b, rows, kcols]
            qk_kk = _bdot_nt(jnp.concatenate([q_ref[b, rows, kcols], k], axis=0), k)
            key = (b, c, kh)
            qk_d[key], kk_d[key] = qk_kk[:DN_CHUNK], qk_kk[DN_CHUNK:]
            kt_d[key] = k.astype(F32).T

    lhs2_d, y_d, rhs_d, qe_d, g_last_d = {}, {}, {}, {}, {}
    for (b, c) in insts:
        rows = slice(c * DN_CHUNK, (c + 1) * DN_CHUNK)
        bg = bg_ref[b, rows, :]
        bg_t = bg.T
        for h in range(DN_V_HEADS):
            key, kkey = (b, c, h), (b, c, h // reps)
            gl = DN_V_HEADS + h
            beta_col = bg[:, h:h + 1]
            gc_col = bg[:, gl:gl + 1]
            gc_row = bg_t[gl:gl + 1, :]
            decay = jnp.exp(jnp.where(causal, gc_col - gc_row, NEG_BIG))
            g_last = gc_col[DN_CHUNK - 1:DN_CHUNK, :]
            lhs2_d[key] = jnp.concatenate(
                [(qk_d[kkey] * decay).astype(BF16),
                 (kt_d[kkey] * jnp.exp(g_last - gc_row)).astype(BF16)], axis=0)
            n1 = kk_d[kkey] * decay * (neg_strict * beta_col)
            y_d[key] = jnp.concatenate([n1, eye], axis=1)
            hcols = slice(h * DN_HEAD_DIM, (h + 1) * DN_HEAD_DIM)
            rhs_d[key] = jnp.concatenate([vb_ref[b, rows, hcols], kbe_ref[b, rows, hcols]], axis=1)
            qe_d[key] = qe_ref[b, rows, hcols]
            g_last_d[key] = g_last

    chunk_heads = [(b, h) for b in range(n_seq) for h in range(DN_V_HEADS)]
    u_d, w_d, ws_qs_d = {}, {}, {}
    s_d = {(b, h): s_ref[b, h] for (b, h) in chunk_heads}

    def inverse_round(c):
        for (b, h) in chunk_heads:
            key = (b, c, h)
            r = _bdot(y_d[key][:, :DN_CHUNK], y_d[key])
            y_d[key] = r + y_d[key] * keep_t

    def solve(c):
        for (b, h) in chunk_heads:
            key = (b, c, h)
            uw = _bdot(y_d[key][:, DN_CHUNK:], rhs_d[key])
            u_d[key], w_d[key] = uw[:, :DN_HEAD_DIM], uw[:, DN_HEAD_DIM:].astype(BF16)

    def state_read(c):
        for (b, h) in chunk_heads:
            key = (b, c, h)
            ws_qs_d[key] = _bdot(jnp.concatenate([w_d[key], qe_d[key]], axis=0), s_d[(b, h)])

    def state_update(c):
        rows = slice(c * DN_CHUNK, (c + 1) * DN_CHUNK)
        for (b, h) in chunk_heads:
            key = (b, c, h)
            v_new = u_d[key] - ws_qs_d[key][:DN_CHUNK]
            av = _bdot(lhs2_d[key], v_new)
            o = ws_qs_d[key][DN_CHUNK:] + av[:DN_CHUNK]
            o_ref[b, rows, h * DN_HEAD_DIM:(h + 1) * DN_HEAD_DIM] = o.astype(BF16)
            s_d[(b, h)] = s_d[(b, h)] * jnp.exp(g_last_d[key]) + av[DN_CHUNK:]

    for c in range(DN_STEP_CHUNKS):
        between = [lambda: state_read(c - 1), lambda: state_update(c - 1)] if c > 0 else []
        for rnd in range(6):
            inverse_round(c)
            if rnd in (0, 2) and between:
                between.pop(0)()
        solve(c)
    state_read(DN_STEP_CHUNKS - 1)
    state_update(DN_STEP_CHUNKS - 1)
    for (b, h), s_new in s_d.items():
        s_ref[b, h] = s_new


def _dn_core(qk, wide, bg):
    b, lp, _ = qk.shape
    rows = DN_STEP_CHUNKS * DN_CHUNK
    blk = lambda width, j=0: pl.BlockSpec((b, rows, width), lambda i: (0, i, j))
    return pl.pallas_call(
        _dn_core_kernel,
        grid=(lp // rows,),
        in_specs=[blk(DN_KEY_WIDTH, 0), blk(DN_KEY_WIDTH, 1), blk(DN_VALUE_WIDTH, 0), blk(DN_VALUE_WIDTH, 1),
                  blk(DN_VALUE_WIDTH, 2), blk(LANES)],
        out_specs=blk(DN_VALUE_WIDTH),
        out_shape=jax.ShapeDtypeStruct((b, lp, DN_VALUE_WIDTH), BF16),
        scratch_shapes=[pltpu.VMEM((b, DN_V_HEADS, DN_HEAD_DIM, DN_HEAD_DIM), F32)],
        compiler_params=pltpu.CompilerParams(dimension_semantics=("arbitrary",),
                                             vmem_limit_bytes=VMEM_LIMIT),
        name="dn_core",
    )(qk, qk, wide, wide, wide, bg)


def _dn_out_kernel(h_ref, o_ref, z_ref, onw_ref, w_ref, out_ref):
    onw = onw_ref[...]
    acc = h_ref[...]
    heads_per_chunk = DN_OUT_K_CHUNK // DN_HEAD_DIM
    for c in range(DN_VALUE_WIDTH // DN_OUT_K_CHUNK):
        parts = []
        for hh in range(c * heads_per_chunk, (c + 1) * heads_per_chunk):
            cols = slice(hh * DN_HEAD_DIM, (hh + 1) * DN_HEAD_DIM)
            y = _rms_rows(o_ref[:, cols].astype(F32), onw) * _silu(z_ref[:, cols].astype(F32))
            parts.append(y.astype(BF16))
        acc = acc + jnp.dot(jnp.concatenate(parts, axis=1),
                            w_ref[c * DN_OUT_K_CHUNK:(c + 1) * DN_OUT_K_CHUNK, :acc.shape[1]],
                            preferred_element_type=F32)
    out_ref[...] = acc


def _dn_out(h, o, wide, onw, w_out, seq):
    b, lp, d = h.shape
    skip = (lp - seq) // ROW_TILE
    in_block = lambda width, j=0: pl.BlockSpec((None, ROW_TILE, width), lambda bi, i: (bi, i + skip, j))
    return pl.pallas_call(
        _dn_out_kernel,
        grid=(b, seq // ROW_TILE),
        in_specs=[in_block(d), in_block(DN_VALUE_WIDTH), in_block(DN_VALUE_WIDTH, 3),
                  pl.BlockSpec((1, DN_HEAD_DIM), lambda bi, i: (0, 0)),
                  pl.BlockSpec(w_out.shape, lambda bi, i: (0, 0))],
        out_specs=pl.BlockSpec((None, ROW_TILE, d), lambda bi, i: (bi, i, 0)),
        out_shape=jax.ShapeDtypeStruct((b, seq, d), F32),
        compiler_params=pltpu.CompilerParams(dimension_semantics=("parallel", "parallel"),
                                             vmem_limit_bytes=VMEM_LIMIT),
        name="dn_out",
    )(h, o, wide, onw, w_out)


def kernel(x, meta_tokens, attn_norm_w, attn_w_in, attn_q_norm_w, attn_k_norm_w, attn_sinks,
           attn_w_out, dn_norm_w, dn_w_in, dn_conv_w, dn_a_log, dn_dt_bias, dn_o_norm_w, dn_w_out):
    b, seq, d = x.shape
    assert seq % ROW_TILE == 0 and attn_norm_w.shape[0] == 1 and dn_norm_w.shape[0] == 1
    meta = jnp.broadcast_to(meta_tokens.astype(x.dtype)[None], (b, N_META, d))
    head = jnp.concatenate([jnp.zeros((b, PAD, d), x.dtype), meta], axis=1)

    q, kv, gate = _attn_in(head, x, attn_norm_w[0][None], attn_w_in[0].astype(BF16))
    og = _attention(q, kv, gate, attn_sinks[0], attn_q_norm_w[0][None], attn_k_norm_w[0][None])

    w_in = dn_w_in[0]
    w_qkv = _mxu_weight(w_in[:, :DN_CONV_WIDTH])
    w_z = _mxu_weight(w_in[:, DN_CONV_WIDTH:DN_CONV_WIDTH + DN_VALUE_WIDTH])
    n_ba = 2 * DN_V_HEADS
    w_ba = jnp.pad(w_in[:, DN_CONV_WIDTH + DN_VALUE_WIDTH:], ((0, 0), (0, LANES - n_ba))).astype(BF16)
    lane_pad = lambda t: jnp.pad(t[None], ((0, 0), (DN_V_HEADS, LANES - n_ba)))
    h1, qk, wide, bg = _dn_in(head, x, og, _mxu_weight(attn_w_out[0]), dn_norm_w[0][None],
                                    w_qkv, w_z, w_ba, dn_conv_w[0],
                                    lane_pad(dn_a_log[0]), lane_pad(dn_dt_bias[0]))
    od = _dn_core(qk, wide, bg)
    return _dn_out(h1, od, wide, dn_o_norm_w[0][None], _mxu_weight(dn_w_out[0]), seq)
```

```python
import numpy as np
import jax
import jax.numpy as jnp
from jax import lax
from jax.experimental import pallas as pl
from jax.experimental.pallas import tpu as pltpu

F32 = jnp.float32
BF16 = jnp.bfloat16

N_META = 16
NORM_EPS = 1e-6

ATTN_HEAD_DIM = 64
ATTN_HEADS = 16
ATTN_KV_HEADS = 2
ATTN_GROUPS = ATTN_HEADS // ATTN_KV_HEADS
ATTN_WIDTH = ATTN_HEADS * ATTN_HEAD_DIM
ATTN_KV_WIDTH = ATTN_KV_HEADS * ATTN_HEAD_DIM
WINDOW = 128
ATTN_BLOCK = 128
ATTN_STEP_BLOCKS = 5

DN_HEAD_DIM = 128
DN_K_HEADS = 8
DN_V_HEADS = 16
DN_KEY_WIDTH = DN_K_HEADS * DN_HEAD_DIM
DN_VALUE_WIDTH = DN_V_HEADS * DN_HEAD_DIM
DN_CONV = 4
DN_CHUNK = 64
DN_CONV_WIDTH = 2 * DN_KEY_WIDTH + DN_VALUE_WIDTH

LANES = 128
SUBLANES = 8
ROW_TILE = 256
PAD = ROW_TILE - N_META
META_BLOCK = PAD // ATTN_BLOCK
META_OFFSET = PAD % ATTN_BLOCK
CONV_COLS = 512
FILL_COLS = 1024
CONV_PHASES = 4
DN_OUT_K_CHUNK = 512
DN_STEP_CHUNKS = 4
WEIGHT_PAD_COLS = 256
VMEM_LIMIT = 56 * 1024 * 1024

LOG2E = 1.4426950408889634
NEG_BIG = -1e30
CLIP_INVALID = 1e30


def _alibi_slopes(n_heads):
    return np.exp2(-8.0 * np.arange(1, n_heads + 1) / n_heads).astype(np.float32)


def _bdot(a, b):
    return jnp.dot(a.astype(BF16), b.astype(BF16), preferred_element_type=F32)


def _bdot_nt(a, b):
    return lax.dot_general(a.astype(BF16), b.astype(BF16), (((1,), (1,)), ((), ())),
                           preferred_element_type=F32)


def _rms_rows(x, w):
    return x * lax.rsqrt(jnp.mean(x * x, axis=-1, keepdims=True) + NORM_EPS) * w


def _silu(x):
    return x * jax.nn.sigmoid(x)


def _mxu_weight(w):
    return jnp.pad(w.astype(BF16), ((0, 0), (0, WEIGHT_PAD_COLS)))


def _layer_input(head_ref, x_ref):
    return jnp.where(pl.program_id(1) == 0, head_ref[...], x_ref[...])


def _attn_in_kernel(head_ref, x_ref, nw_ref, w_ref, q_ref, kv_ref, g_ref):
    xn = _rms_rows(_layer_input(head_ref, x_ref), nw_ref[...]).astype(BF16)
    q_ref[...] = jnp.dot(xn, w_ref[:, :ATTN_WIDTH], preferred_element_type=F32).astype(BF16)
    kv_ref[...] = jnp.dot(xn, w_ref[:, ATTN_WIDTH:ATTN_WIDTH + 2 * ATTN_KV_WIDTH],
                          preferred_element_type=F32).astype(BF16)
    g_ref[...] = jnp.dot(xn, w_ref[:, ATTN_WIDTH + 2 * ATTN_KV_WIDTH:],
                         preferred_element_type=F32).astype(BF16)


def _padded_input_specs(d):
    return [pl.BlockSpec((None, ROW_TILE, d), lambda bi, i: (bi, 0, 0)),
            pl.BlockSpec((None, ROW_TILE, d), lambda bi, i: (bi, jnp.maximum(i - 1, 0), 0))]


def _attn_in(head, x, norm_w, w_in):
    b, seq, d = x.shape
    lp = seq + ROW_TILE
    n_in = w_in.shape[1]
    row_block = lambda width: pl.BlockSpec((None, ROW_TILE, width), lambda bi, i: (bi, i, 0))
    return pl.pallas_call(
        _attn_in_kernel,
        grid=(b, lp // ROW_TILE),
        in_specs=_padded_input_specs(d) + [pl.BlockSpec((1, d), lambda bi, i: (0, 0)),
                                           pl.BlockSpec((d, n_in), lambda bi, i: (0, 0))],
        out_specs=[row_block(ATTN_WIDTH), row_block(2 * ATTN_KV_WIDTH), row_block(ATTN_WIDTH)],
        out_shape=[jax.ShapeDtypeStruct((b, lp, ATTN_WIDTH), BF16),
                   jax.ShapeDtypeStruct((b, lp, 2 * ATTN_KV_WIDTH), BF16),
                   jax.ShapeDtypeStruct((b, lp, ATTN_WIDTH), BF16)],
        compiler_params=pltpu.CompilerParams(dimension_semantics=("parallel", "parallel"),
                                             vmem_limit_bytes=VMEM_LIMIT),
        name="attn_in",
    )(head, x, norm_w, w_in)


def _attn_kernel(sink_ref, q_ref, kvm_ref, kvp_ref, kvc_ref, gate_ref, qnw_ref, knw_ref, o_ref):
    def clip(dist, valid):
        return jnp.where(valid, jnp.minimum(dist, WINDOW).astype(F32), CLIP_INVALID)

    jm = lax.broadcasted_iota(jnp.int32, (N_META, ATTN_BLOCK), 0)
    qm = lax.broadcasted_iota(jnp.int32, (N_META, ATTN_BLOCK), 1)
    c = lax.broadcasted_iota(jnp.int32, (ATTN_BLOCK, ATTN_BLOCK), 0)
    qi = lax.broadcasted_iota(jnp.int32, (ATTN_BLOCK, ATTN_BLOCK), 1)
    slopes = _alibi_slopes(ATTN_HEADS) * np.float32(LOG2E)
    qnw = qnw_ref[...] * (ATTN_HEAD_DIM ** -0.5 * LOG2E)
    knw = knw_ref[...]

    def kv_rows(sub, cols):
        cur = kvc_ref[sub * ATTN_BLOCK:(sub + 1) * ATTN_BLOCK, cols]
        prev = kvp_ref[:, cols] if sub == 0 else kvc_ref[(sub - 1) * ATTN_BLOCK:sub * ATTN_BLOCK, cols]
        return jnp.concatenate([kvm_ref[:, cols], prev, cur], axis=0)

    units = [(sub, h) for sub in range(ATTN_STEP_BLOCKS) for h in range(ATTN_KV_HEADS)]
    s_d, v_d, clip_d = {}, {}, {}
    for sub in range(ATTN_STEP_BLOCKS):
        nref = pl.program_id(1) * ATTN_STEP_BLOCKS + sub - META_BLOCK
        dist_m = nref * ATTN_BLOCK + qm - (META_OFFSET + jm)
        clip_d[sub] = jnp.concatenate(
            [clip(dist_m, dist_m >= 0),
             clip(ATTN_BLOCK + qi - c, jnp.logical_and(c > qi, nref >= 2)),
             clip(qi - c, jnp.logical_and(c <= qi, nref >= 1))], axis=0)
    for (sub, h) in units:
        rows = slice(sub * ATTN_BLOCK, (sub + 1) * ATTN_BLOCK)
        k_ext = kv_rows(sub, slice(h * ATTN_HEAD_DIM, (h + 1) * ATTN_HEAD_DIM))
        k_ext = _rms_rows(k_ext.astype(F32), knw).astype(BF16)
        v_d[(sub, h)] = kv_rows(sub, slice(ATTN_KV_WIDTH + h * ATTN_HEAD_DIM,
                                           ATTN_KV_WIDTH + (h + 1) * ATTN_HEAD_DIM))
        group_cols = slice(h * ATTN_GROUPS * ATTN_HEAD_DIM, (h + 1) * ATTN_GROUPS * ATTN_HEAD_DIM)
        q_t = q_ref[rows, group_cols].astype(F32).T
        qn = []
        for g in range(ATTN_GROUPS):
            qg = q_t[g * ATTN_HEAD_DIM:(g + 1) * ATTN_HEAD_DIM]
            inv = lax.rsqrt(jnp.mean(qg * qg, axis=0, keepdims=True) + NORM_EPS)
            qn.append((qg * inv * qnw).astype(BF16))
        s_d[(sub, h)] = jnp.dot(k_ext, jnp.concatenate(qn, axis=1),
                                preferred_element_type=F32)

    for (sub, h) in units:
        rows = slice(sub * ATTN_BLOCK, (sub + 1) * ATTN_BLOCK)
        p_l, rinv_l = [], []
        for g in range(ATTN_GROUPS):
            hq = h * ATTN_GROUPS + g
            s = s_d[(sub, h)][:, g * ATTN_BLOCK:(g + 1) * ATTN_BLOCK] - float(slopes[hq]) * clip_d[sub]
            sink = sink_ref[hq] * LOG2E
            m = jnp.maximum(jnp.max(s, axis=0, keepdims=True), sink)
            p = jnp.exp2(s - m)
            rinv_l.append(1.0 / (jnp.sum(p, axis=0, keepdims=True) + jnp.exp2(sink - m)))
            p_l.append(p.astype(BF16))
        o_t = lax.dot_general(v_d[(sub, h)], jnp.concatenate(p_l, axis=1), (((0,), (0,)), ((), ())),
                              preferred_element_type=F32)
        o_t = jnp.concatenate(
            [o_t[:, g * ATTN_BLOCK:(g + 1) * ATTN_BLOCK] * rinv_l[g] for g in range(ATTN_GROUPS)],
            axis=0)
        group_cols = slice(h * ATTN_GROUPS * ATTN_HEAD_DIM, (h + 1) * ATTN_GROUPS * ATTN_HEAD_DIM)
        o_ref[rows, group_cols] = (o_t.T * _silu(gate_ref[rows, group_cols].astype(F32))).astype(BF16)


def _attention(q, kv, gate, sinks, qnw, knw):
    b, lp, _ = q.shape
    step_rows = ATTN_STEP_BLOCKS * ATTN_BLOCK
    kv_width = 2 * ATTN_KV_WIDTH
    wide_block = (None, step_rows, ATTN_WIDTH)
    qnw_lanes = jnp.broadcast_to(qnw.reshape(ATTN_HEAD_DIM, 1), (ATTN_HEAD_DIM, LANES))
    return pl.pallas_call(
        _attn_kernel,
        grid=(b, lp // step_rows),
        in_specs=[pl.BlockSpec(memory_space=pltpu.SMEM),
                  pl.BlockSpec(wide_block, lambda bi, n: (bi, n, 0)),
                  pl.BlockSpec((None, N_META, kv_width), lambda bi, n: (bi, PAD // N_META, 0)),
                  pl.BlockSpec((None, ATTN_BLOCK, kv_width),
                               lambda bi, n: (bi, jnp.maximum(n * ATTN_STEP_BLOCKS - 1, 0), 0)),
                  pl.BlockSpec((None, step_rows, kv_width), lambda bi, n: (bi, n, 0)),
                  pl.BlockSpec(wide_block, lambda bi, n: (bi, n, 0)),
                  pl.BlockSpec((ATTN_HEAD_DIM, LANES), lambda bi, n: (0, 0)),
                  pl.BlockSpec((1, ATTN_HEAD_DIM), lambda bi, n: (0, 0))],
        out_specs=pl.BlockSpec(wide_block, lambda bi, n: (bi, n, 0)),
        out_shape=jax.ShapeDtypeStruct((b, lp, ATTN_WIDTH), BF16),
        compiler_params=pltpu.CompilerParams(dimension_semantics=("parallel", "parallel"),
                                             vmem_limit_bytes=VMEM_LIMIT),
        name="attn_core",
    )(sinks, q, kv, kv, kv, gate, qnw_lanes, knw)


def _dn_in_kernel(head_ref, og0_ref, xnext_ref, ognext_ref, wout_ref, nw_ref, wqkv_ref, wz_ref, wba_ref,
                  convw_ref, alog_ref, dtb_ref, tri_ref,
                  h1_ref, q_ref, k_ref, vb_ref, kbe_ref, qe_ref, z_ref, bg_ref,
                  buf_ref, ybuf_ref, halo_ref, h1s_ref, xn_ref):
    i = pl.program_id(1)
    halo = SUBLANES

    @pl.when(i == 0)
    def _():
        halo_ref[...] = jnp.zeros(halo_ref.shape, F32)
        h1s_ref[...] = head_ref[...] + jnp.dot(og0_ref[...], wout_ref[:, :h1s_ref.shape[1]],
                                               preferred_element_type=F32)
        xn_ref[...] = _rms_rows(h1s_ref[...], nw_ref[...]).astype(BF16)

    h1_ref[...] = h1s_ref[...]
    xn = xn_ref[...]

    def z_part(c0):
        def run():
            z_ref[:, c0:c0 + FILL_COLS] = jnp.dot(xn, wz_ref[:, c0:c0 + FILL_COLS],
                                                  preferred_element_type=F32).astype(BF16)
        return run

    def next_h1_part(c0):
        def run():
            h1s_ref[:, c0:c0 + FILL_COLS] = xnext_ref[:, c0:c0 + FILL_COLS] + jnp.dot(
                ognext_ref[...], wout_ref[:, c0:c0 + FILL_COLS], preferred_element_type=F32)
        return run

    fillers = ([z_part(c0) for c0 in range(0, DN_VALUE_WIDTH, FILL_COLS)]
               + [next_h1_part(c0) for c0 in range(0, h1s_ref.shape[1], FILL_COLS)])

    ba = jnp.dot(xn, wba_ref[...], preferred_element_type=F32)
    lane = lax.broadcasted_iota(jnp.int32, ba.shape, 1)
    row = i * ROW_TILE + lax.broadcasted_iota(jnp.int32, ba.shape, 0)
    x = ba + dtb_ref[...]
    softplus = jnp.maximum(x, 0.0) + jnp.log1p(jnp.exp(-jnp.abs(x)))
    g = -jnp.exp(alog_ref[...]) * softplus
    bg = jnp.where(lane < DN_V_HEADS, jax.nn.sigmoid(ba), g)
    bg = jnp.where(row >= PAD, bg, 0.0)
    csum = jnp.dot(tri_ref[...], bg, preferred_element_type=F32, precision=lax.Precision.HIGHEST)
    bg = jnp.where(lane < DN_V_HEADS, bg, csum)
    bg_ref[...] = bg
    egc = jnp.exp(bg)
    reps = DN_V_HEADS // DN_K_HEADS

    q_scale = DN_HEAD_DIM ** -0.5
    rows_per_phase = ROW_TILE // CONV_PHASES
    slabs_per_chunk = CONV_COLS // LANES
    for cj in range(DN_CONV_WIDTH // CONV_COLS):
        u = jnp.dot(xn, wqkv_ref[:, cj * CONV_COLS:(cj + 1) * CONV_COLS],
                    preferred_element_type=F32)
        if cj >= 1 and fillers:
            fillers.pop(0)()
        for sl in range(slabs_per_chunk):
            slab = cj * slabs_per_chunk + sl
            col0 = slab * LANES
            buf_ref[sl, 0:halo, :] = halo_ref[slab]
            buf_ref[sl, halo:halo + ROW_TILE, :] = u[:, sl * LANES:(sl + 1) * LANES]
            w4 = convw_ref[:, col0:col0 + LANES]
            taps = {}
            for start in range(halo - (DN_CONV - 1), halo + CONV_PHASES):
                taps[start] = buf_ref[sl, pl.ds(start, rows_per_phase, stride=CONV_PHASES), :]
            for a in range(CONV_PHASES):
                acc = None
                for j in range(DN_CONV):
                    term = w4[DN_CONV - 1 - j:DN_CONV - j] * taps[halo + a - j]
                    acc = term if acc is None else acc + term
                ybuf_ref[sl, pl.ds(a, rows_per_phase, stride=CONV_PHASES), :] = _silu(acc)
            halo_ref[slab] = buf_ref[sl, ROW_TILE:ROW_TILE + halo, :]
            y = ybuf_ref[sl]
            if col0 < DN_KEY_WIDTH:
                kh = slab
                qn = y * (lax.rsqrt(jnp.sum(y * y, axis=-1, keepdims=True) + NORM_EPS) * q_scale)
                q_ref[:, col0:col0 + LANES] = qn.astype(BF16)
                for h in range(reps * kh, reps * (kh + 1)):
                    gl = DN_V_HEADS + h
                    qe_ref[:, h * LANES:(h + 1) * LANES] = (qn * egc[:, gl:gl + 1]).astype(BF16)
            elif col0 < 2 * DN_KEY_WIDTH:
                kh = slab - DN_K_HEADS
                kn = y * lax.rsqrt(jnp.sum(y * y, axis=-1, keepdims=True) + NORM_EPS)
                k_ref[:, kh * LANES:(kh + 1) * LANES] = kn.astype(BF16)
                for h in range(reps * kh, reps * (kh + 1)):
                    gl = DN_V_HEADS + h
                    kbe_ref[:, h * LANES:(h + 1) * LANES] = (
                        kn * (bg[:, h:h + 1] * egc[:, gl:gl + 1])).astype(BF16)
            else:
                h = slab - 2 * DN_K_HEADS
                vb_ref[:, h * LANES:(h + 1) * LANES] = (y * bg[:, h:h + 1]).astype(BF16)

    for run in fillers:
        run()
    xn_ref[...] = _rms_rows(h1s_ref[...], nw_ref[...]).astype(BF16)


def _dn_in(head, x, og, w_out, norm_w, w_qkv, w_z, w_ba, conv_w, alog_row, dtb_row):
    b, seq, d = x.shape
    lp = seq + ROW_TILE
    nt = lp // ROW_TILE
    n_x = seq // ROW_TILE
    idx = np.arange(ROW_TILE)
    tri = jnp.asarray((idx[:, None] >= idx[None, :])
                      & (idx[:, None] // DN_CHUNK == idx[None, :] // DN_CHUNK), F32)
    row_block = lambda width: pl.BlockSpec((None, ROW_TILE, width), lambda bi, i: (bi, i, 0))
    full = lambda shape: pl.BlockSpec(shape, lambda bi, i: (0,) * len(shape),
                                      pipeline_mode=pl.Buffered(1))
    return pl.pallas_call(
        _dn_in_kernel,
        grid=(b, nt),
        in_specs=[
            pl.BlockSpec((None, ROW_TILE, d), lambda bi, i: (bi, 0, 0)),
            pl.BlockSpec((None, ROW_TILE, og.shape[-1]), lambda bi, i: (bi, 0, 0)),
            pl.BlockSpec((None, ROW_TILE, d), lambda bi, i: (bi, jnp.minimum(i, n_x - 1), 0)),
            pl.BlockSpec((None, ROW_TILE, og.shape[-1]), lambda bi, i: (bi, jnp.minimum(i + 1, nt - 1), 0)),
            full(w_out.shape), full((1, d)), full(w_qkv.shape), full(w_z.shape),
            full(w_ba.shape), full(conv_w.shape), full((1, LANES)), full((1, LANES)), full(tri.shape)],
        out_specs=[row_block(d), row_block(DN_KEY_WIDTH), row_block(DN_KEY_WIDTH),
                   row_block(DN_VALUE_WIDTH), row_block(DN_VALUE_WIDTH), row_block(DN_VALUE_WIDTH),
                   row_block(DN_VALUE_WIDTH), row_block(LANES)],
        out_shape=[jax.ShapeDtypeStruct((b, lp, d), F32),
                   jax.ShapeDtypeStruct((b, lp, DN_KEY_WIDTH), BF16),
                   jax.ShapeDtypeStruct((b, lp, DN_KEY_WIDTH), BF16),
                   jax.ShapeDtypeStruct((b, lp, DN_VALUE_WIDTH), BF16),
                   jax.ShapeDtypeStruct((b, lp, DN_VALUE_WIDTH), BF16),
                   jax.ShapeDtypeStruct((b, lp, DN_VALUE_WIDTH), BF16),
                   jax.ShapeDtypeStruct((b, lp, DN_VALUE_WIDTH), BF16),
                   jax.ShapeDtypeStruct((b, lp, LANES), F32)],
        scratch_shapes=[pltpu.VMEM((CONV_COLS // LANES, ROW_TILE + SUBLANES, LANES), F32),
                        pltpu.VMEM((CONV_COLS // LANES, ROW_TILE, LANES), F32),
                        pltpu.VMEM((DN_CONV_WIDTH // LANES, SUBLANES, LANES), F32),
                        pltpu.VMEM((ROW_TILE, d), F32),
                        pltpu.VMEM((ROW_TILE, d), BF16)],
        compiler_params=pltpu.CompilerParams(dimension_semantics=("parallel", "arbitrary"),
                                             vmem_limit_bytes=VMEM_LIMIT),
        name="dn_in",
    )(head, og, x, og, w_out, norm_w, w_qkv, w_z, w_ba, conv_w, alog_row, dtb_row, tri)


def _dn_core_kernel(q_ref, k_ref, vb_ref, kbe_ref, qe_ref, bg_ref, o_ref, s_ref):
    @pl.when(pl.program_id(0) == 0)
    def _():
        s_ref[...] = jnp.zeros(s_ref.shape, F32)

    n_seq = q_ref.shape[0]
    reps = DN_V_HEADS // DN_K_HEADS
    ri = lax.broadcasted_iota(jnp.int32, (DN_CHUNK, DN_CHUNK), 0)
    ci = lax.broadcasted_iota(jnp.int32, (DN_CHUNK, DN_CHUNK), 1)
    causal = ri >= ci
    neg_strict = -(ri > ci).astype(F32)
    eye = (ri == ci).astype(F32)
    keep_t = (lax.broadcasted_iota(jnp.int32, (DN_CHUNK, 2 * DN_CHUNK), 1) >= DN_CHUNK).astype(F32)

    insts = [(b, c) for c in range(DN_STEP_CHUNKS) for b in range(n_seq)]

    qk_d, kk_d, kt_d = {}, {}, {}
    for (b, c) in insts:
        rows = slice(c * DN_CHUNK, (c + 1) * DN_CHUNK)
        for kh in range(DN_K_HEADS):
            kcols = slice(kh * DN_HEAD_DIM, (kh + 1) * DN_HEAD_DIM)
            k = k_ref[b, rows, kcols]
            qk_kk = _bdot_nt(jnp.concatenate([q_ref[b, rows, kcols], k], axis=0), k)
            key = (b, c, kh)
            qk_d[key], kk_d[key] = qk_kk[:DN_CHUNK], qk_kk[DN_CHUNK:]
            kt_d[key] = k.astype(F32).T

    lhs2_d, y_d, rhs_d, qe_d, g_last_d = {}, {}, {}, {}, {}
    for (b, c) in insts:
        rows = slice(c * DN_CHUNK, (c + 1) * DN_CHUNK)
        bg = bg_ref[b, rows, :]
        bg_t = bg.T
        for h in range(DN_V_HEADS):
            key, kkey = (b, c, h), (b, c, h // reps)
            gl = DN_V_HEADS + h
            beta_col = bg[:, h:h + 1]
            gc_col = bg[:, gl:gl + 1]
            gc_row = bg_t[gl:gl + 1, :]
            decay = jnp.exp(jnp.where(causal, gc_col - gc_row, NEG_BIG))
            g_last = gc_col[DN_CHUNK - 1:DN_CHUNK, :]
            lhs2_d[key] = jnp.concatenate(
                [(qk_d[kkey] * decay).astype(BF16),
                 (kt_d[kkey] * jnp.exp(g_last - gc_row)).astype(BF16)], axis=0)
            n1 = kk_d[kkey] * decay * (neg_strict * beta_col)
            y_d[key] = jnp.concatenate([n1, eye], axis=1)
            hcols = slice(h * DN_HEAD_DIM, (h + 1) * DN_HEAD_DIM)
            rhs_d[key] = jnp.concatenate([vb_ref[b, rows, hcols], kbe_ref[b, rows, hcols]], axis=1)
            qe_d[key] = qe_ref[b, rows, hcols]
            g_last_d[key] = g_last

    chunk_heads = [(b, h) for b in range(n_seq) for h in range(DN_V_HEADS)]
    u_d, w_d, ws_qs_d = {}, {}, {}
    s_d = {(b, h): s_ref[b, h] for (b, h) in chunk_heads}

    def inverse_round(c):
        for (b, h) in chunk_heads:
            key = (b, c, h)
            r = _bdot(y_d[key][:, :DN_CHUNK], y_d[key])
            y_d[key] = r + y_d[key] * keep_t

    def solve(c):
        for (b, h) in chunk_heads:
            key = (b, c, h)
            uw = _bdot(y_d[key][:, DN_CHUNK:], rhs_d[key])
            u_d[key], w_d[key] = uw[:, :DN_HEAD_DIM], uw[:, DN_HEAD_DIM:].astype(BF16)

    def state_read(c):
        for (b, h) in chunk_heads:
            key = (b, c, h)
            ws_qs_d[key] = _bdot(jnp.concatenate([w_d[key], qe_d[key]], axis=0), s_d[(b, h)])

    def state_update(c):
        rows = slice(c * DN_CHUNK, (c + 1) * DN_CHUNK)
        for (b, h) in chunk_heads:
            key = (b, c, h)
            v_new = u_d[key] - ws_qs_d[key][:DN_CHUNK]
            av = _bdot(lhs2_d[key], v_new)
            o = ws_qs_d[key][DN_CHUNK:] + av[:DN_CHUNK]
            o_ref[b, rows, h * DN_HEAD_DIM:(h + 1) * DN_HEAD_DIM] = o.astype(BF16)
            s_d[(b, h)] = s_d[(b, h)] * jnp.exp(g_last_d[key]) + av[DN_CHUNK:]

    for c in range(DN_STEP_CHUNKS):
        between = [lambda: state_read(c - 1), lambda: state_update(c - 1)] if c > 0 else []
        for rnd in range(6):
            inverse_round(c)
            if rnd in (0, 2) and between:
                between.pop(0)()
        solve(c)
    state_read(DN_STEP_CHUNKS - 1)
    state_update(DN_STEP_CHUNKS - 1)
    for (b, h), s_new in s_d.items():
        s_ref[b, h] = s_new


def _dn_core(q, k, vb, kbe, qe, bg):
    b, lp, _ = q.shape
    rows = DN_STEP_CHUNKS * DN_CHUNK
    blk = lambda width: pl.BlockSpec((b, rows, width), lambda i: (0, i, 0))
    return pl.pallas_call(
        _dn_core_kernel,
        grid=(lp // rows,),
        in_specs=[blk(DN_KEY_WIDTH), blk(DN_KEY_WIDTH), blk(DN_VALUE_WIDTH), blk(DN_VALUE_WIDTH),
                  blk(DN_VALUE_WIDTH), blk(LANES)],
        out_specs=blk(DN_VALUE_WIDTH),
        out_shape=jax.ShapeDtypeStruct((b, lp, DN_VALUE_WIDTH), BF16),
        scratch_shapes=[pltpu.VMEM((b, DN_V_HEADS, DN_HEAD_DIM, DN_HEAD_DIM), F32)],
        compiler_params=pltpu.CompilerParams(dimension_semantics=("arbitrary",),
                                             vmem_limit_bytes=VMEM_LIMIT),
        name="dn_core",
    )(q, k, vb, kbe, qe, bg)


def _dn_out_kernel(h_ref, o_ref, z_ref, onw_ref, w_ref, out_ref):
    onw = onw_ref[...]
    acc = h_ref[...]
    heads_per_chunk = DN_OUT_K_CHUNK // DN_HEAD_DIM
    for c in range(DN_VALUE_WIDTH // DN_OUT_K_CHUNK):
        parts = []
        for hh in range(c * heads_per_chunk, (c + 1) * heads_per_chunk):
            cols = slice(hh * DN_HEAD_DIM, (hh + 1) * DN_HEAD_DIM)
            y = _rms_rows(o_ref[:, cols].astype(F32), onw) * _silu(z_ref[:, cols].astype(F32))
            parts.append(y.astype(BF16))
        acc = acc + jnp.dot(jnp.concatenate(parts, axis=1),
                            w_ref[c * DN_OUT_K_CHUNK:(c + 1) * DN_OUT_K_CHUNK, :acc.shape[1]],
                            preferred_element_type=F32)
    out_ref[...] = acc


def _dn_out(h, o, z, onw, w_out, seq):
    b, lp, d = h.shape
    skip = (lp - seq) // ROW_TILE
    in_block = lambda width: pl.BlockSpec((None, ROW_TILE, width), lambda bi, i: (bi, i + skip, 0))
    return pl.pallas_call(
        _dn_out_kernel,
        grid=(b, seq // ROW_TILE),
        in_specs=[in_block(d), in_block(DN_VALUE_WIDTH), in_block(DN_VALUE_WIDTH),
                  pl.BlockSpec((1, DN_HEAD_DIM), lambda bi, i: (0, 0)),
                  pl.BlockSpec(w_out.shape, lambda bi, i: (0, 0))],
        out_specs=pl.BlockSpec((None, ROW_TILE, d), lambda bi, i: (bi, i, 0)),
        out_shape=jax.ShapeDtypeStruct((b, seq, d), F32),
        compiler_params=pltpu.CompilerParams(dimension_semantics=("parallel", "parallel"),
                                             vmem_limit_bytes=VMEM_LIMIT),
        name="dn_out",
    )(h, o, z, onw, w_out)


def kernel(x, meta_tokens, attn_norm_w, attn_w_in, attn_q_norm_w, attn_k_norm_w, attn_sinks,
           attn_w_out, dn_norm_w, dn_w_in, dn_conv_w, dn_a_log, dn_dt_bias, dn_o_norm_w, dn_w_out):
    b, seq, d = x.shape
    assert seq % ROW_TILE == 0 and attn_norm_w.shape[0] == 1 and dn_norm_w.shape[0] == 1
    assert (seq + ROW_TILE) % (ATTN_STEP_BLOCKS * ATTN_BLOCK) == 0
    assert (seq + ROW_TILE) % (DN_STEP_CHUNKS * DN_CHUNK) == 0
    meta = jnp.broadcast_to(meta_tokens.astype(x.dtype)[None], (b, N_META, d))
    head = jnp.concatenate([jnp.zeros((b, PAD, d), x.dtype), meta], axis=1)

    q, kv, gate = _attn_in(head, x, attn_norm_w[0][None], attn_w_in[0].astype(BF16))
    og = _attention(q, kv, gate, attn_sinks[0], attn_q_norm_w[0][None], attn_k_norm_w[0][None])

    w_in = dn_w_in[0]
    w_qkv = _mxu_weight(w_in[:, :DN_CONV_WIDTH])
    w_z = _mxu_weight(w_in[:, DN_CONV_WIDTH:DN_CONV_WIDTH + DN_VALUE_WIDTH])
    n_ba = 2 * DN_V_HEADS
    w_ba = jnp.pad(w_in[:, DN_CONV_WIDTH + DN_VALUE_WIDTH:], ((0, 0), (0, LANES - n_ba))).astype(BF16)
    lane_pad = lambda t: jnp.pad(t[None], ((0, 0), (DN_V_HEADS, LANES - n_ba)))
    h1, qd, kd, vbd, kbed, qed, zd, bg = _dn_in(head, x, og, _mxu_weight(attn_w_out[0]), dn_norm_w[0][None],
                                    w_qkv, w_z, w_ba, dn_conv_w[0],
                                    lane_pad(dn_a_log[0]), lane_pad(dn_dt_bias[0]))
    od = _dn_core(qd, kd, vbd, kbed, qed, bg)
    return _dn_out(h1, od, zd, dn_o_norm_w[0][None], _mxu_weight(dn_w_out[0]), seq)
```

```python
import numpy as np
import jax
import jax.numpy as jnp
from jax import lax
from jax.experimental import pallas as pl
from jax.experimental.pallas import tpu as pltpu

F32 = jnp.float32
BF16 = jnp.bfloat16

N_META = 16
NORM_EPS = 1e-6

ATTN_HEAD_DIM = 64
ATTN_HEADS = 16
ATTN_KV_HEADS = 2
ATTN_GROUPS = ATTN_HEADS // ATTN_KV_HEADS
ATTN_WIDTH = ATTN_HEADS * ATTN_HEAD_DIM
ATTN_KV_WIDTH = ATTN_KV_HEADS * ATTN_HEAD_DIM
WINDOW = 128
ATTN_BLOCK = 128
ATTN_STEP_BLOCKS = 5

DN_HEAD_DIM = 128
DN_K_HEADS = 8
DN_V_HEADS = 16
DN_KEY_WIDTH = DN_K_HEADS * DN_HEAD_DIM
DN_VALUE_WIDTH = DN_V_HEADS * DN_HEAD_DIM
DN_CONV = 4
DN_CHUNK = 64
DN_CONV_WIDTH = 2 * DN_KEY_WIDTH + DN_VALUE_WIDTH

LANES = 128
SUBLANES = 8
ROW_TILE = 256
PAD = ROW_TILE - N_META
META_BLOCK = PAD // ATTN_BLOCK
META_OFFSET = PAD % ATTN_BLOCK
CONV_COLS = 512
FILL_COLS = 1024
CONV_PHASES = 4
DN_OUT_K_CHUNK = 512
DN_STEP_CHUNKS = 5
WEIGHT_PAD_COLS = 256
VMEM_LIMIT = 56 * 1024 * 1024

LOG2E = 1.4426950408889634
NEG_BIG = -1e30
CLIP_INVALID = 1e30


def _alibi_slopes(n_heads):
    return np.exp2(-8.0 * np.arange(1, n_heads + 1) / n_heads).astype(np.float32)


def _bdot(a, b):
    return jnp.dot(a.astype(BF16), b.astype(BF16), preferred_element_type=F32)


def _bdot_nt(a, b):
    return lax.dot_general(a.astype(BF16), b.astype(BF16), (((1,), (1,)), ((), ())),
                           preferred_element_type=F32)


def _rms_rows(x, w):
    return x * lax.rsqrt(jnp.mean(x * x, axis=-1, keepdims=True) + NORM_EPS) * w


def _silu(x):
    return x * jax.nn.sigmoid(x)


def _mxu_weight(w):
    return jnp.pad(w.astype(BF16), ((0, 0), (0, WEIGHT_PAD_COLS)))


def _layer_input(head_ref, x_ref):
    return jnp.where(pl.program_id(1) == 0, head_ref[...], x_ref[...])


def _attn_in_kernel(head_ref, x_ref, nw_ref, w_ref, q_ref, kv_ref, g_ref):
    xn = _rms_rows(_layer_input(head_ref, x_ref), nw_ref[...]).astype(BF16)
    q_ref[...] = jnp.dot(xn, w_ref[:, :ATTN_WIDTH], preferred_element_type=F32).astype(BF16)
    kv_ref[...] = jnp.dot(xn, w_ref[:, ATTN_WIDTH:ATTN_WIDTH + 2 * ATTN_KV_WIDTH],
                          preferred_element_type=F32).astype(BF16)
    g_ref[...] = jnp.dot(xn, w_ref[:, ATTN_WIDTH + 2 * ATTN_KV_WIDTH:],
                         preferred_element_type=F32).astype(BF16)


def _padded_input_specs(d):
    return [pl.BlockSpec((None, ROW_TILE, d), lambda bi, i: (bi, 0, 0)),
            pl.BlockSpec((None, ROW_TILE, d), lambda bi, i: (bi, jnp.maximum(i - 1, 0), 0))]


def _attn_in(head, x, norm_w, w_in):
    b, seq, d = x.shape
    lp = seq + ROW_TILE
    n_in = w_in.shape[1]
    row_block = lambda width: pl.BlockSpec((None, ROW_TILE, width), lambda bi, i: (bi, i, 0))
    return pl.pallas_call(
        _attn_in_kernel,
        grid=(b, lp // ROW_TILE),
        in_specs=_padded_input_specs(d) + [pl.BlockSpec((1, d), lambda bi, i: (0, 0)),
                                           pl.BlockSpec((d, n_in), lambda bi, i: (0, 0))],
        out_specs=[row_block(ATTN_WIDTH), row_block(2 * ATTN_KV_WIDTH), row_block(ATTN_WIDTH)],
        out_shape=[jax.ShapeDtypeStruct((b, lp, ATTN_WIDTH), BF16),
                   jax.ShapeDtypeStruct((b, lp, 2 * ATTN_KV_WIDTH), BF16),
                   jax.ShapeDtypeStruct((b, lp, ATTN_WIDTH), BF16)],
        compiler_params=pltpu.CompilerParams(dimension_semantics=("parallel", "parallel"),
                                             vmem_limit_bytes=VMEM_LIMIT),
        name="attn_in",
    )(head, x, norm_w, w_in)


def _attn_kernel(sink_ref, q_ref, kvm_ref, kvp_ref, kvc_ref, gate_ref, qnw_ref, knw_ref, o_ref):
    def clip(dist, valid):
        return jnp.where(valid, jnp.minimum(dist, WINDOW).astype(F32), CLIP_INVALID)

    jm = lax.broadcasted_iota(jnp.int32, (N_META, ATTN_BLOCK), 0)
    qm = lax.broadcasted_iota(jnp.int32, (N_META, ATTN_BLOCK), 1)
    c = lax.broadcasted_iota(jnp.int32, (ATTN_BLOCK, ATTN_BLOCK), 0)
    qi = lax.broadcasted_iota(jnp.int32, (ATTN_BLOCK, ATTN_BLOCK), 1)
    slopes = _alibi_slopes(ATTN_HEADS) * np.float32(LOG2E)
    qnw = qnw_ref[...] * (ATTN_HEAD_DIM ** -0.5 * LOG2E)
    knw = knw_ref[...]

    def kv_rows(sub, cols):
        cur = kvc_ref[sub * ATTN_BLOCK:(sub + 1) * ATTN_BLOCK, cols]
        prev = kvp_ref[:, cols] if sub == 0 else kvc_ref[(sub - 1) * ATTN_BLOCK:sub * ATTN_BLOCK, cols]
        return jnp.concatenate([kvm_ref[:, cols], prev, cur], axis=0)

    units = [(sub, h) for sub in range(ATTN_STEP_BLOCKS) for h in range(ATTN_KV_HEADS)]
    s_d, v_d, clip_d = {}, {}, {}
    for sub in range(ATTN_STEP_BLOCKS):
        nref = pl.program_id(1) * ATTN_STEP_BLOCKS + sub - META_BLOCK
        dist_m = nref * ATTN_BLOCK + qm - (META_OFFSET + jm)
        clip_d[sub] = jnp.concatenate(
            [clip(dist_m, dist_m >= 0),
             clip(ATTN_BLOCK + qi - c, jnp.logical_and(c > qi, nref >= 2)),
             clip(qi - c, jnp.logical_and(c <= qi, nref >= 1))], axis=0)
    for (sub, h) in units:
        rows = slice(sub * ATTN_BLOCK, (sub + 1) * ATTN_BLOCK)
        k_ext = kv_rows(sub, slice(h * ATTN_HEAD_DIM, (h + 1) * ATTN_HEAD_DIM))
        k_ext = _rms_rows(k_ext.astype(F32), knw).astype(BF16)
        v_d[(sub, h)] = kv_rows(sub, slice(ATTN_KV_WIDTH + h * ATTN_HEAD_DIM,
                                           ATTN_KV_WIDTH + (h + 1) * ATTN_HEAD_DIM))
        group_cols = slice(h * ATTN_GROUPS * ATTN_HEAD_DIM, (h + 1) * ATTN_GROUPS * ATTN_HEAD_DIM)
        q_t = q_ref[rows, group_cols].astype(F32).T
        qn = []
        for g in range(ATTN_GROUPS):
            qg = q_t[g * ATTN_HEAD_DIM:(g + 1) * ATTN_HEAD_DIM]
            inv = lax.rsqrt(jnp.mean(qg * qg, axis=0, keepdims=True) + NORM_EPS)
            qn.append((qg * inv * qnw).astype(BF16))
        s_d[(sub, h)] = jnp.dot(k_ext, jnp.concatenate(qn, axis=1),
                                preferred_element_type=F32)

    for (sub, h) in units:
        rows = slice(sub * ATTN_BLOCK, (sub + 1) * ATTN_BLOCK)
        p_l, rinv_l = [], []
        for g in range(ATTN_GROUPS):
            hq = h * ATTN_GROUPS + g
            s = s_d[(sub, h)][:, g * ATTN_BLOCK:(g + 1) * ATTN_BLOCK] - float(slopes[hq]) * clip_d[sub]
            sink = sink_ref[hq] * LOG2E
            m = jnp.maximum(jnp.max(s, axis=0, keepdims=True), sink)
            p = jnp.exp2(s - m)
            rinv_l.append(1.0 / (jnp.sum(p, axis=0, keepdims=True) + jnp.exp2(sink - m)))
            p_l.append(p.astype(BF16))
        o_t = lax.dot_general(v_d[(sub, h)], jnp.concatenate(p_l, axis=1), (((0,), (0,)), ((), ())),
                              preferred_element_type=F32)
        o_t = jnp.concatenate(
            [o_t[:, g * ATTN_BLOCK:(g + 1) * ATTN_BLOCK] * rinv_l[g] for g in range(ATTN_GROUPS)],
            axis=0)
        group_cols = slice(h * ATTN_GROUPS * ATTN_HEAD_DIM, (h + 1) * ATTN_GROUPS * ATTN_HEAD_DIM)
        o_ref[rows, group_cols] = (o_t.T * _silu(gate_ref[rows, group_cols].astype(F32))).astype(BF16)


def _attention(q, kv, gate, sinks, qnw, knw):
    b, lp, _ = q.shape
    step_rows = ATTN_STEP_BLOCKS * ATTN_BLOCK
    kv_width = 2 * ATTN_KV_WIDTH
    wide_block = (None, step_rows, ATTN_WIDTH)
    qnw_lanes = jnp.broadcast_to(qnw.reshape(ATTN_HEAD_DIM, 1), (ATTN_HEAD_DIM, LANES))
    return pl.pallas_call(
        _attn_kernel,
        grid=(b, lp // step_rows),
        in_specs=[pl.BlockSpec(memory_space=pltpu.SMEM),
                  pl.BlockSpec(wide_block, lambda bi, n: (bi, n, 0)),
                  pl.BlockSpec((None, N_META, kv_width), lambda bi, n: (bi, PAD // N_META, 0)),
                  pl.BlockSpec((None, ATTN_BLOCK, kv_width),
                               lambda bi, n: (bi, jnp.maximum(n * ATTN_STEP_BLOCKS - 1, 0), 0)),
                  pl.BlockSpec((None, step_rows, kv_width), lambda bi, n: (bi, n, 0)),
                  pl.BlockSpec(wide_block, lambda bi, n: (bi, n, 0)),
                  pl.BlockSpec((ATTN_HEAD_DIM, LANES), lambda bi, n: (0, 0)),
                  pl.BlockSpec((1, ATTN_HEAD_DIM), lambda bi, n: (0, 0))],
        out_specs=pl.BlockSpec(wide_block, lambda bi, n: (bi, n, 0)),
        out_shape=jax.ShapeDtypeStruct((b, lp, ATTN_WIDTH), BF16),
        compiler_params=pltpu.CompilerParams(dimension_semantics=("parallel", "parallel"),
                                             vmem_limit_bytes=VMEM_LIMIT),
        name="attn_core",
    )(sinks, q, kv, kv, kv, gate, qnw_lanes, knw)


def _dn_in_kernel(head_ref, og0_ref, xnext_ref, ognext_ref, wout_ref, nw_ref, wqkv_ref, wz_ref, wba_ref,
                  convw_ref, alog_ref, dtb_ref, tri_ref,
                  h1_ref, q_ref, k_ref, vb_ref, kbe_ref, qe_ref, z_ref, bg_ref,
                  buf_ref, ybuf_ref, halo_ref, h1s_ref, xn_ref):
    i = pl.program_id(1)
    halo = SUBLANES

    @pl.when(i == 0)
    def _():
        halo_ref[...] = jnp.zeros(halo_ref.shape, F32)
        h1s_ref[...] = head_ref[...] + jnp.dot(og0_ref[...], wout_ref[:, :h1s_ref.shape[1]],
                                               preferred_element_type=F32)
        xn_ref[...] = _rms_rows(h1s_ref[...], nw_ref[...]).astype(BF16)

    h1_ref[...] = h1s_ref[...]
    xn = xn_ref[...]

    def z_part(c0):
        def run():
            z_ref[:, c0:c0 + FILL_COLS] = jnp.dot(xn, wz_ref[:, c0:c0 + FILL_COLS],
                                                  preferred_element_type=F32).astype(BF16)
        return run

    def next_h1_part(c0):
        def run():
            h1s_ref[:, c0:c0 + FILL_COLS] = xnext_ref[:, c0:c0 + FILL_COLS] + jnp.dot(
                ognext_ref[...], wout_ref[:, c0:c0 + FILL_COLS], preferred_element_type=F32)
        return run

    fillers = ([z_part(c0) for c0 in range(0, DN_VALUE_WIDTH, FILL_COLS)]
               + [next_h1_part(c0) for c0 in range(0, h1s_ref.shape[1], FILL_COLS)])

    ba = jnp.dot(xn, wba_ref[...], preferred_element_type=F32)
    lane = lax.broadcasted_iota(jnp.int32, ba.shape, 1)
    row = i * ROW_TILE + lax.broadcasted_iota(jnp.int32, ba.shape, 0)
    x = ba + dtb_ref[...]
    softplus = jnp.maximum(x, 0.0) + jnp.log1p(jnp.exp(-jnp.abs(x)))
    g = -jnp.exp(alog_ref[...]) * softplus
    bg = jnp.where(lane < DN_V_HEADS, jax.nn.sigmoid(ba), g)
    bg = jnp.where(row >= PAD, bg, 0.0)
    csum = jnp.dot(tri_ref[...], bg, preferred_element_type=F32, precision=lax.Precision.HIGHEST)
    bg = jnp.where(lane < DN_V_HEADS, bg, csum)
    bg_ref[...] = bg
    egc = jnp.exp(bg)
    reps = DN_V_HEADS // DN_K_HEADS

    q_scale = DN_HEAD_DIM ** -0.5
    rows_per_phase = ROW_TILE // CONV_PHASES
    slabs_per_chunk = CONV_COLS // LANES
    for cj in range(DN_CONV_WIDTH // CONV_COLS):
        u = jnp.dot(xn, wqkv_ref[:, cj * CONV_COLS:(cj + 1) * CONV_COLS],
                    preferred_element_type=F32)
        if cj >= 1 and fillers:
            fillers.pop(0)()
        for sl in range(slabs_per_chunk):
            slab = cj * slabs_per_chunk + sl
            col0 = slab * LANES
            buf_ref[sl, 0:halo, :] = halo_ref[slab]
            buf_ref[sl, halo:halo + ROW_TILE, :] = u[:, sl * LANES:(sl + 1) * LANES]
            w4 = convw_ref[:, col0:col0 + LANES]
            taps = {}
            for start in range(halo - (DN_CONV - 1), halo + CONV_PHASES):
                taps[start] = buf_ref[sl, pl.ds(start, rows_per_phase, stride=CONV_PHASES), :]
            for a in range(CONV_PHASES):
                acc = None
                for j in range(DN_CONV):
                    term = w4[DN_CONV - 1 - j:DN_CONV - j] * taps[halo + a - j]
                    acc = term if acc is None else acc + term
                ybuf_ref[sl, pl.ds(a, rows_per_phase, stride=CONV_PHASES), :] = _silu(acc)
            halo_ref[slab] = buf_ref[sl, ROW_TILE:ROW_TILE + halo, :]
            y = ybuf_ref[sl]
            if col0 < DN_KEY_WIDTH:
                kh = slab
                qn = y * (lax.rsqrt(jnp.sum(y * y, axis=-1, keepdims=True) + NORM_EPS) * q_scale)
                q_ref[:, col0:col0 + LANES] = qn.astype(BF16)
                for h in range(reps * kh, reps * (kh + 1)):
                    gl = DN_V_HEADS + h
                    qe_ref[:, h * LANES:(h + 1) * LANES] = (qn * egc[:, gl:gl + 1]).astype(BF16)
            elif col0 < 2 * DN_KEY_WIDTH:
                kh = slab - DN_K_HEADS
                kn = y * lax.rsqrt(jnp.sum(y * y, axis=-1, keepdims=True) + NORM_EPS)
                k_ref[:, kh * LANES:(kh + 1) * LANES] = kn.astype(BF16)
                for h in range(reps * kh, reps * (kh + 1)):
                    gl = DN_V_HEADS + h
                    kbe_ref[:, h * LANES:(h + 1) * LANES] = (
                        kn * (bg[:, h:h + 1] * egc[:, gl:gl + 1])).astype(BF16)
            else:
                h = slab - 2 * DN_K_HEADS
                vb_ref[:, h * LANES:(h + 1) * LANES] = (y * bg[:, h:h + 1]).astype(BF16)

    for run in fillers:
        run()
    xn_ref[...] = _rms_rows(h1s_ref[...], nw_ref[...]).astype(BF16)


def _dn_in(head, x, og, w_out, norm_w, w_qkv, w_z, w_ba, conv_w, alog_row, dtb_row):
    b, seq, d = x.shape
    lp = seq + ROW_TILE
    nt = lp // ROW_TILE
    n_x = seq // ROW_TILE
    idx = np.arange(ROW_TILE)
    tri = jnp.asarray((idx[:, None] >= idx[None, :])
                      & (idx[:, None] // DN_CHUNK == idx[None, :] // DN_CHUNK), F32)
    row_block = lambda width: pl.BlockSpec((None, ROW_TILE, width), lambda bi, i: (bi, i, 0))
    full = lambda shape: pl.BlockSpec(shape, lambda bi, i: (0,) * len(shape),
                                      pipeline_mode=pl.Buffered(1))
    return pl.pallas_call(
        _dn_in_kernel,
        grid=(b, nt),
        in_specs=[
            pl.BlockSpec((None, ROW_TILE, d), lambda bi, i: (bi, 0, 0)),
            pl.BlockSpec((None, ROW_TILE, og.shape[-1]), lambda bi, i: (bi, 0, 0)),
            pl.BlockSpec((None, ROW_TILE, d), lambda bi, i: (bi, jnp.minimum(i, n_x - 1), 0)),
            pl.BlockSpec((None, ROW_TILE, og.shape[-1]), lambda bi, i: (bi, jnp.minimum(i + 1, nt - 1), 0)),
            full(w_out.shape), full((1, d)), full(w_qkv.shape), full(w_z.shape),
            full(w_ba.shape), full(conv_w.shape), full((1, LANES)), full((1, LANES)), full(tri.shape)],
        out_specs=[row_block(d), row_block(DN_KEY_WIDTH), row_block(DN_KEY_WIDTH),
                   row_block(DN_VALUE_WIDTH), row_block(DN_VALUE_WIDTH), row_block(DN_VALUE_WIDTH),
                   row_block(DN_VALUE_WIDTH), row_block(LANES)],
        out_shape=[jax.ShapeDtypeStruct((b, lp, d), F32),
                   jax.ShapeDtypeStruct((b, lp, DN_KEY_WIDTH), BF16),
                   jax.ShapeDtypeStruct((b, lp, DN_KEY_WIDTH), BF16),
                   jax.ShapeDtypeStruct((b, lp, DN_VALUE_WIDTH), BF16),
                   jax.ShapeDtypeStruct((b, lp, DN_VALUE_WIDTH), BF16),
                   jax.ShapeDtypeStruct((b, lp, DN_VALUE_WIDTH), BF16),
                   jax.ShapeDtypeStruct((b, lp, DN_VALUE_WIDTH), BF16),
                   jax.ShapeDtypeStruct((b, lp, LANES), F32)],
        scratch_shapes=[pltpu.VMEM((CONV_COLS // LANES, ROW_TILE + SUBLANES, LANES), F32),
                        pltpu.VMEM((CONV_COLS // LANES, ROW_TILE, LANES), F32),
                        pltpu.VMEM((DN_CONV_WIDTH // LANES, SUBLANES, LANES), F32),
                        pltpu.VMEM((ROW_TILE, d), F32),
                        pltpu.VMEM((ROW_TILE, d), BF16)],
        compiler_params=pltpu.CompilerParams(dimension_semantics=("parallel", "arbitrary"),
                                             vmem_limit_bytes=VMEM_LIMIT),
        name="dn_in",
    )(head, og, x, og, w_out, norm_w, w_qkv, w_z, w_ba, conv_w, alog_row, dtb_row, tri)


def _dn_core_kernel(q_ref, k_ref, vb_ref, kbe_ref, qe_ref, bg_ref, o_ref, s_ref):
    @pl.when(pl.program_id(0) == 0)
    def _():
        s_ref[...] = jnp.zeros(s_ref.shape, F32)

    n_seq = q_ref.shape[0]
    reps = DN_V_HEADS // DN_K_HEADS
    ri = lax.broadcasted_iota(jnp.int32, (DN_CHUNK, DN_CHUNK), 0)
    ci = lax.broadcasted_iota(jnp.int32, (DN_CHUNK, DN_CHUNK), 1)
    causal = ri >= ci
    neg_strict = -(ri > ci).astype(F32)
    eye = (ri == ci).astype(F32)
    keep_t = (lax.broadcasted_iota(jnp.int32, (DN_CHUNK, 2 * DN_CHUNK), 1) >= DN_CHUNK).astype(F32)

    insts = [(b, c) for c in range(DN_STEP_CHUNKS) for b in range(n_seq)]

    qk_d, kk_d, kt_d = {}, {}, {}
    for (b, c) in insts:
        rows = slice(c * DN_CHUNK, (c + 1) * DN_CHUNK)
        for kh in range(DN_K_HEADS):
            kcols = slice(kh * DN_HEAD_DIM, (kh + 1) * DN_HEAD_DIM)
            k = k_ref[b, rows, kcols]
            qk_kk = _bdot_nt(jnp.concatenate([q_ref[b, rows, kcols], k], axis=0), k)
            key = (b, c, kh)
            qk_d[key], kk_d[key] = qk_kk[:DN_CHUNK], qk_kk[DN_CHUNK:]
            kt_d[key] = k.astype(F32).T

    lhs2_d, y_d, rhs_d, qe_d, g_last_d = {}, {}, {}, {}, {}
    for (b, c) in insts:
        rows = slice(c * DN_CHUNK, (c + 1) * DN_CHUNK)
        bg = bg_ref[b, rows, :]
        bg_t = bg.T
        for h in range(DN_V_HEADS):
            key, kkey = (b, c, h), (b, c, h // reps)
            gl = DN_V_HEADS + h
            beta_col = bg[:, h:h + 1]
            gc_col = bg[:, gl:gl + 1]
            gc_row = bg_t[gl:gl + 1, :]
            decay = jnp.exp(jnp.where(causal, gc_col - gc_row, NEG_BIG))
            g_last = gc_col[DN_CHUNK - 1:DN_CHUNK, :]
            lhs2_d[key] = jnp.concatenate(
                [(qk_d[kkey] * decay).astype(BF16),
                 (kt_d[kkey] * jnp.exp(g_last - gc_row)).astype(BF16)], axis=0)
            n1 = kk_d[kkey] * decay * (neg_strict * beta_col)
            y_d[key] = jnp.concatenate([n1, eye], axis=1)
            hcols = slice(h * DN_HEAD_DIM, (h + 1) * DN_HEAD_DIM)
            rhs_d[key] = jnp.concatenate([vb_ref[b, rows, hcols], kbe_ref[b, rows, hcols]], axis=1)
            qe_d[key] = qe_ref[b, rows, hcols]
            g_last_d[key] = g_last

    chunk_heads = [(b, h) for b in range(n_seq) for h in range(DN_V_HEADS)]
    u_d, w_d, ws_qs_d = {}, {}, {}
    s_d = {(b, h): s_ref[b, h] for (b, h) in chunk_heads}

    def inverse_round(c):
        for (b, h) in chunk_heads:
            key = (b, c, h)
            r = _bdot(y_d[key][:, :DN_CHUNK], y_d[key])
            y_d[key] = r + y_d[key] * keep_t

    def solve(c):
        for (b, h) in chunk_heads:
            key = (b, c, h)
            uw = _bdot(y_d[key][:, DN_CHUNK:], rhs_d[key])
            u_d[key], w_d[key] = uw[:, :DN_HEAD_DIM], uw[:, DN_HEAD_DIM:].astype(BF16)

    def state_read(c):
        for (b, h) in chunk_heads:
            key = (b, c, h)
            ws_qs_d[key] = _bdot(jnp.concatenate([w_d[key], qe_d[key]], axis=0), s_d[(b, h)])

    def state_update(c):
        rows = slice(c * DN_CHUNK, (c + 1) * DN_CHUNK)
        for (b, h) in chunk_heads:
            key = (b, c, h)
            v_new = u_d[key] - ws_qs_d[key][:DN_CHUNK]
            av = _bdot(lhs2_d[key], v_new)
            o = ws_qs_d[key][DN_CHUNK:] + av[:DN_CHUNK]
            o_ref[b, rows, h * DN_HEAD_DIM:(h + 1) * DN_HEAD_DIM] = o.astype(BF16)
            s_d[(b, h)] = s_d[(b, h)] * jnp.exp(g_last_d[key]) + av[DN_CHUNK:]

    for c in range(DN_STEP_CHUNKS):
        between = [lambda: state_read(c - 1), lambda: state_update(c - 1)] if c > 0 else []
        for rnd in range(6):
            inverse_round(c)
            if rnd in (0, 2) and between:
                between.pop(0)()
        solve(c)
    state_read(DN_STEP_CHUNKS - 1)
    state_update(DN_STEP_CHUNKS - 1)
    for (b, h), s_new in s_d.items():
        s_ref[b, h] = s_new


def _dn_core(q, k, vb, kbe, qe, bg):
    b, lp, _ = q.shape
    rows = DN_STEP_CHUNKS * DN_CHUNK
    blk = lambda width: pl.BlockSpec((b, rows, width), lambda i: (0, i, 0))
    return pl.pallas_call(
        _dn_core_kernel,
        grid=(lp // rows,),
        in_specs=[blk(DN_KEY_WIDTH), blk(DN_KEY_WIDTH), blk(DN_VALUE_WIDTH), blk(DN_VALUE_WIDTH),
                  blk(DN_VALUE_WIDTH), blk(LANES)],
        out_specs=blk(DN_VALUE_WIDTH),
        out_shape=jax.ShapeDtypeStruct((b, lp, DN_VALUE_WIDTH), BF16),
        scratch_shapes=[pltpu.VMEM((b, DN_V_HEADS, DN_HEAD_DIM, DN_HEAD_DIM), F32)],
        compiler_params=pltpu.CompilerParams(dimension_semantics=("arbitrary",),
                                             vmem_limit_bytes=VMEM_LIMIT),
        name="dn_core",
    )(q, k, vb, kbe, qe, bg)


def _dn_out_kernel(h_ref, o_ref, z_ref, onw_ref, w_ref, out_ref):
    onw = onw_ref[...]
    acc = h_ref[...]
    heads_per_chunk = DN_OUT_K_CHUNK // DN_HEAD_DIM
    for c in range(DN_VALUE_WIDTH // DN_OUT_K_CHUNK):
        parts = []
        for hh in range(c * heads_per_chunk, (c + 1) * heads_per_chunk):
            cols = slice(hh * DN_HEAD_DIM, (hh + 1) * DN_HEAD_DIM)
            y = _rms_rows(o_ref[:, cols].astype(F32), onw) * _silu(z_ref[:, cols].astype(F32))
            parts.append(y.astype(BF16))
        acc = acc + jnp.dot(jnp.concatenate(parts, axis=1),
                            w_ref[c * DN_OUT_K_CHUNK:(c + 1) * DN_OUT_K_CHUNK, :acc.shape[1]],
                            preferred_element_type=F32)
    out_ref[...] = acc


def _dn_out(h, o, z, onw, w_out, seq):
    b, lp, d = h.shape
    skip = (lp - seq) // ROW_TILE
    in_block = lambda width: pl.BlockSpec((None, ROW_TILE, width), lambda bi, i: (bi, i + skip, 0))
    return pl.pallas_call(
        _dn_out_kernel,
        grid=(b, seq // ROW_TILE),
        in_specs=[in_block(d), in_block(DN_VALUE_WIDTH), in_block(DN_VALUE_WIDTH),
                  pl.BlockSpec((1, DN_HEAD_DIM), lambda bi, i: (0, 0)),
                  pl.BlockSpec(w_out.shape, lambda bi, i: (0, 0))],
        out_specs=pl.BlockSpec((None, ROW_TILE, d), lambda bi, i: (bi, i, 0)),
        out_shape=jax.ShapeDtypeStruct((b, seq, d), F32),
        compiler_params=pltpu.CompilerParams(dimension_semantics=("parallel", "parallel"),
                                             vmem_limit_bytes=VMEM_LIMIT),
        name="dn_out",
    )(h, o, z, onw, w_out)


def kernel(x, meta_tokens, attn_norm_w, attn_w_in, attn_q_norm_w, attn_k_norm_w, attn_sinks,
           attn_w_out, dn_norm_w, dn_w_in, dn_conv_w, dn_a_log, dn_dt_bias, dn_o_norm_w, dn_w_out):
    b, seq, d = x.shape
    assert seq % ROW_TILE == 0 and attn_norm_w.shape[0] == 1 and dn_norm_w.shape[0] == 1
    assert (seq + ROW_TILE) % (ATTN_STEP_BLOCKS * ATTN_BLOCK) == 0
    assert (seq + ROW_TILE) % (DN_STEP_CHUNKS * DN_CHUNK) == 0
    meta = jnp.broadcast_to(meta_tokens.astype(x.dtype)[None], (b, N_META, d))
    head = jnp.concatenate([jnp.zeros((b, PAD, d), x.dtype), meta], axis=1)

    q, kv, gate = _attn_in(head, x, attn_norm_w[0][None], attn_w_in[0].astype(BF16))
    og = _attention(q, kv, gate, attn_sinks[0], attn_q_norm_w[0][None], attn_k_norm_w[0][None])

    w_in = dn_w_in[0]
    w_qkv = _mxu_weight(w_in[:, :DN_CONV_WIDTH])
    w_z = _mxu_weight(w_in[:, DN_CONV_WIDTH:DN_CONV_WIDTH + DN_VALUE_WIDTH])
    n_ba = 2 * DN_V_HEADS
    w_ba = jnp.pad(w_in[:, DN_CONV_WIDTH + DN_VALUE_WIDTH:], ((0, 0), (0, LANES - n_ba))).astype(BF16)
    lane_pad = lambda t: jnp.pad(t[None], ((0, 0), (DN_V_HEADS, LANES - n_ba)))
    h1, qd, kd, vbd, kbed, qed, zd, bg = _dn_in(head, x, og, _mxu_weight(attn_w_out[0]), dn_norm_w[0][None],
                                    w_qkv, w_z, w_ba, dn_conv_w[0],
                                    lane_pad(dn_a_log[0]), lane_pad(dn_dt_bias[0]))
    od = _dn_core(qd, kd, vbd, kbed, qed, bg)
    return _dn_out(h1, od, zd, dn_o_norm_w[0][None], _mxu_weight(dn_w_out[0]), seq)
```

```python
import numpy as np
import jax
import jax.numpy as jnp
from jax import lax
from jax.experimental import pallas as pl
from jax.experimental.pallas import tpu as pltpu

F32 = jnp.float32
BF16 = jnp.bfloat16

N_META = 16
NORM_EPS = 1e-6

ATTN_HEAD_DIM = 64
ATTN_HEADS = 16
ATTN_KV_HEADS = 2
ATTN_GROUPS = ATTN_HEADS // ATTN_KV_HEADS
ATTN_WIDTH = ATTN_HEADS * ATTN_HEAD_DIM
ATTN_KV_WIDTH = ATTN_KV_HEADS * ATTN_HEAD_DIM
WINDOW = 128
ATTN_BLOCK = 128
ATTN_STEP_BLOCKS = 5

DN_HEAD_DIM = 128
DN_K_HEADS = 8
DN_V_HEADS = 16
DN_KEY_WIDTH = DN_K_HEADS * DN_HEAD_DIM
DN_VALUE_WIDTH = DN_V_HEADS * DN_HEAD_DIM
DN_CONV = 4
DN_CHUNK = 64
DN_CONV_WIDTH = 2 * DN_KEY_WIDTH + DN_VALUE_WIDTH

LANES = 128
SUBLANES = 8
ROW_TILE = 256
PAD = ROW_TILE - N_META
META_BLOCK = PAD // ATTN_BLOCK
META_OFFSET = PAD % ATTN_BLOCK
CONV_COLS = 512
FILL_COLS = 1024
CONV_PHASES = 4
DN_OUT_K_CHUNK = 256
DN_STEP_CHUNKS = 5
WEIGHT_PAD_COLS = 256
VMEM_LIMIT = 56 * 1024 * 1024

LOG2E = 1.4426950408889634
NEG_BIG = -1e30
CLIP_INVALID = 1e30


def _alibi_slopes(n_heads):
    return np.exp2(-8.0 * np.arange(1, n_heads + 1) / n_heads).astype(np.float32)


def _bdot(a, b):
    return jnp.dot(a.astype(BF16), b.astype(BF16), preferred_element_type=F32)


def _bdot_nt(a, b):
    return lax.dot_general(a.astype(BF16), b.astype(BF16), (((1,), (1,)), ((), ())),
                           preferred_element_type=F32)


def _rms_rows(x, w):
    return x * lax.rsqrt(jnp.mean(x * x, axis=-1, keepdims=True) + NORM_EPS) * w


def _silu(x):
    return x * jax.nn.sigmoid(x)


def _mxu_weight(w):
    return jnp.pad(w.astype(BF16), ((0, 0), (0, WEIGHT_PAD_COLS)))


def _layer_input(head_ref, x_ref):
    return jnp.where(pl.program_id(1) == 0, head_ref[...], x_ref[...])


def _attn_in_kernel(head_ref, x_ref, nw_ref, w_ref, q_ref, kv_ref, g_ref):
    xn = _rms_rows(_layer_input(head_ref, x_ref), nw_ref[...]).astype(BF16)
    q_ref[...] = jnp.dot(xn, w_ref[:, :ATTN_WIDTH], preferred_element_type=F32).astype(BF16)
    kv_ref[...] = jnp.dot(xn, w_ref[:, ATTN_WIDTH:ATTN_WIDTH + 2 * ATTN_KV_WIDTH],
                          preferred_element_type=F32).astype(BF16)
    g_ref[...] = jnp.dot(xn, w_ref[:, ATTN_WIDTH + 2 * ATTN_KV_WIDTH:],
                         preferred_element_type=F32).astype(BF16)


def _padded_input_specs(d):
    return [pl.BlockSpec((None, ROW_TILE, d), lambda bi, i: (bi, 0, 0)),
            pl.BlockSpec((None, ROW_TILE, d), lambda bi, i: (bi, jnp.maximum(i - 1, 0), 0))]


def _attn_in(head, x, norm_w, w_in):
    b, seq, d = x.shape
    lp = seq + ROW_TILE
    n_in = w_in.shape[1]
    row_block = lambda width: pl.BlockSpec((None, ROW_TILE, width), lambda bi, i: (bi, i, 0))
    return pl.pallas_call(
        _attn_in_kernel,
        grid=(b, lp // ROW_TILE),
        in_specs=_padded_input_specs(d) + [pl.BlockSpec((1, d), lambda bi, i: (0, 0)),
                                           pl.BlockSpec((d, n_in), lambda bi, i: (0, 0))],
        out_specs=[row_block(ATTN_WIDTH), row_block(2 * ATTN_KV_WIDTH), row_block(ATTN_WIDTH)],
        out_shape=[jax.ShapeDtypeStruct((b, lp, ATTN_WIDTH), BF16),
                   jax.ShapeDtypeStruct((b, lp, 2 * ATTN_KV_WIDTH), BF16),
                   jax.ShapeDtypeStruct((b, lp, ATTN_WIDTH), BF16)],
        compiler_params=pltpu.CompilerParams(dimension_semantics=("parallel", "parallel"),
                                             vmem_limit_bytes=VMEM_LIMIT),
        name="attn_in",
    )(head, x, norm_w, w_in)


def _attn_kernel(sink_ref, q_ref, kvm_ref, kvp_ref, kvc_ref, gate_ref, qnw_ref, knw_ref, o_ref):
    def clip(dist, valid):
        return jnp.where(valid, jnp.minimum(dist, WINDOW).astype(F32), CLIP_INVALID)

    jm = lax.broadcasted_iota(jnp.int32, (N_META, ATTN_BLOCK), 0)
    qm = lax.broadcasted_iota(jnp.int32, (N_META, ATTN_BLOCK), 1)
    c = lax.broadcasted_iota(jnp.int32, (ATTN_BLOCK, ATTN_BLOCK), 0)
    qi = lax.broadcasted_iota(jnp.int32, (ATTN_BLOCK, ATTN_BLOCK), 1)
    slopes = _alibi_slopes(ATTN_HEADS) * np.float32(LOG2E)
    qnw = qnw_ref[...] * (ATTN_HEAD_DIM ** -0.5 * LOG2E)
    knw = knw_ref[...]

    def kv_rows(sub, cols):
        cur = kvc_ref[sub * ATTN_BLOCK:(sub + 1) * ATTN_BLOCK, cols]
        prev = kvp_ref[:, cols] if sub == 0 else kvc_ref[(sub - 1) * ATTN_BLOCK:sub * ATTN_BLOCK, cols]
        return jnp.concatenate([kvm_ref[:, cols], prev, cur], axis=0)

    units = [(sub, h) for sub in range(ATTN_STEP_BLOCKS) for h in range(ATTN_KV_HEADS)]
    s_d, v_d, clip_d = {}, {}, {}
    for sub in range(ATTN_STEP_BLOCKS):
        nref = pl.program_id(1) * ATTN_STEP_BLOCKS + sub - META_BLOCK
        dist_m = nref * ATTN_BLOCK + qm - (META_OFFSET + jm)
        clip_d[sub] = jnp.concatenate(
            [clip(dist_m, dist_m >= 0),
             clip(ATTN_BLOCK + qi - c, jnp.logical_and(c > qi, nref >= 2)),
             clip(qi - c, jnp.logical_and(c <= qi, nref >= 1))], axis=0)
    for (sub, h) in units:
        rows = slice(sub * ATTN_BLOCK, (sub + 1) * ATTN_BLOCK)
        k_ext = kv_rows(sub, slice(h * ATTN_HEAD_DIM, (h + 1) * ATTN_HEAD_DIM))
        k_ext = _rms_rows(k_ext.astype(F32), knw).astype(BF16)
        v_d[(sub, h)] = kv_rows(sub, slice(ATTN_KV_WIDTH + h * ATTN_HEAD_DIM,
                                           ATTN_KV_WIDTH + (h + 1) * ATTN_HEAD_DIM))
        group_cols = slice(h * ATTN_GROUPS * ATTN_HEAD_DIM, (h + 1) * ATTN_GROUPS * ATTN_HEAD_DIM)
        q_t = q_ref[rows, group_cols].astype(F32).T
        qn = []
        for g in range(ATTN_GROUPS):
            qg = q_t[g * ATTN_HEAD_DIM:(g + 1) * ATTN_HEAD_DIM]
            inv = lax.rsqrt(jnp.mean(qg * qg, axis=0, keepdims=True) + NORM_EPS)
            qn.append((qg * inv * qnw).astype(BF16))
        s_d[(sub, h)] = jnp.dot(k_ext, jnp.concatenate(qn, axis=1),
                                preferred_element_type=F32)

    for (sub, h) in units:
        rows = slice(sub * ATTN_BLOCK, (sub + 1) * ATTN_BLOCK)
        p_l, rinv_l = [], []
        for g in range(ATTN_GROUPS):
            hq = h * ATTN_GROUPS + g
            s = s_d[(sub, h)][:, g * ATTN_BLOCK:(g + 1) * ATTN_BLOCK] - float(slopes[hq]) * clip_d[sub]
            sink = sink_ref[hq] * LOG2E
            m = jnp.maximum(jnp.max(s, axis=0, keepdims=True), sink)
            p = jnp.exp2(s - m)
            rinv_l.append(1.0 / (jnp.sum(p, axis=0, keepdims=True) + jnp.exp2(sink - m)))
            p_l.append(p.astype(BF16))
        o_t = lax.dot_general(v_d[(sub, h)], jnp.concatenate(p_l, axis=1), (((0,), (0,)), ((), ())),
                              preferred_element_type=F32)
        o_t = jnp.concatenate(
            [o_t[:, g * ATTN_BLOCK:(g + 1) * ATTN_BLOCK] * rinv_l[g] for g in range(ATTN_GROUPS)],
            axis=0)
        group_cols = slice(h * ATTN_GROUPS * ATTN_HEAD_DIM, (h + 1) * ATTN_GROUPS * ATTN_HEAD_DIM)
        o_ref[rows, group_cols] = (o_t.T * _silu(gate_ref[rows, group_cols].astype(F32))).astype(BF16)


def _attention(q, kv, gate, sinks, qnw, knw):
    b, lp, _ = q.shape
    step_rows = ATTN_STEP_BLOCKS * ATTN_BLOCK
    kv_width = 2 * ATTN_KV_WIDTH
    wide_block = (None, step_rows, ATTN_WIDTH)
    qnw_lanes = jnp.broadcast_to(qnw.reshape(ATTN_HEAD_DIM, 1), (ATTN_HEAD_DIM, LANES))
    return pl.pallas_call(
        _attn_kernel,
        grid=(b, lp // step_rows),
        in_specs=[pl.BlockSpec(memory_space=pltpu.SMEM),
                  pl.BlockSpec(wide_block, lambda bi, n: (bi, n, 0)),
                  pl.BlockSpec((None, N_META, kv_width), lambda bi, n: (bi, PAD // N_META, 0)),
                  pl.BlockSpec((None, ATTN_BLOCK, kv_width),
                               lambda bi, n: (bi, jnp.maximum(n * ATTN_STEP_BLOCKS - 1, 0), 0)),
                  pl.BlockSpec((None, step_rows, kv_width), lambda bi, n: (bi, n, 0)),
                  pl.BlockSpec(wide_block, lambda bi, n: (bi, n, 0)),
                  pl.BlockSpec((ATTN_HEAD_DIM, LANES), lambda bi, n: (0, 0)),
                  pl.BlockSpec((1, ATTN_HEAD_DIM), lambda bi, n: (0, 0))],
        out_specs=pl.BlockSpec(wide_block, lambda bi, n: (bi, n, 0)),
        out_shape=jax.ShapeDtypeStruct((b, lp, ATTN_WIDTH), BF16),
        compiler_params=pltpu.CompilerParams(dimension_semantics=("parallel", "parallel"),
                                             vmem_limit_bytes=VMEM_LIMIT),
        name="attn_core",
    )(sinks, q, kv, kv, kv, gate, qnw_lanes, knw)


def _dn_in_kernel(head_ref, og0_ref, xnext_ref, ognext_ref, wout_ref, nw_ref, wqkv_ref, wz_ref, wba_ref,
                  convw_ref, alog_ref, dtb_ref, tri_ref,
                  h1_ref, q_ref, k_ref, vb_ref, kbe_ref, qe_ref, z_ref, bg_ref,
                  buf_ref, ybuf_ref, halo_ref, h1s_ref, xn_ref):
    i = pl.program_id(1)
    halo = SUBLANES

    @pl.when(i == 0)
    def _():
        halo_ref[...] = jnp.zeros(halo_ref.shape, F32)
        h1s_ref[...] = head_ref[...] + jnp.dot(og0_ref[...], wout_ref[:, :h1s_ref.shape[1]],
                                               preferred_element_type=F32)
        xn_ref[...] = _rms_rows(h1s_ref[...], nw_ref[...]).astype(BF16)

    h1_ref[...] = h1s_ref[...]
    xn = xn_ref[...]

    def z_part(c0):
        def run():
            z_ref[:, c0:c0 + FILL_COLS] = jnp.dot(xn, wz_ref[:, c0:c0 + FILL_COLS],
                                                  preferred_element_type=F32).astype(BF16)
        return run

    def next_h1_part(c0):
        def run():
            h1s_ref[:, c0:c0 + FILL_COLS] = xnext_ref[:, c0:c0 + FILL_COLS] + jnp.dot(
                ognext_ref[...], wout_ref[:, c0:c0 + FILL_COLS], preferred_element_type=F32)
        return run

    fillers = ([z_part(c0) for c0 in range(0, DN_VALUE_WIDTH, FILL_COLS)]
               + [next_h1_part(c0) for c0 in range(0, h1s_ref.shape[1], FILL_COLS)])

    ba = jnp.dot(xn, wba_ref[...], preferred_element_type=F32)
    lane = lax.broadcasted_iota(jnp.int32, ba.shape, 1)
    row = i * ROW_TILE + lax.broadcasted_iota(jnp.int32, ba.shape, 0)
    x = ba + dtb_ref[...]
    softplus = jnp.maximum(x, 0.0) + jnp.log1p(jnp.exp(-jnp.abs(x)))
    g = -jnp.exp(alog_ref[...]) * softplus
    bg = jnp.where(lane < DN_V_HEADS, jax.nn.sigmoid(ba), g)
    bg = jnp.where(row >= PAD, bg, 0.0)
    csum = jnp.dot(tri_ref[...], bg, preferred_element_type=F32, precision=lax.Precision.HIGHEST)
    bg = jnp.where(lane < DN_V_HEADS, bg, csum)
    bg_ref[...] = bg
    egc = jnp.exp(bg)
    reps = DN_V_HEADS // DN_K_HEADS

    q_scale = DN_HEAD_DIM ** -0.5
    rows_per_phase = ROW_TILE // CONV_PHASES
    slabs_per_chunk = CONV_COLS // LANES
    for cj in range(DN_CONV_WIDTH // CONV_COLS):
        u = jnp.dot(xn, wqkv_ref[:, cj * CONV_COLS:(cj + 1) * CONV_COLS],
                    preferred_element_type=F32)
        if cj >= 1 and fillers:
            fillers.pop(0)()
        for sl in range(slabs_per_chunk):
            slab = cj * slabs_per_chunk + sl
            col0 = slab * LANES
            buf_ref[sl, 0:halo, :] = halo_ref[slab]
            buf_ref[sl, halo:halo + ROW_TILE, :] = u[:, sl * LANES:(sl + 1) * LANES]
            w4 = convw_ref[:, col0:col0 + LANES]
            taps = {}
            for start in range(halo - (DN_CONV - 1), halo + CONV_PHASES):
                taps[start] = buf_ref[sl, pl.ds(start, rows_per_phase, stride=CONV_PHASES), :]
            for a in range(CONV_PHASES):
                acc = None
                for j in range(DN_CONV):
                    term = w4[DN_CONV - 1 - j:DN_CONV - j] * taps[halo + a - j]
                    acc = term if acc is None else acc + term
                ybuf_ref[sl, pl.ds(a, rows_per_phase, stride=CONV_PHASES), :] = _silu(acc)
            halo_ref[slab] = buf_ref[sl, ROW_TILE:ROW_TILE + halo, :]
            y = ybuf_ref[sl]
            if col0 < DN_KEY_WIDTH:
                kh = slab
                qn = y * (lax.rsqrt(jnp.sum(y * y, axis=-1, keepdims=True) + NORM_EPS) * q_scale)
                q_ref[:, col0:col0 + LANES] = qn.astype(BF16)
                for h in range(reps * kh, reps * (kh + 1)):
                    gl = DN_V_HEADS + h
                    qe_ref[:, h * LANES:(h + 1) * LANES] = (qn * egc[:, gl:gl + 1]).astype(BF16)
            elif col0 < 2 * DN_KEY_WIDTH:
                kh = slab - DN_K_HEADS
                kn = y * lax.rsqrt(jnp.sum(y * y, axis=-1, keepdims=True) + NORM_EPS)
                k_ref[:, kh * LANES:(kh + 1) * LANES] = kn.astype(BF16)
                for h in range(reps * kh, reps * (kh + 1)):
                    gl = DN_V_HEADS + h
                    kbe_ref[:, h * LANES:(h + 1) * LANES] = (
                        kn * (bg[:, h:h + 1] * egc[:, gl:gl + 1])).astype(BF16)
            else:
                h = slab - 2 * DN_K_HEADS
                vb_ref[:, h * LANES:(h + 1) * LANES] = (y * bg[:, h:h + 1]).astype(BF16)

    for run in fillers:
        run()
    xn_ref[...] = _rms_rows(h1s_ref[...], nw_ref[...]).astype(BF16)


def _dn_in(head, x, og, w_out, norm_w, w_qkv, w_z, w_ba, conv_w, alog_row, dtb_row):
    b, seq, d = x.shape
    lp = seq + ROW_TILE
    nt = lp // ROW_TILE
    n_x = seq // ROW_TILE
    idx = np.arange(ROW_TILE)
    tri = jnp.asarray((idx[:, None] >= idx[None, :])
                      & (idx[:, None] // DN_CHUNK == idx[None, :] // DN_CHUNK), F32)
    row_block = lambda width: pl.BlockSpec((None, ROW_TILE, width), lambda bi, i: (bi, i, 0))
    full = lambda shape: pl.BlockSpec(shape, lambda bi, i: (0,) * len(shape),
                                      pipeline_mode=pl.Buffered(1))
    return pl.pallas_call(
        _dn_in_kernel,
        grid=(b, nt),
        in_specs=[
            pl.BlockSpec((None, ROW_TILE, d), lambda bi, i: (bi, 0, 0)),
            pl.BlockSpec((None, ROW_TILE, og.shape[-1]), lambda bi, i: (bi, 0, 0)),
            pl.BlockSpec((None, ROW_TILE, d), lambda bi, i: (bi, jnp.minimum(i, n_x - 1), 0)),
            pl.BlockSpec((None, ROW_TILE, og.shape[-1]), lambda bi, i: (bi, jnp.minimum(i + 1, nt - 1), 0)),
            full(w_out.shape), full((1, d)), full(w_qkv.shape), full(w_z.shape),
            full(w_ba.shape), full(conv_w.shape), full((1, LANES)), full((1, LANES)), full(tri.shape)],
        out_specs=[row_block(d), row_block(DN_KEY_WIDTH), row_block(DN_KEY_WIDTH),
                   row_block(DN_VALUE_WIDTH), row_block(DN_VALUE_WIDTH), row_block(DN_VALUE_WIDTH),
                   row_block(DN_VALUE_WIDTH), row_block(LANES)],
        out_shape=[jax.ShapeDtypeStruct((b, lp, d), F32),
                   jax.ShapeDtypeStruct((b, lp, DN_KEY_WIDTH), BF16),
                   jax.ShapeDtypeStruct((b, lp, DN_KEY_WIDTH), BF16),
                   jax.ShapeDtypeStruct((b, lp, DN_VALUE_WIDTH), BF16),
                   jax.ShapeDtypeStruct((b, lp, DN_VALUE_WIDTH), BF16),
                   jax.ShapeDtypeStruct((b, lp, DN_VALUE_WIDTH), BF16),
                   jax.ShapeDtypeStruct((b, lp, DN_VALUE_WIDTH), BF16),
                   jax.ShapeDtypeStruct((b, lp, LANES), F32)],
        scratch_shapes=[pltpu.VMEM((CONV_COLS // LANES, ROW_TILE + SUBLANES, LANES), F32),
                        pltpu.VMEM((CONV_COLS // LANES, ROW_TILE, LANES), F32),
                        pltpu.VMEM((DN_CONV_WIDTH // LANES, SUBLANES, LANES), F32),
                        pltpu.VMEM((ROW_TILE, d), F32),
                        pltpu.VMEM((ROW_TILE, d), BF16)],
        compiler_params=pltpu.CompilerParams(dimension_semantics=("parallel", "arbitrary"),
                                             vmem_limit_bytes=VMEM_LIMIT),
        name="dn_in",
    )(head, og, x, og, w_out, norm_w, w_qkv, w_z, w_ba, conv_w, alog_row, dtb_row, tri)


def _dn_core_kernel(q_ref, k_ref, vb_ref, kbe_ref, qe_ref, bg_ref, o_ref, s_ref):
    @pl.when(pl.program_id(0) == 0)
    def _():
        s_ref[...] = jnp.zeros(s_ref.shape, F32)

    n_seq = q_ref.shape[0]
    reps = DN_V_HEADS // DN_K_HEADS
    ri = lax.broadcasted_iota(jnp.int32, (DN_CHUNK, DN_CHUNK), 0)
    ci = lax.broadcasted_iota(jnp.int32, (DN_CHUNK, DN_CHUNK), 1)
    causal = ri >= ci
    neg_strict = -(ri > ci).astype(F32)
    eye = (ri == ci).astype(F32)
    keep_t = (lax.broadcasted_iota(jnp.int32, (DN_CHUNK, 2 * DN_CHUNK), 1) >= DN_CHUNK).astype(F32)

    insts = [(b, c) for c in range(DN_STEP_CHUNKS) for b in range(n_seq)]

    qk_d, kk_d, kt_d = {}, {}, {}
    for (b, c) in insts:
        rows = slice(c * DN_CHUNK, (c + 1) * DN_CHUNK)
        for kh in range(DN_K_HEADS):
            kcols = slice(kh * DN_HEAD_DIM, (kh + 1) * DN_HEAD_DIM)
            k = k_ref[b, rows, kcols]
            qk_kk = _bdot_nt(jnp.concatenate([q_ref[b, rows, kcols], k], axis=0), k)
            key = (b, c, kh)
            qk_d[key], kk_d[key] = qk_kk[:DN_CHUNK], qk_kk[DN_CHUNK:]
            kt_d[key] = k.astype(F32).T

    lhs2_d, y_d, rhs_d, qe_d, g_last_d = {}, {}, {}, {}, {}
    for (b, c) in insts:
        rows = slice(c * DN_CHUNK, (c + 1) * DN_CHUNK)
        bg = bg_ref[b, rows, :]
        bg_t = bg.T
        for h in range(DN_V_HEADS):
            key, kkey = (b, c, h), (b, c, h // reps)
            gl = DN_V_HEADS + h
            beta_col = bg[:, h:h + 1]
            gc_col = bg[:, gl:gl + 1]
            gc_row = bg_t[gl:gl + 1, :]
            decay = jnp.exp(jnp.where(causal, gc_col - gc_row, NEG_BIG))
            g_last = gc_col[DN_CHUNK - 1:DN_CHUNK, :]
            lhs2_d[key] = jnp.concatenate(
                [(qk_d[kkey] * decay).astype(BF16),
                 (kt_d[kkey] * jnp.exp(g_last - gc_row)).astype(BF16)], axis=0)
            n1 = kk_d[kkey] * decay * (neg_strict * beta_col)
            y_d[key] = jnp.concatenate([n1, eye], axis=1)
            hcols = slice(h * DN_HEAD_DIM, (h + 1) * DN_HEAD_DIM)
            rhs_d[key] = jnp.concatenate([vb_ref[b, rows, hcols], kbe_ref[b, rows, hcols]], axis=1)
            qe_d[key] = qe_ref[b, rows, hcols]
            g_last_d[key] = g_last

    chunk_heads = [(b, h) for b in range(n_seq) for h in range(DN_V_HEADS)]
    u_d, w_d, ws_qs_d = {}, {}, {}
    s_d = {(b, h): s_ref[b, h] for (b, h) in chunk_heads}

    def inverse_round(c):
        for (b, h) in chunk_heads:
            key = (b, c, h)
            r = _bdot(y_d[key][:, :DN_CHUNK], y_d[key])
            y_d[key] = r + y_d[key] * keep_t

    def solve(c):
        for (b, h) in chunk_heads:
            key = (b, c, h)
            uw = _bdot(y_d[key][:, DN_CHUNK:], rhs_d[key])
            u_d[key], w_d[key] = uw[:, :DN_HEAD_DIM], uw[:, DN_HEAD_DIM:].astype(BF16)

    def state_read(c):
        for (b, h) in chunk_heads:
            key = (b, c, h)
            ws_qs_d[key] = _bdot(jnp.concatenate([w_d[key], qe_d[key]], axis=0), s_d[(b, h)])

    def state_update(c):
        rows = slice(c * DN_CHUNK, (c + 1) * DN_CHUNK)
        for (b, h) in chunk_heads:
            key = (b, c, h)
            v_new = u_d[key] - ws_qs_d[key][:DN_CHUNK]
            av = _bdot(lhs2_d[key], v_new)
            o = ws_qs_d[key][DN_CHUNK:] + av[:DN_CHUNK]
            o_ref[b, rows, h * DN_HEAD_DIM:(h + 1) * DN_HEAD_DIM] = o.astype(BF16)
            s_d[(b, h)] = s_d[(b, h)] * jnp.exp(g_last_d[key]) + av[DN_CHUNK:]

    for c in range(DN_STEP_CHUNKS):
        between = [lambda: state_read(c - 1), lambda: state_update(c - 1)] if c > 0 else []
        for rnd in range(6):
            inverse_round(c)
            if rnd in (0, 2) and between:
                between.pop(0)()
        solve(c)
    state_read(DN_STEP_CHUNKS - 1)
    state_update(DN_STEP_CHUNKS - 1)
    for (b, h), s_new in s_d.items():
        s_ref[b, h] = s_new


def _dn_core(q, k, vb, kbe, qe, bg):
    b, lp, _ = q.shape
    rows = DN_STEP_CHUNKS * DN_CHUNK
    blk = lambda width: pl.BlockSpec((b, rows, width), lambda i: (0, i, 0))
    return pl.pallas_call(
        _dn_core_kernel,
        grid=(lp // rows,),
        in_specs=[blk(DN_KEY_WIDTH), blk(DN_KEY_WIDTH), blk(DN_VALUE_WIDTH), blk(DN_VALUE_WIDTH),
                  blk(DN_VALUE_WIDTH), blk(LANES)],
        out_specs=blk(DN_VALUE_WIDTH),
        out_shape=jax.ShapeDtypeStruct((b, lp, DN_VALUE_WIDTH), BF16),
        scratch_shapes=[pltpu.VMEM((b, DN_V_HEADS, DN_HEAD_DIM, DN_HEAD_DIM), F32)],
        compiler_params=pltpu.CompilerParams(dimension_semantics=("arbitrary",),
                                             vmem_limit_bytes=VMEM_LIMIT),
        name="dn_core",
    )(q, k, vb, kbe, qe, bg)


def _dn_out_kernel(h_ref, o_ref, z_ref, onw_ref, w_ref, out_ref):
    onw = onw_ref[...]
    acc = h_ref[...]
    heads_per_chunk = DN_OUT_K_CHUNK // DN_HEAD_DIM
    for c in range(DN_VALUE_WIDTH // DN_OUT_K_CHUNK):
        parts = []
        for hh in range(c * heads_per_chunk, (c + 1) * heads_per_chunk):
            cols = slice(hh * DN_HEAD_DIM, (hh + 1) * DN_HEAD_DIM)
            y = _rms_rows(o_ref[:, cols].astype(F32), onw) * _silu(z_ref[:, cols].astype(F32))
            parts.append(y.astype(BF16))
        acc = acc + jnp.dot(jnp.concatenate(parts, axis=1),
                            w_ref[c * DN_OUT_K_CHUNK:(c + 1) * DN_OUT_K_CHUNK, :acc.shape[1]],
                            preferred_element_type=F32)
    out_ref[...] = acc


def _dn_out(h, o, z, onw, w_out, seq):
    b, lp, d = h.shape
    skip = (lp - seq) // ROW_TILE
    in_block = lambda width: pl.BlockSpec((None, ROW_TILE, width), lambda bi, i: (bi, i + skip, 0))
    return pl.pallas_call(
        _dn_out_kernel,
        grid=(b, seq // ROW_TILE),
        in_specs=[in_block(d), in_block(DN_VALUE_WIDTH), in_block(DN_VALUE_WIDTH),
                  pl.BlockSpec((1, DN_HEAD_DIM), lambda bi, i: (0, 0)),
                  pl.BlockSpec(w_out.shape, lambda bi, i: (0, 0))],
        out_specs=pl.BlockSpec((None, ROW_TILE, d), lambda bi, i: (bi, i, 0)),
        out_shape=jax.ShapeDtypeStruct((b, seq, d), F32),
        compiler_params=pltpu.CompilerParams(dimension_semantics=("parallel", "parallel"),
                                             vmem_limit_bytes=VMEM_LIMIT),
        name="dn_out",
    )(h, o, z, onw, w_out)


def kernel(x, meta_tokens, attn_norm_w, attn_w_in, attn_q_norm_w, attn_k_norm_w, attn_sinks,
           attn_w_out, dn_norm_w, dn_w_in, dn_conv_w, dn_a_log, dn_dt_bias, dn_o_norm_w, dn_w_out):
    b, seq, d = x.shape
    assert seq % ROW_TILE == 0 and attn_norm_w.shape[0] == 1 and dn_norm_w.shape[0] == 1
    assert (seq + ROW_TILE) % (ATTN_STEP_BLOCKS * ATTN_BLOCK) == 0
    assert (seq + ROW_TILE) % (DN_STEP_CHUNKS * DN_CHUNK) == 0
    meta = jnp.broadcast_to(meta_tokens.astype(x.dtype)[None], (b, N_META, d))
    head = jnp.concatenate([jnp.zeros((b, PAD, d), x.dtype), meta], axis=1)

    q, kv, gate = _attn_in(head, x, attn_norm_w[0][None], attn_w_in[0].astype(BF16))
    og = _attention(q, kv, gate, attn_sinks[0], attn_q_norm_w[0][None], attn_k_norm_w[0][None])

    w_in = dn_w_in[0]
    w_qkv = _mxu_weight(w_in[:, :DN_CONV_WIDTH])
    w_z = _mxu_weight(w_in[:, DN_CONV_WIDTH:DN_CONV_WIDTH + DN_VALUE_WIDTH])
    n_ba = 2 * DN_V_HEADS
    w_ba = jnp.pad(w_in[:, DN_CONV_WIDTH + DN_VALUE_WIDTH:], ((0, 0), (0, LANES - n_ba))).astype(BF16)
    lane_pad = lambda t: jnp.pad(t[None], ((0, 0), (DN_V_HEADS, LANES - n_ba)))
    h1, qd, kd, vbd, kbed, qed, zd, bg = _dn_in(head, x, og, _mxu_weight(attn_w_out[0]), dn_norm_w[0][None],
                                    w_qkv, w_z, w_ba, dn_conv_w[0],
                                    lane_pad(dn_a_log[0]), lane_pad(dn_dt_bias[0]))
    od = _dn_core(qd, kd, vbd, kbed, qed, bg)
    return _dn_out(h1, od, zd, dn_o_norm_w[0][None], _mxu_weight(dn_w_out[0]), seq)
```
